```python
import math
import jax
import jax.numpy as jnp
from jax import lax
import numpy as np

D_MODEL = 1024
BATCH = 16
SEQ = 2048
DEPTH = 2
DEC_BATCH = 32
DEC_SEQ = 8
PAST_LEN = 16384
PAGE_SIZE = 128

HEAD_DIM = 64
ROPE_THETA = 500000.0
A_HEADS = 8
A_KV_HEADS = 2
IDX_HEADS = 8
IDX_DIM = 64
TOPK_MAX = 256
GM_GROUPS = 8
GM_WIDTH = GM_GROUPS * HEAD_DIM
CHUNK = 128
C_HEADS = 4
D_HEADS = 8
FORGET_BIAS_INIT = 2.0
D_FF = -(-(8 * D_MODEL) // (3 * 256)) * 256
BLOCK_Q = 128
N_EVEN = (DEPTH + 1) // 2
N_ODD = DEPTH // 2
EPS = 1e-6
AB_SIZES = (A_HEADS * HEAD_DIM, A_KV_HEADS * HEAD_DIM, A_KV_HEADS * HEAD_DIM,
            IDX_HEADS * IDX_DIM, IDX_DIM, IDX_HEADS, GM_WIDTH, GM_WIDTH)
CD_SIZES = (C_HEADS * 2 * HEAD_DIM,) * 3 + (D_HEADS * HEAD_DIM,) * 3 + (D_HEADS,)
AB_IN = sum(AB_SIZES)
CD_IN = sum(CD_SIZES)
AB_OUT = A_HEADS * HEAD_DIM + GM_WIDTH
CD_OUT = C_HEADS * 2 * HEAD_DIM + D_HEADS * HEAD_DIM
F32 = jnp.float32

kernel_name = 'hybrid_dsa_gmlp_diff_fox_decode_step'


def rms_norm(x, g):
    xf = x.astype(F32)
    y = xf * lax.rsqrt(jnp.mean(xf * xf, axis=-1, keepdims=True) + EPS)
    return (y * g.astype(F32)).astype(x.dtype)


def layer_norm(x, g):
    xf = x.astype(F32)
    xc = xf - jnp.mean(xf, axis=-1, keepdims=True)
    y = xc * lax.rsqrt(jnp.mean(xc * xc, axis=-1, keepdims=True) + EPS)
    return (y * g.astype(F32)).astype(x.dtype)


def rope(x, pos):
    rot = x.shape[-1] // 4
    half = rot // 2
    inv = ROPE_THETA ** (-jnp.arange(half, dtype=F32) / half)
    ang = pos.astype(F32)[:, None] * inv[None, :]
    cos = jnp.cos(ang)[:, None, :]
    sin = jnp.sin(ang)[:, None, :]
    xr = x[..., :rot].astype(F32)
    x1, x2 = xr[..., :half], xr[..., half:]
    xr = jnp.concatenate([x1 * cos - x2 * sin, x2 * cos + x1 * sin], axis=-1).astype(x.dtype)
    return jnp.concatenate([xr, x[..., rot:]], axis=-1)


def _split(z, sizes):
    offs = np.cumsum(sizes)[:-1].tolist()
    return jnp.split(z, offs, axis=-1)


def take_rows(a, idx):
    return jax.vmap(lambda ab, ib: ab[ib])(a, idx)


def modulation(c, w, b):
    m = jax.nn.silu(c) @ w + b
    m = m.reshape(c.shape[0], 6, 1, D_MODEL)
    return [m[:, i] for i in range(6)]


def swiglu(h, wg, wu, wd):
    return (jax.nn.silu(h @ wg) * (h @ wu)) @ wd


def dsa_select(qi, wi, ki_all, pos_q, n_sel):
    L = ki_all.shape[1]
    s = jnp.einsum('bthd,bld->bthl', qi, ki_all).astype(F32)
    score = jnp.einsum('bthl,bth->btl', jax.nn.relu(s), wi.astype(F32)) * (IDX_HEADS * IDX_DIM) ** -0.5
    allowed = jnp.arange(L)[None, :] <= pos_q[:, None]
    score = jnp.where(allowed[None], score, -jnp.inf)
    _, idx = lax.top_k(score, n_sel)
    valid = idx <= pos_q[None, :, None]
    return idx, valid


def dsa_attend(q, k_sel, v_sel, valid):
    B_, T = q.shape[:2]
    rep = A_HEADS // A_KV_HEADS
    qg = q.reshape(B_, T, A_KV_HEADS, rep, HEAD_DIM)
    s = jnp.einsum('btgrd,btngd->btgrn', qg, k_sel).astype(F32) * HEAD_DIM ** -0.5
    s = jnp.where(valid[:, :, None, None, :], s, -jnp.inf)
    p = jax.nn.softmax(s, axis=-1).astype(v_sel.dtype)
    o = jnp.einsum('btgrn,btngd->btgrd', p, v_sel)
    return o.reshape(B_, T, A_HEADS * HEAD_DIM)


def gather_rows(cache, li, page_table, past_len, new, idx):
    Bd, T = new.shape[:2]
    ip = jnp.minimum(idx, past_len - 1)
    page = jnp.take_along_axis(page_table, (ip // PAGE_SIZE).reshape(Bd, -1), axis=1).reshape(idx.shape)
    from_pool = cache[li, page, ip % PAGE_SIZE].astype(new.dtype)
    from_new = take_rows(new, jnp.clip(idx - past_len, 0, T - 1))
    sel = (idx < past_len).reshape(idx.shape + (1,) * (new.ndim - 2))
    return jnp.where(sel, from_pool, from_new)


def chunk_gate(u, gv, w_s, b_s):
    B_, T, _ = u.shape
    pad = (-T) % CHUNK
    nc = (T + pad) // CHUNK
    gvc = jnp.pad(gv, ((0, 0), (0, pad), (0, 0))).reshape(B_, nc, CHUNK, GM_GROUPS, HEAD_DIM)
    w = w_s * jnp.tril(jnp.ones((CHUNK, CHUNK), dtype=w_s.dtype))
    mix = jnp.einsum('gts,bcsgd->bctgd', w, gvc) + b_s.T[None, None, :, :, None]
    mix = mix.reshape(B_, nc * CHUNK, GM_WIDTH)[:, :T]
    return u * mix


def ab_project(h, pos, w_in, gm_g):
    B_, T, _ = h.shape
    q, k, v, qi, ki, wi, u, gv = _split(h @ w_in, AB_SIZES)
    q = rope(q.reshape(B_, T, A_HEADS, HEAD_DIM), pos)
    k = rope(k.reshape(B_, T, A_KV_HEADS, HEAD_DIM), pos)
    v = v.reshape(B_, T, A_KV_HEADS, HEAD_DIM)
    qi = rope(qi.reshape(B_, T, IDX_HEADS, IDX_DIM), pos)
    ki = rope(ki.reshape(B_, T, 1, IDX_DIM), pos)[:, :, 0]
    u = jax.nn.gelu(u)
    gv = layer_norm(jax.nn.gelu(gv), gm_g)
    return q, k, v, qi, ki, wi, u, gv


def ab_prompt(h, w_in, w_out, gm_g, gm_w, gm_b):
    B_, S, _ = h.shape
    pos = jnp.arange(S)
    q, k, v, qi, ki, wi, u, gv = ab_project(h, pos, w_in, gm_g)
    n_sel = min(TOPK_MAX, S // 4)
    nb = S // BLOCK_Q

    def blocks(a):
        return a.reshape((B_, nb, BLOCK_Q) + a.shape[2:]).swapaxes(0, 1)

    def one(args):
        qb, qib, wib, i = args
        pos_q = i * BLOCK_Q + jnp.arange(BLOCK_Q)
        idx, valid = dsa_select(qib, wib, ki, pos_q, n_sel)
        return dsa_attend(qb, take_rows(k, idx), take_rows(v, idx), valid)

    oa = lax.map(one, (blocks(q), blocks(qi), blocks(wi), jnp.arange(nb)))
    oa = oa.swapaxes(0, 1).reshape(B_, S, A_HEADS * HEAD_DIM)
    ob = chunk_gate(u, gv, gm_w, gm_b)
    y = jnp.concatenate([oa, ob], axis=-1) @ w_out
    return y, (k, v, ki)


def ab_sample(h, past_len, page_table, cache_a_k, cache_a_v, cache_a_idx, li, w_in, w_out, gm_g, gm_w, gm_b):
    Bd, T, _ = h.shape
    pos = past_len + jnp.arange(T)
    q, k, v, qi, ki, wi, u, gv = ab_project(h, pos, w_in, gm_g)
    ki_past = cache_a_idx[li, page_table].reshape(Bd, past_len, IDX_DIM).astype(ki.dtype)
    n_sel = min(TOPK_MAX, (past_len + T) // 4)
    idx, valid = dsa_select(qi, wi, jnp.concatenate([ki_past, ki], axis=1), pos, n_sel)
    k_sel = gather_rows(cache_a_k, li, page_table, past_len, k, idx)
    v_sel = gather_rows(cache_a_v, li, page_table, past_len, v, idx)
    oa = dsa_attend(q, k_sel, v_sel, valid)
    ob = chunk_gate(u, gv, gm_w, gm_b)
    y = jnp.concatenate([oa, ob], axis=-1) @ w_out
    return y, (k, v, ki, gv)


def cd_project(h, pos, w_in, fox_b):
    B_, T, _ = h.shape
    cq, ck, cv, dq, dk, dv, df = _split(h @ w_in, CD_SIZES)
    cq = rope(cq.reshape(B_, T, 2 * C_HEADS, HEAD_DIM), pos).reshape(B_, T, C_HEADS, 2, HEAD_DIM)
    ck = rope(ck.reshape(B_, T, 2 * C_HEADS, HEAD_DIM), pos).reshape(B_, T, C_HEADS, 2, HEAD_DIM)
    cv = cv.reshape(B_, T, C_HEADS, 2 * HEAD_DIM)
    dq = dq.reshape(B_, T, D_HEADS, HEAD_DIM)
    dk = dk.reshape(B_, T, D_HEADS, HEAD_DIM)
    dv = dv.reshape(B_, T, D_HEADS, HEAD_DIM)
    logf = jax.nn.log_sigmoid(df.astype(F32) + fox_b.astype(F32))
    return cq, ck, cv, dq, dk, dv, logf


def diff_lambda(lq1, lk1, lq2, lk2, lam_init):
    d1 = jnp.sum(lq1.astype(F32) * lk1.astype(F32))
    d2 = jnp.sum(lq2.astype(F32) * lk2.astype(F32))
    return jnp.exp(d1) - jnp.exp(d2) + lam_init


def diff_finish(o, g_sub, lam_init):
    B_, T = o.shape[:2]
    return (rms_norm(o, g_sub) * (1.0 - lam_init)).reshape(B_, T, C_HEADS * 2 * HEAD_DIM)


def diff_prompt(q, k, v, lam):
    B_, S = q.shape[:2]
    nb = S // BLOCK_Q
    pos_k = jnp.arange(S)
    qb = q.reshape((B_, nb, BLOCK_Q) + q.shape[2:]).swapaxes(0, 1)

    def one(args):
        qblk, i = args
        pos_q = i * BLOCK_Q + jnp.arange(BLOCK_Q)
        s = jnp.einsum('bqhcd,bkhcd->bhcqk', qblk, k).astype(F32) * HEAD_DIM ** -0.5
        s = jnp.where(pos_k[None, :] <= pos_q[:, None], s, -jnp.inf)
        p = jax.nn.softmax(s, axis=-1)
        w = (p[:, :, 0] - lam * p[:, :, 1]).astype(v.dtype)
        return jnp.einsum('bhqk,bkhe->bqhe', w, v)

    o = lax.map(one, (qb, jnp.arange(nb)))
    return o.swapaxes(0, 1).reshape(B_, S, C_HEADS, 2 * HEAD_DIM)


def diff_sample(q, k, v, kp, vp, lam):
    T = q.shape[1]
    P = kp.shape[1]
    sc = HEAD_DIM ** -0.5
    s_p = jnp.einsum('bqhcd,bkhcd->bhcqk', q, kp).astype(F32) * sc
    s_n = jnp.einsum('bqhcd,bkhcd->bhcqk', q, k).astype(F32) * sc
    s_n = jnp.where(jnp.tril(jnp.ones((T, T), dtype=bool)), s_n, -jnp.inf)
    p = jax.nn.softmax(jnp.concatenate([s_p, s_n], axis=-1), axis=-1)
    w = (p[:, :, 0] - lam * p[:, :, 1]).astype(v.dtype)
    return jnp.einsum('bhqk,bkhe->bqhe', w[..., :P], vp) + jnp.einsum('bhqk,bkhe->bqhe', w[..., P:], v)


def fox_prompt(q, k, v, logf):
    B_, S, H, d = q.shape
    nb = S // BLOCK_Q
    cum = jnp.cumsum(logf, axis=1).transpose(0, 2, 1)
    pos_k = jnp.arange(S)
    qb = q.reshape(B_, nb, BLOCK_Q, H, d).swapaxes(0, 1)
    cb = cum.reshape(B_, H, nb, BLOCK_Q).transpose(2, 0, 1, 3)

    def one(args):
        qblk, cq, i = args
        pos_q = i * BLOCK_Q + jnp.arange(BLOCK_Q)
        s = jnp.einsum('bqhd,bkhd->bhqk', qblk, k).astype(F32) * d ** -0.5
        s = s + cq[..., None] - cum[:, :, None, :]
        s = jnp.where(pos_k[None, :] <= pos_q[:, None], s, -jnp.inf)
        p = jax.nn.softmax(s, axis=-1).astype(v.dtype)
        return jnp.einsum('bhqk,bkhd->bqhd', p, v)

    o = lax.map(one, (qb, cb, jnp.arange(nb)))
    return o.swapaxes(0, 1).reshape(B_, S, H * d)


def fox_sample(q, k, v, logf, kp, vp, logf_p):
    Bd, T, H, d = q.shape
    P = kp.shape[1]
    rev = lax.cumsum(logf_p, axis=1, reverse=True)
    suffix = jnp.concatenate([rev[:, 1:], jnp.zeros_like(rev[:, :1])], axis=1)
    cn = jnp.cumsum(logf, axis=1)
    bias_p = cn.transpose(0, 2, 1)[..., None] + suffix.transpose(0, 2, 1)[:, :, None, :]
    bias_n = (cn[:, :, None, :] - cn[:, None, :, :]).transpose(0, 3, 1, 2)
    s_p = jnp.einsum('bqhd,bkhd->bhqk', q, kp).astype(F32) * d ** -0.5 + bias_p
    s_n = jnp.einsum('bqhd,bkhd->bhqk', q, k).astype(F32) * d ** -0.5 + bias_n
    s_n = jnp.where(jnp.tril(jnp.ones((T, T), dtype=bool)), s_n, -jnp.inf)
    p = jax.nn.softmax(jnp.concatenate([s_p, s_n], axis=-1), axis=-1).astype(v.dtype)
    o = jnp.einsum('bhqk,bkhd->bqhd', p[..., :P], vp) + jnp.einsum('bhqk,bkhd->bqhd', p[..., P:], v)
    return o.reshape(Bd, T, H * d)


def cd_prompt(h, w_in, w_out, fox_b, lam, lam_init, g_sub):
    S = h.shape[1]
    cq, ck, cv, dq, dk, dv, logf = cd_project(h, jnp.arange(S), w_in, fox_b)
    oc = diff_finish(diff_prompt(cq, ck, cv, lam), g_sub, lam_init)
    od = fox_prompt(dq, dk, dv, logf)
    y = jnp.concatenate([oc, od.astype(oc.dtype)], axis=-1) @ w_out
    return y, (ck, cv, dk, dv, logf)


def cd_sample(h, past_len, page_table, cache_c_k, cache_c_v, cache_d_k, cache_d_v, cache_d_logf, li,
              w_in, w_out, fox_b, lam, lam_init, g_sub):
    Bd, T, _ = h.shape
    cq, ck, cv, dq, dk, dv, logf = cd_project(h, past_len + jnp.arange(T), w_in, fox_b)
    kp_c = cache_c_k[li, page_table].reshape(Bd, past_len, C_HEADS, 2, HEAD_DIM).astype(ck.dtype)
    vp_c = cache_c_v[li, page_table].reshape(Bd, past_len, C_HEADS, 2 * HEAD_DIM).astype(cv.dtype)
    kp_d = cache_d_k[li, page_table].reshape(Bd, past_len, D_HEADS, HEAD_DIM).astype(dk.dtype)
    vp_d = cache_d_v[li, page_table].reshape(Bd, past_len, D_HEADS, HEAD_DIM).astype(dv.dtype)
    lf_d = cache_d_logf[li, page_table].reshape(Bd, past_len, D_HEADS).astype(F32)
    oc = diff_finish(diff_sample(cq, ck, cv, kp_c, vp_c, lam), g_sub, lam_init)
    od = fox_sample(dq, dk, dv, logf, kp_d, vp_d, lf_d)
    y = jnp.concatenate([oc, od.astype(oc.dtype)], axis=-1) @ w_out
    return y, (ck, cv, dk, dv, logf)


def setup_inputs(seed: int = 0) -> dict:
    key = jax.random.key(seed)
    ks = jax.random.split(key, 48)
    cnt = [0]

    def nxt():
        k = ks[cnt[0]]
        cnt[0] += 1
        return k

    def nrm(shape, scale=1.0):
        return jax.random.normal(nxt(), shape, F32) * scale

    n_pages = PAST_LEN // PAGE_SIZE
    used = DEC_BATCH * n_pages
    n_pool = used + max(1, used // 4)
    page_table = jax.random.permutation(nxt(), n_pool)[:used].reshape(DEC_BATCH, n_pages).astype(jnp.int32)
    x_prompt = nrm((BATCH, SEQ, D_MODEL))
    x_sample = nrm((DEC_BATCH, DEC_SEQ, D_MODEL))
    cache_a_k = nrm((N_EVEN, n_pool, PAGE_SIZE, A_KV_HEADS, HEAD_DIM))
    cache_a_v = nrm((N_EVEN, n_pool, PAGE_SIZE, A_KV_HEADS, HEAD_DIM))
    cache_a_idx = nrm((N_EVEN, n_pool, PAGE_SIZE, IDX_DIM))
    cache_c_k = nrm((N_ODD, n_pool, PAGE_SIZE, C_HEADS, 2, HEAD_DIM))
    cache_c_v = nrm((N_ODD, n_pool, PAGE_SIZE, C_HEADS, 2 * HEAD_DIM))
    cache_d_k = nrm((N_ODD, n_pool, PAGE_SIZE, D_HEADS, HEAD_DIM))
    cache_d_v = nrm((N_ODD, n_pool, PAGE_SIZE, D_HEADS, HEAD_DIM))
    cache_d_logf = jax.nn.log_sigmoid(FORGET_BIAS_INIT + nrm((N_ODD, n_pool, PAGE_SIZE, D_HEADS)))
    c_prompt = nrm((BATCH, D_MODEL))
    c_sample = nrm((DEC_BATCH, D_MODEL))
    w_mod = nrm((DEPTH, D_MODEL, 6 * D_MODEL), 0.5 * D_MODEL ** -0.5)
    b_mod = nrm((DEPTH, 6 * D_MODEL), 0.01)
    g_norm = 1.0 + nrm((DEPTH, 4, D_MODEL), 0.01)
    w_in_ab = nrm((N_EVEN, D_MODEL, AB_IN), D_MODEL ** -0.5)
    w_out_ab = nrm((N_EVEN, AB_OUT, D_MODEL), AB_OUT ** -0.5)
    gm_norm_v = 1.0 + nrm((N_EVEN, GM_WIDTH), 0.01)
    gm_spatial_w = nrm((N_EVEN, GM_GROUPS, CHUNK, CHUNK), CHUNK ** -0.5)
    gm_spatial_b = 1.0 + nrm((N_EVEN, GM_GROUPS, CHUNK), 0.1)
    w_in_cd = nrm((N_ODD, D_MODEL, CD_IN), D_MODEL ** -0.5)
    w_out_cd = nrm((N_ODD, CD_OUT, D_MODEL), CD_OUT ** -0.5)
    fox_bias = FORGET_BIAS_INIT + nrm((N_ODD, D_HEADS), 0.1)
    lam_q1 = nrm((N_ODD, HEAD_DIM), 0.1)
    lam_k1 = nrm((N_ODD, HEAD_DIM), 0.1)
    lam_q2 = nrm((N_ODD, HEAD_DIM), 0.1)
    lam_k2 = nrm((N_ODD, HEAD_DIM), 0.1)
    g_subln = 1.0 + nrm((N_ODD, 2 * HEAD_DIM), 0.01)
    w_gate = nrm((DEPTH, D_MODEL, D_FF), D_MODEL ** -0.5)
    w_up = nrm((DEPTH, D_MODEL, D_FF), D_MODEL ** -0.5)
    w_down = nrm((DEPTH, D_FF, D_MODEL), D_FF ** -0.5)
    return {
        'x_prompt': x_prompt, 'x_sample': x_sample,
        'cache_a_k': cache_a_k, 'cache_a_v': cache_a_v, 'cache_a_idx': cache_a_idx,
        'cache_c_k': cache_c_k, 'cache_c_v': cache_c_v,
        'cache_d_k': cache_d_k, 'cache_d_v': cache_d_v, 'cache_d_logf': cache_d_logf,
        'page_table': page_table, 'c_prompt': c_prompt, 'c_sample': c_sample,
        'w_mod': w_mod, 'b_mod': b_mod, 'g_norm': g_norm,
        'w_in_ab': w_in_ab, 'w_out_ab': w_out_ab,
        'gm_norm_v': gm_norm_v, 'gm_spatial_w': gm_spatial_w, 'gm_spatial_b': gm_spatial_b,
        'w_in_cd': w_in_cd, 'w_out_cd': w_out_cd, 'fox_bias': fox_bias,
        'lam_q1': lam_q1, 'lam_k1': lam_k1, 'lam_q2': lam_q2, 'lam_k2': lam_k2, 'g_subln': g_subln,
        'w_gate': w_gate, 'w_up': w_up, 'w_down': w_down,
    }


def _stack(rows, j):
    return jnp.stack([r[j] for r in rows])


def reference(x_prompt, x_sample, cache_a_k, cache_a_v, cache_a_idx, cache_c_k, cache_c_v,
              cache_d_k, cache_d_v, cache_d_logf, page_table, c_prompt, c_sample,
              w_mod, b_mod, g_norm, w_in_ab, w_out_ab, gm_norm_v, gm_spatial_w, gm_spatial_b,
              w_in_cd, w_out_cd, fox_bias, lam_q1, lam_k1, lam_q2, lam_k2, g_subln,
              w_gate, w_up, w_down):
    past_len = page_table.shape[1] * PAGE_SIZE
    xp, xs = x_prompt, x_sample
    a_p, a_s, c_p, c_s, gm_s = [], [], [], [], []
    for layer in range(DEPTH):
        li = layer // 2
        mp = modulation(c_prompt, w_mod[layer], b_mod[layer])
        ms = modulation(c_sample, w_mod[layer], b_mod[layer])
        hp = rms_norm(xp, g_norm[layer, 0]) * (1 + mp[1]) + mp[0]
        hs = rms_norm(xs, g_norm[layer, 0]) * (1 + ms[1]) + ms[0]
        if layer % 2 == 0:
            op, rows_p = ab_prompt(hp, w_in_ab[li], w_out_ab[li], gm_norm_v[li], gm_spatial_w[li], gm_spatial_b[li])
            os_, rows_s = ab_sample(hs, past_len, page_table, cache_a_k, cache_a_v, cache_a_idx, li,
                                    w_in_ab[li], w_out_ab[li], gm_norm_v[li], gm_spatial_w[li], gm_spatial_b[li])
            a_p.append(rows_p)
            a_s.append(rows_s[:3])
            gm_s.append(rows_s[3])
        else:
            lam_init = 0.8 - 0.6 * math.exp(-0.3 * layer)
            lam = diff_lambda(lam_q1[li], lam_k1[li], lam_q2[li], lam_k2[li], lam_init)
            op, rows_p = cd_prompt(hp, w_in_cd[li], w_out_cd[li], fox_bias[li], lam, lam_init, g_subln[li])
            os_, rows_s = cd_sample(hs, past_len, page_table, cache_c_k, cache_c_v, cache_d_k, cache_d_v,
                                    cache_d_logf, li, w_in_cd[li], w_out_cd[li], fox_bias[li], lam,
                                    lam_init, g_subln[li])
            c_p.append(rows_p)
            c_s.append(rows_s)
        xp = xp + mp[2] * rms_norm(op, g_norm[layer, 1])
        xs = xs + ms[2] * rms_norm(os_, g_norm[layer, 1])
        hp = rms_norm(xp, g_norm[layer, 2]) * (1 + mp[4]) + mp[3]
        hs = rms_norm(xs, g_norm[layer, 2]) * (1 + ms[4]) + ms[3]
        xp = xp + mp[5] * rms_norm(swiglu(hp, w_gate[layer], w_up[layer], w_down[layer]), g_norm[layer, 3])
        xs = xs + ms[5] * rms_norm(swiglu(hs, w_gate[layer], w_up[layer], w_down[layer]), g_norm[layer, 3])
    return (xp, xs,
            _stack(a_p, 0), _stack(a_p, 1), _stack(a_p, 2),
            _stack(c_p, 0), _stack(c_p, 1), _stack(c_p, 2), _stack(c_p, 3), _stack(c_p, 4),
            _stack(a_s, 0), _stack(a_s, 1), _stack(a_s, 2),
            _stack(c_s, 0), _stack(c_s, 1), _stack(c_s, 2), _stack(c_s, 3), _stack(c_s, 4),
            jnp.stack(gm_s))
```

```python
import functools
import math

import numpy as np
import jax
import jax.numpy as jnp
from jax import lax
from jax.experimental import pallas as pl
from jax.experimental.pallas import tpu as pltpu

F32 = jnp.float32
BF16 = jnp.bfloat16
NEG_INF = float("-inf")

D_MODEL = 1024
HEAD_DIM = 64
ROPE_THETA = 500000.0
A_HEADS = 8
A_KV_HEADS = 2
IDX_HEADS = 8
IDX_DIM = 64
TOPK_MAX = 256
GM_GROUPS = 8
GM_WIDTH = GM_GROUPS * HEAD_DIM
CHUNK = 128
C_HEADS = 4
D_HEADS = 8
PAGE_SIZE = 128
EPS = 1e-6
D_FF = -(-(8 * D_MODEL) // (3 * 256)) * 256

LANES = 128
VMEM_LIMIT_BYTES = 56 * 1024 * 1024

ROW_TILE = 256
FFN_ROW_TILE = 256
Q_TILE = 128
PAGES_PER_STEP = 8
MOD_COL_TILE = 1536
BISECT_STEPS = 28

AB_Q, AB_K, AB_V, AB_QI, AB_KIW, AB_U, AB_GV, AB_COLS = 0, 512, 640, 768, 1280, 1408, 1920, 2432
CD_CQ, CD_CK, CD_CV, CD_DQ, CD_DK, CD_DV, CD_DF, CD_COLS = 0, 512, 1024, 1536, 2048, 2560, 3072, 3200
A_HEAD_PERM = (0, 4, 1, 5, 2, 6, 3, 7)


def _cparams(*sem):
    return pltpu.CompilerParams(dimension_semantics=sem, vmem_limit_bytes=VMEM_LIMIT_BYTES)


def _nt(a, b):
    return lax.dot_general(a, b, (((1,), (1,)), ((), ())), preferred_element_type=F32)


def _mm(a, b):
    return jnp.dot(a, b, preferred_element_type=F32)


def _rms(x, g):
    return x * lax.rsqrt(jnp.mean(x * x, axis=-1, keepdims=True) + EPS) * g


def _silu(x):
    return x / (1.0 + jnp.exp(-x))


def _gelu_tanh(x):
    return 0.5 * x * (1.0 + jnp.tanh(math.sqrt(2.0 / math.pi) * (x + 0.044715 * (x * x * x))))


def _half_masks(dtype):
    lane = lax.broadcasted_iota(jnp.int32, (1, LANES), 1)
    lo = (lane < HEAD_DIM).astype(dtype)
    return lo, (1 - lo).astype(dtype)


def _rope(z, c, s1, s2):
    outs = []
    for j in range(z.shape[1] // LANES):
        x = z[:, j * LANES:(j + 1) * LANES]
        outs.append(x * c + pltpu.roll(x, LANES - 8, 1) * s1 + pltpu.roll(x, 8, 1) * s2)
    return outs[0] if len(outs) == 1 else jnp.concatenate(outs, axis=1)


def _split3(x):
    hi = x.astype(BF16)
    r1 = x - hi.astype(F32)
    mid = r1.astype(BF16)
    lo = (r1 - mid.astype(F32)).astype(BF16)
    return hi, mid, lo


def _lane_prefix(x, tri):
    hi, mid, lo = _split3(x)
    r = _mm(jnp.concatenate([hi, mid, lo], axis=0), tri)
    return r[0:8] + r[8:16] + r[16:24]


def _tri(incl_upper=None, strict_lower=None):
    r = lax.broadcasted_iota(jnp.int32, (LANES, LANES), 0)
    c = lax.broadcasted_iota(jnp.int32, (LANES, LANES), 1)
    if incl_upper:
        return (r <= c).astype(BF16)
    return (r > c).astype(BF16)


def _count_ge(sm, x):
    return jnp.sum(jnp.where(sm >= x, 1.0, 0.0), axis=1, keepdims=True)


def _topk_bias(sm_ref, bias_ref, n_sel):
    kf = float(n_sel)
    sm = sm_ref[...]
    allowed = sm > NEG_INF
    mx = jnp.max(sm, axis=1, keepdims=True)
    mn = jnp.min(jnp.where(allowed, sm, jnp.inf), axis=1, keepdims=True)
    n_allowed = jnp.sum(jnp.where(allowed, 1.0, 0.0), axis=1, keepdims=True)
    few = n_allowed <= kf
    top_tie = _count_ge(sm, mx) >= kf

    def bisect(_, c):
        lo, hi = c
        mid = lo + (hi - lo) * 0.5
        ge = _count_ge(sm_ref[...], mid) >= kf
        return jnp.where(ge, mid, lo), jnp.where(ge, hi, mid)

    _, hi = lax.fori_loop(0, BISECT_STEPS, bisect, (mn, mx))

    done0 = jnp.where(few, 1.0, jnp.where(top_tie, 1.0, 0.0))
    thr0 = jnp.where(few, NEG_INF, mx)

    def not_done(c):
        return c[3] > 0.0

    def step(c):
        hi, thr, done, _ = c
        s = sm_ref[...]
        m = jnp.max(jnp.where(s < hi, s, NEG_INF), axis=1, keepdims=True)
        found = _count_ge(s, m) >= kf
        is_done = done > 0.5
        thr = jnp.where(is_done, thr, jnp.where(found, m, thr))
        hi = jnp.where(is_done, hi, jnp.where(found, hi, m))
        done = jnp.where(found, 1.0, done)
        return hi, thr, done, jnp.sum(1.0 - done)

    _, thr, _, _ = lax.while_loop(not_done, step, (hi, thr0, done0, jnp.sum(1.0 - done0)))

    is_gt = sm > thr
    is_eq = jnp.where(sm == thr, jnp.where(allowed, 1.0, 0.0), 0.0)
    need = kf - jnp.sum(jnp.where(is_gt, 1.0, 0.0), axis=1, keepdims=True)
    n_eq = jnp.sum(is_eq, axis=1, keepdims=True)
    bias_ref[...] = jnp.where(sm >= thr, jnp.where(allowed, 0.0, NEG_INF), NEG_INF)

    @pl.when(jnp.max(n_eq - need) > 0.0)
    def _():
        tri = _tri(incl_upper=True)

        def blk(j, run):
            cols = pl.ds(pl.multiple_of(j * LANES, LANES), LANES)
            s = sm_ref[:, cols]
            e = jnp.where(s == thr, jnp.where(s > NEG_INF, 1.0, 0.0), 0.0)
            rank = run + _mm(e.astype(BF16), tri)
            keep = jnp.where(e > 0.5, jnp.where(rank <= need, 1.0, 0.0), 0.0)
            bias_ref[:, cols] = jnp.where(s > thr, 0.0, jnp.where(keep > 0.5, 0.0, NEG_INF))
            return run + jnp.sum(e, axis=1, keepdims=True)

        lax.fori_loop(0, sm_ref.shape[1] // LANES, blk, jnp.zeros_like(need))


def _mod_kernel(c_ref, w_ref, b_ref, o_ref):
    a = _silu(c_ref[...]).astype(BF16)
    o_ref[...] = _mm(a, w_ref[...].astype(BF16)) + b_ref[...]


def _modulation(c_all, w_mod, b_mod):
    n_layers, _, n_out = w_mod.shape
    rows = c_all.shape[0]
    tn = MOD_COL_TILE
    return pl.pallas_call(
        _mod_kernel,
        grid=(n_layers, n_out // tn),
        in_specs=[
            pl.BlockSpec((rows, D_MODEL), lambda l, n: (0, 0)),
            pl.BlockSpec((None, D_MODEL, tn), lambda l, n: (l, 0, n)),
            pl.BlockSpec((None, 1, tn), lambda l, n: (l, 0, n)),
        ],
        out_specs=pl.BlockSpec((None, rows, tn), lambda l, n: (l, 0, n)),
        out_shape=jax.ShapeDtypeStruct((n_layers, rows, n_out), F32),
        compiler_params=_cparams("arbitrary", "arbitrary"),
        name="modulation",
    )(c_all, w_mod, b_mod.reshape(n_layers, 1, n_out))


def _rowvec_spec(vec, tm, tiles_per_seq):
    if vec.ndim == 3:
        return pl.BlockSpec((None, 1, vec.shape[-1]), lambda i: (i // tiles_per_seq, 0, 0))
    return pl.BlockSpec((tm, vec.shape[-1]), lambda i: (i, 0))


def _const_spec(arr):
    nd = arr.ndim
    return pl.BlockSpec(arr.shape, lambda i: (0,) * nd)


def _inproj_ab_kernel(x_ref, g_ref, sh_ref, sc_ref, w_ref, rc_ref, rs1_ref, rs2_ref, gmg_ref, wsp_ref, bsp_ref,
                      q_ref, k32_ref, v32_ref, kiw_ref, kbf_ref, vbf_ref, qi_ref, kid_ref, ob_ref, *gv_ref):
    h = (_rms(x_ref[...], g_ref[...]) * (1.0 + sc_ref[...]) + sh_ref[...]).astype(BF16)
    c, s1, s2 = rc_ref[...], rs1_ref[...], rs2_ref[...]

    def seg(lo, hi):
        return _mm(h, w_ref[:, lo:hi])

    q_ref[...] = (_rope(seg(AB_Q, AB_K), c, s1, s2) * HEAD_DIM ** -0.5).astype(BF16)
    k = _rope(seg(AB_K, AB_V), c, s1, s2)
    k32_ref[...] = k
    kbf_ref[...] = k.astype(BF16)
    v = seg(AB_V, AB_QI)
    v32_ref[...] = v
    vbf_ref[...] = v.astype(BF16)
    qi_ref[...] = _rope(seg(AB_QI, AB_KIW), c, s1, s2).astype(BF16)
    kz = seg(AB_KIW, AB_U)
    kr = _rope(kz, c, s1, s2)
    lane = lax.broadcasted_iota(jnp.int32, kz.shape, 1)
    kiw_ref[...] = jnp.where(lane < IDX_DIM, kr, kz)
    kid_ref[...] = jnp.where(lane < IDX_DIM, kr, pltpu.roll(kr, IDX_DIM, 1)).astype(BF16)

    u = _gelu_tanh(seg(AB_U, AB_GV))
    gv = _gelu_tanh(seg(AB_GV, AB_COLS))
    gc = gv - jnp.mean(gv, axis=-1, keepdims=True)
    gvn = gc * lax.rsqrt(jnp.mean(gc * gc, axis=-1, keepdims=True) + EPS) * gmg_ref[...]
    if gv_ref:
        gv_ref[0][...] = gvn
    gvb = gvn.astype(BF16)
    lane = lax.broadcasted_iota(jnp.int32, (1, LANES), 1)
    mix = []
    for j in range(GM_GROUPS // 2):
        slab = gvb[:, j * LANES:(j + 1) * LANES]
        mix.append(jnp.where(lane < HEAD_DIM, _mm(wsp_ref[2 * j], slab), _mm(wsp_ref[2 * j + 1], slab)))
    ob_ref[...] = (u * (jnp.concatenate(mix, axis=1) + bsp_ref[...])).astype(BF16)


def _inproj_ab(x, g, shift, scale, w, rope_tabs, gm_g, wsp, bsp, tiles_per_seq, want_gv):
    m = x.shape[0]
    tm = min(ROW_TILE, m)
    n_tab_tiles = rope_tabs[0].shape[0] // tm
    row = lambda width: pl.BlockSpec((tm, width), lambda i: (i, 0))
    tab = pl.BlockSpec((tm, LANES), lambda i: (i % n_tab_tiles, 0))
    sds = lambda width, dt: jax.ShapeDtypeStruct((m, width), dt)
    out_shape = [sds(512, BF16), sds(128, F32), sds(128, F32), sds(128, F32), sds(128, BF16), sds(128, BF16),
                 sds(512, BF16), sds(128, BF16), sds(512, BF16)]
    out_specs = [row(512), row(128), row(128), row(128), row(128), row(128), row(512), row(128), row(512)]
    if want_gv:
        out_shape.append(sds(512, F32))
        out_specs.append(row(512))
    return pl.pallas_call(
        _inproj_ab_kernel,
        grid=(m // tm,),
        in_specs=[row(D_MODEL), _const_spec(g), _rowvec_spec(shift, tm, tiles_per_seq),
                  _rowvec_spec(scale, tm, tiles_per_seq), _const_spec(w), tab, tab, tab,
                  _const_spec(gm_g), _const_spec(wsp), _const_spec(bsp)],
        out_specs=out_specs,
        out_shape=out_shape,
        compiler_params=_cparams("parallel"),
        name="inproj_ab",
    )(x, g, shift, scale, w, *rope_tabs, gm_g, wsp, bsp)


def _inproj_cd_kernel(x_ref, g_ref, sh_ref, sc_ref, w_ref, rc_ref, rs1_ref, rs2_ref, fb_ref,
                      cq_ref, ck32_ref, cv32_ref, ckbf_ref, cvbf_ref,
                      dq_ref, dk32_ref, dv32_ref, dkbf_ref, dvbf_ref, lf_ref):
    h = (_rms(x_ref[...], g_ref[...]) * (1.0 + sc_ref[...]) + sh_ref[...]).astype(BF16)
    c, s1, s2 = rc_ref[...], rs1_ref[...], rs2_ref[...]

    def seg(lo, hi):
        return _mm(h, w_ref[:, lo:hi])

    cq_ref[...] = (_rope(seg(CD_CQ, CD_CK), c, s1, s2) * HEAD_DIM ** -0.5).astype(BF16)
    ck = _rope(seg(CD_CK, CD_CV), c, s1, s2)
    ck32_ref[...] = ck
    ckbf_ref[...] = ck.astype(BF16)
    cv = seg(CD_CV, CD_DQ)
    cv32_ref[...] = cv
    cvbf_ref[...] = cv.astype(BF16)
    dq_ref[...] = (seg(CD_DQ, CD_DK) * HEAD_DIM ** -0.5).astype(BF16)
    dk = seg(CD_DK, CD_DV)
    dk32_ref[...] = dk
    dkbf_ref[...] = dk.astype(BF16)
    dv = seg(CD_DV, CD_DF)
    dv32_ref[...] = dv
    dvbf_ref[...] = dv.astype(BF16)
    f = seg(CD_DF, CD_COLS) + fb_ref[...]
    lf_ref[...] = jnp.minimum(f, 0.0) - jnp.log(1.0 + jnp.exp(-jnp.abs(f)))


def _inproj_cd(x, g, shift, scale, w, rope_tabs, fox_b, tiles_per_seq):
    m = x.shape[0]
    tm = min(ROW_TILE, m)
    n_tab_tiles = rope_tabs[0].shape[0] // tm
    row = lambda width: pl.BlockSpec((tm, width), lambda i: (i, 0))
    tab = pl.BlockSpec((tm, LANES), lambda i: (i % n_tab_tiles, 0))
    sds = lambda width, dt: jax.ShapeDtypeStruct((m, width), dt)
    per = [sds(512, BF16), sds(512, F32), sds(512, F32), sds(512, BF16), sds(512, BF16)]
    return pl.pallas_call(
        _inproj_cd_kernel,
        grid=(m // tm,),
        in_specs=[row(D_MODEL), _const_spec(g), _rowvec_spec(shift, tm, tiles_per_seq),
                  _rowvec_spec(scale, tm, tiles_per_seq), _const_spec(w), tab, tab, tab, _const_spec(fox_b)],
        out_specs=[row(512)] * 10 + [row(128)],
        out_shape=per + per + [sds(128, F32)],
        compiler_params=_cparams("parallel"),
        name="inproj_cd",
    )(x, g, shift, scale, w, *rope_tabs, fox_b)


def _outproj_kernel(a_ref, b_ref, wa_ref, wb_ref, x_ref, gate_ref, g_ref, o_ref):
    op = _mm(a_ref[...], wa_ref[...]) + _mm(b_ref[...], wb_ref[...])
    o_ref[...] = x_ref[...] + gate_ref[...] * _rms(op, g_ref[...])


def _outproj(a, b, wa, wb, x, gate, g, tiles_per_seq_of):
    m = x.shape[0]
    tm = min(FFN_ROW_TILE, m)
    row = lambda width: pl.BlockSpec((tm, width), lambda i: (i, 0))
    return pl.pallas_call(
        _outproj_kernel,
        grid=(m // tm,),
        in_specs=[row(a.shape[1]), row(b.shape[1]), _const_spec(wa), _const_spec(wb), row(D_MODEL),
                  _rowvec_spec(gate, tm, tiles_per_seq_of(tm)), _const_spec(g)],
        out_specs=row(D_MODEL),
        out_shape=jax.ShapeDtypeStruct((m, D_MODEL), F32),
        compiler_params=_cparams("parallel"),
        name="outproj",
    )(a, b, wa, wb, x, gate, g)


def _ffn_kernel(x_ref, g2_ref, sh_ref, sc_ref, wg_ref, wu_ref, wd_ref, gate_ref, g3_ref, o_ref):
    x = x_ref[...]
    h = (_rms(x, g2_ref[...]) * (1.0 + sc_ref[...]) + sh_ref[...]).astype(BF16)
    act = (_silu(_mm(h, wg_ref[...])) * _mm(h, wu_ref[...])).astype(BF16)
    o_ref[...] = x + gate_ref[...] * _rms(_mm(act, wd_ref[...]), g3_ref[...])


def _ffn(x, g2, shift, scale, wg, wu, wd, gate, g3, tiles_per_seq_of):
    m = x.shape[0]
    tm = min(FFN_ROW_TILE, m)
    tps = tiles_per_seq_of(tm)
    row = pl.BlockSpec((tm, D_MODEL), lambda i: (i, 0))
    return pl.pallas_call(
        _ffn_kernel,
        grid=(m // tm,),
        in_specs=[row, _const_spec(g2), _rowvec_spec(shift, tm, tps), _rowvec_spec(scale, tm, tps),
                  _const_spec(wg), _const_spec(wu), _const_spec(wd), _rowvec_spec(gate, tm, tps), _const_spec(g3)],
        out_specs=row,
        out_shape=jax.ShapeDtypeStruct((m, D_MODEL), F32),
        compiler_params=_cparams("parallel"),
        name="ffn",
    )(x, g2, shift, scale, wg, wu, wd, gate, g3)


def _dsa_prompt_kernel(qi_ref, wi_ref, kid_ref, q_ref, k_ref, v_ref, o_ref, sm_ref, bias_ref, *, n_sel):
    tq, s_len = sm_ref.shape
    i = pl.program_id(1)
    m_lo, m_hi = _half_masks(BF16)
    kid = kid_ref[...]
    score = jnp.zeros((tq, s_len), F32)
    for j in range(IDX_HEADS // 2):
        slab = qi_ref[:, j * LANES:(j + 1) * LANES]
        s = _nt(jnp.concatenate([slab * m_lo, slab * m_hi], axis=0), kid)
        score = score + jnp.maximum(s[:tq], 0.0) * wi_ref[:, 2 * j:2 * j + 1]
        score = score + jnp.maximum(s[tq:], 0.0) * wi_ref[:, 2 * j + 1:2 * j + 2]
    score = score * (IDX_HEADS * IDX_DIM) ** -0.5
    row = i * tq + lax.broadcasted_iota(jnp.int32, (tq, s_len), 0)
    col = lax.broadcasted_iota(jnp.int32, (tq, s_len), 1)
    sm_ref[...] = jnp.where(col <= row, score, NEG_INF)
    _topk_bias(sm_ref, bias_ref, n_sel)

    k = k_ref[...]
    v = v_ref[...]
    n_slab = A_HEADS // 2
    outs = []
    for mask in (m_lo, m_hi):
        stk = jnp.concatenate([q_ref[:, j * LANES:(j + 1) * LANES] * mask for j in range(n_slab)], axis=0)
        lg = _nt(stk, k)
        per = []
        for j in range(n_slab):
            l = lg[j * tq:(j + 1) * tq] + bias_ref[...]
            p = jnp.exp(l - jnp.max(l, axis=1, keepdims=True))
            per.append(_mm(p.astype(BF16), v) / jnp.sum(p, axis=1, keepdims=True))
        outs.append(per)
    lane = lax.broadcasted_iota(jnp.int32, (1, LANES), 1)
    for j in range(n_slab):
        o_ref[:, j * LANES:(j + 1) * LANES] = jnp.where(lane < HEAD_DIM, outs[0][j], outs[1][j]).astype(BF16)


def _dsa_prompt(qi, wi, kid, q, k, v, n_sel):
    b, s_len, _ = q.shape
    tq = min(Q_TILE, s_len)
    qblk = lambda width: pl.BlockSpec((None, tq, width), lambda bb, i: (bb, i, 0))
    full = lambda width: pl.BlockSpec((None, s_len, width), lambda bb, i: (bb, 0, 0))
    return pl.pallas_call(
        functools.partial(_dsa_prompt_kernel, n_sel=n_sel),
        grid=(b, s_len // tq),
        in_specs=[qblk(512), qblk(IDX_HEADS), full(128), qblk(512), full(128), full(128)],
        out_specs=qblk(512),
        out_shape=jax.ShapeDtypeStruct((b, s_len, 512), BF16),
        scratch_shapes=[pltpu.VMEM((tq, s_len), F32), pltpu.VMEM((tq, s_len), F32)],
        compiler_params=_cparams("parallel", "arbitrary"),
        name="dsa_prompt",
    )(qi, wi, kid, q, k, v)


def _diff_lambda(lamv_ref, lam_init):
    lv = lamv_ref[...]
    d1 = jnp.sum(lv[0:1] * lv[1:2], axis=1, keepdims=True)
    d2 = jnp.sum(lv[2:3] * lv[3:4], axis=1, keepdims=True)
    return jnp.exp(d1) - jnp.exp(d2) + lam_init


def _diff_prompt_kernel(lamv_ref, gsub_ref, q_ref, k_ref, v_ref, o_ref, *, lam_init):
    tq = q_ref.shape[0]
    s_len = k_ref.shape[0]
    i = pl.program_id(1)
    lam = _diff_lambda(lamv_ref, lam_init)
    m_lo, m_hi = _half_masks(BF16)
    row = i * tq + lax.broadcasted_iota(jnp.int32, (tq, s_len), 0)
    col = lax.broadcasted_iota(jnp.int32, (tq, s_len), 1)
    causal = col <= row
    for h in range(C_HEADS):
        cols = slice(h * LANES, (h + 1) * LANES)
        slab = q_ref[:, cols]
        lg = _nt(jnp.concatenate([slab * m_lo, slab * m_hi], axis=0), k_ref[:, cols])
        l0 = jnp.where(causal, lg[:tq], NEG_INF)
        l1 = jnp.where(causal, lg[tq:], NEG_INF)
        p0 = jnp.exp(l0 - jnp.max(l0, axis=1, keepdims=True))
        p1 = jnp.exp(l1 - jnp.max(l1, axis=1, keepdims=True))
        w = p0 * (1.0 / jnp.sum(p0, axis=1, keepdims=True)) - p1 * (lam / jnp.sum(p1, axis=1, keepdims=True))
        o = _mm(w.astype(BF16), v_ref[:, cols])
        o_ref[:, cols] = (_rms(o, gsub_ref[...]) * (1.0 - lam_init)).astype(BF16)


def _diff_prompt(lamv, gsub, q, k, v, lam_init):
    b, s_len, _ = q.shape
    tq = min(Q_TILE, s_len)
    qblk = pl.BlockSpec((None, tq, 512), lambda bb, i: (bb, i, 0))
    full = pl.BlockSpec((None, s_len, 512), lambda bb, i: (bb, 0, 0))
    cst = lambda arr: pl.BlockSpec(arr.shape, lambda bb, i: (0, 0))
    return pl.pallas_call(
        functools.partial(_diff_prompt_kernel, lam_init=lam_init),
        grid=(b, s_len // tq),
        in_specs=[cst(lamv), cst(gsub), qblk, full, full],
        out_specs=qblk,
        out_shape=jax.ShapeDtypeStruct((b, s_len, 512), BF16),
        compiler_params=_cparams("parallel", "arbitrary"),
        name="diff_prompt",
    )(lamv, gsub, q, k, v)


def _cumsum_kernel(x_ref, o_ref):
    tri = _tri(incl_upper=True)
    carry = jnp.zeros((8, 1), F32)
    for c in range(x_ref.shape[1] // LANES):
        cols = slice(c * LANES, (c + 1) * LANES)
        out = _lane_prefix(x_ref[:, cols], tri) + carry
        o_ref[:, cols] = out
        carry = out[:, LANES - 1:LANES]


def _cumsum_lanes(x):
    b, h, s_len = x.shape
    blk = pl.BlockSpec((None, h, s_len), lambda bb: (bb, 0, 0))
    return pl.pallas_call(
        _cumsum_kernel,
        grid=(b,),
        in_specs=[blk],
        out_specs=blk,
        out_shape=jax.ShapeDtypeStruct(x.shape, F32),
        compiler_params=_cparams("parallel"),
        name="fox_cumsum",
    )(x)


def _fox_prompt_kernel(q_ref, k_ref, v_ref, cq_ref, ck_ref, o_ref):
    tq = q_ref.shape[0]
    s_len = k_ref.shape[0]
    i = pl.program_id(1)
    m_lo, m_hi = _half_masks(BF16)
    row = i * tq + lax.broadcasted_iota(jnp.int32, (tq, s_len), 0)
    col = lax.broadcasted_iota(jnp.int32, (tq, s_len), 1)
    causal = col <= row
    lane = lax.broadcasted_iota(jnp.int32, (1, LANES), 1)
    for j in range(D_HEADS // 2):
        cols = slice(j * LANES, (j + 1) * LANES)
        slab = q_ref[:, cols]
        lg = _nt(jnp.concatenate([slab * m_lo, slab * m_hi], axis=0), k_ref[:, cols])
        v = v_ref[:, cols]
        pv = []
        for hh in range(2):
            h = 2 * j + hh
            l = lg[hh * tq:(hh + 1) * tq] + cq_ref[:, h:h + 1] - ck_ref[h:h + 1, :]
            l = jnp.where(causal, l, NEG_INF)
            p = jnp.exp(l - jnp.max(l, axis=1, keepdims=True))
            pv.append(_mm(p.astype(BF16), v) / jnp.sum(p, axis=1, keepdims=True))
        o_ref[:, cols] = jnp.where(lane < HEAD_DIM, pv[0], pv[1]).astype(BF16)


def _fox_prompt(q, k, v, cum, cum_t):
    b, s_len, _ = q.shape
    tq = min(Q_TILE, s_len)
    qblk = lambda width: pl.BlockSpec((None, tq, width), lambda bb, i: (bb, i, 0))
    full = pl.BlockSpec((None, s_len, 512), lambda bb, i: (bb, 0, 0))
    return pl.pallas_call(
        _fox_prompt_kernel,
        grid=(b, s_len // tq),
        in_specs=[qblk(512), full, full, qblk(D_HEADS),
                  pl.BlockSpec((None, D_HEADS, s_len), lambda bb, i: (bb, 0, 0))],
        out_specs=qblk(512),
        out_shape=jax.ShapeDtypeStruct((b, s_len, 512), BF16),
        compiler_params=_cparams("parallel", "arbitrary"),
        name="fox_prompt",
    )(q, k, v, cum, cum_t)


def _page_specs(block, n_pg, reverse_from=None):
    specs = []
    for pg in range(n_pg):
        if reverse_from is None:
            idx = lambda b, st, pt, pg=pg: (pt[b, st * n_pg + pg],) + (0,) * (len(block) - 1)
        else:
            idx = lambda b, st, pt, pg=pg: (pt[b, reverse_from - (st * n_pg + pg)],) + (0,) * (len(block) - 1)
        specs.append(pl.BlockSpec((None,) + tuple(block[1:]), idx))
    return specs


def _per_seq(shape):
    nd = len(shape)
    return pl.BlockSpec((None,) + tuple(shape[1:]), lambda b, st, pt: (b,) + (0,) * (nd - 1))


def _softmax_step(lg, v, m_ref, l_ref, acc_ref):
    m_old = m_ref[...]
    m_new = jnp.maximum(m_old, jnp.max(lg, axis=1, keepdims=True))
    m_safe = jnp.where(m_new == NEG_INF, 0.0, m_new)
    alpha = jnp.exp(m_old - m_safe)
    p = jnp.exp(lg - m_safe)
    l_ref[...] = alpha * l_ref[...] + jnp.sum(p, axis=1, keepdims=True)
    acc_ref[...] = alpha * acc_ref[...] + _mm(p.astype(BF16), v)
    m_ref[...] = m_new


def _softmax_init(m_ref, l_ref, acc_ref):
    m_ref[...] = jnp.full(m_ref.shape, NEG_INF, F32)
    l_ref[...] = jnp.zeros(l_ref.shape, F32)
    acc_ref[...] = jnp.zeros(acc_ref.shape, F32)


def _head_sum(w, n_groups):
    out = w[0:8]
    for h in range(1, n_groups):
        out = out + w[8 * h:8 * h + 8]
    return out


def _new_key_mask(rows, token_major=False):
    r = lax.broadcasted_iota(jnp.int32, (rows, LANES), 0)
    t = r // 8 if token_major else r % 8
    c = lax.broadcasted_iota(jnp.int32, (rows, LANES), 1)
    return c <= t


def _dsa_select_kernel(pt_ref, q_ref, w_ref, knew_ref, *rest, n_pg, n_sel):
    pages = rest[:n_pg]
    bias_ref, sm_ref = rest[n_pg], rest[n_pg + 1]
    st = pl.program_id(1)
    q = q_ref[...]
    wcol = w_ref[...]
    scale = (IDX_HEADS * IDX_DIM) ** -0.5

    def scores(keys):
        return _head_sum(jnp.maximum(_nt(q, keys), 0.0) * wcol, IDX_HEADS) * scale

    for pg in range(n_pg):
        col0 = pl.multiple_of((st * n_pg + pg) * PAGE_SIZE, PAGE_SIZE)
        sm_ref[:, pl.ds(col0, PAGE_SIZE)] = scores(pages[pg][...].astype(BF16))

    @pl.when(st == pl.num_programs(1) - 1)
    def _():
        past = sm_ref.shape[1] - LANES
        sm_ref[:, past:] = jnp.where(_new_key_mask(8), scores(knew_ref[...]), NEG_INF)
        _topk_bias(sm_ref, bias_ref, n_sel)


def _dsa_select(page_table, q, wcol, knew, cache_idx, n_sel):
    bd, n_pages = page_table.shape
    n_pg = min(PAGES_PER_STEP, n_pages)
    width = n_pages * PAGE_SIZE + LANES
    return pl.pallas_call(
        functools.partial(_dsa_select_kernel, n_pg=n_pg, n_sel=n_sel),
        grid_spec=pltpu.PrefetchScalarGridSpec(
            num_scalar_prefetch=1,
            grid=(bd, n_pages // n_pg),
            in_specs=[_per_seq(q.shape), _per_seq(wcol.shape), _per_seq(knew.shape)]
                     + _page_specs(cache_idx.shape, n_pg),
            out_specs=pl.BlockSpec((None, 8, width), lambda b, st, pt: (b, 0, 0)),
            scratch_shapes=[pltpu.VMEM((8, width), F32)],
        ),
        out_shape=jax.ShapeDtypeStruct((bd, 8, width), F32),
        compiler_params=_cparams("parallel", "arbitrary"),
        name="dsa_select",
    )(page_table, q, wcol, knew, *([cache_idx] * n_pg))


def _dsa_sample_kernel(pt_ref, q_ref, bias_ref, knew_ref, vnew_ref, *rest, n_pg):
    kpages, vpages = rest[:n_pg], rest[n_pg:2 * n_pg]
    o_ref, m_ref, l_ref, acc_ref = rest[2 * n_pg:]
    st = pl.program_id(1)

    @pl.when(st == 0)
    def _():
        _softmax_init(m_ref, l_ref, acc_ref)

    q = q_ref[...]
    n_rep = q.shape[0] // 8

    def bias_rows(cols):
        bt = bias_ref[:, cols]
        return jnp.concatenate([bt] * n_rep, axis=0)

    for pg in range(n_pg):
        col0 = pl.multiple_of((st * n_pg + pg) * PAGE_SIZE, PAGE_SIZE)
        lg = _nt(q, kpages[pg][...].astype(BF16)) + bias_rows(pl.ds(col0, PAGE_SIZE))
        _softmax_step(lg, vpages[pg][...].astype(BF16), m_ref, l_ref, acc_ref)

    @pl.when(st == pl.num_programs(1) - 1)
    def _():
        past = bias_ref.shape[1] - LANES
        lg = _nt(q, knew_ref[...]) + bias_rows(slice(past, past + LANES))
        _softmax_step(lg, vnew_ref[...], m_ref, l_ref, acc_ref)
        o = acc_ref[...] / l_ref[...]
        lane = lax.broadcasted_iota(jnp.int32, (1, LANES), 1)
        half = q.shape[0] // 2
        for j in range(half // 8):
            o_ref[:, j * LANES:(j + 1) * LANES] = jnp.where(
                lane < HEAD_DIM, o[8 * j:8 * j + 8], o[half + 8 * j:half + 8 * j + 8])


def _dsa_sample(page_table, q, bias, knew, vnew, cache_k, cache_v):
    bd, n_pages = page_table.shape
    n_pg = min(PAGES_PER_STEP, n_pages)
    rows = q.shape[1]
    return pl.pallas_call(
        functools.partial(_dsa_sample_kernel, n_pg=n_pg),
        grid_spec=pltpu.PrefetchScalarGridSpec(
            num_scalar_prefetch=1,
            grid=(bd, n_pages // n_pg),
            in_specs=[_per_seq(q.shape), _per_seq(bias.shape), _per_seq(knew.shape), _per_seq(vnew.shape)]
                     + _page_specs(cache_k.shape, n_pg) + _page_specs(cache_v.shape, n_pg),
            out_specs=pl.BlockSpec((None, 8, 512), lambda b, st, pt: (b, 0, 0)),
            scratch_shapes=[pltpu.VMEM((rows, 1), F32), pltpu.VMEM((rows, 1), F32), pltpu.VMEM((rows, LANES), F32)],
        ),
        out_shape=jax.ShapeDtypeStruct((bd, 8, 512), F32),
        compiler_params=_cparams("parallel", "arbitrary"),
        name="dsa_sample",
    )(page_table, q, bias, knew, vnew, *([cache_k] * n_pg), *([cache_v] * n_pg))


def _diff_sample_kernel(pt_ref, lamv_ref, gsub_ref, q_ref, knew_ref, vnew_ref, *rest, n_pg, lam_init):
    kpages, vpages = rest[:n_pg], rest[n_pg:2 * n_pg]
    o_ref, m_ref, l_ref, acc_ref = rest[2 * n_pg:]
    st = pl.program_id(1)

    @pl.when(st == 0)
    def _():
        _softmax_init(m_ref, l_ref, acc_ref)

    q = q_ref[...]
    for pg in range(n_pg):
        lg = _nt(q, kpages[pg][...].astype(BF16))
        _softmax_step(lg, vpages[pg][...].astype(BF16), m_ref, l_ref, acc_ref)

    @pl.when(st == pl.num_programs(1) - 1)
    def _():
        lg = jnp.where(_new_key_mask(q.shape[0]), _nt(q, knew_ref[...]), NEG_INF)
        _softmax_step(lg, vnew_ref[...], m_ref, l_ref, acc_ref)
        lam = _diff_lambda(lamv_ref, lam_init)
        o = acc_ref[...] / l_ref[...]
        for h in range(C_HEADS):
            cols = slice(h * LANES, (h + 1) * LANES)
            oh = o[16 * h:16 * h + 8, cols] - lam * o[16 * h + 8:16 * h + 16, cols]
            o_ref[:, cols] = _rms(oh, gsub_ref[...]) * (1.0 - lam_init)


def _diff_sample(page_table, lamv, gsub, q, knew, vnew, cache_k, cache_v, lam_init):
    bd, n_pages = page_table.shape
    n_pg = min(PAGES_PER_STEP, n_pages)
    rows = q.shape[1]
    cst = lambda arr: pl.BlockSpec(arr.shape, lambda b, st, pt: (0, 0))
    return pl.pallas_call(
        functools.partial(_diff_sample_kernel, n_pg=n_pg, lam_init=lam_init),
        grid_spec=pltpu.PrefetchScalarGridSpec(
            num_scalar_prefetch=1,
            grid=(bd, n_pages // n_pg),
            in_specs=[cst(lamv), cst(gsub), _per_seq(q.shape), _per_seq(knew.shape), _per_seq(vnew.shape)]
                     + _page_specs(cache_k.shape, n_pg) + _page_specs(cache_v.shape, n_pg),
            out_specs=pl.BlockSpec((None, 8, 512), lambda b, st, pt: (b, 0, 0)),
            scratch_shapes=[pltpu.VMEM((rows, 1), F32), pltpu.VMEM((rows, 1), F32), pltpu.VMEM((rows, 512), F32)],
        ),
        out_shape=jax.ShapeDtypeStruct((bd, 8, 512), F32),
        compiler_params=_cparams("parallel", "arbitrary"),
        name="diff_sample",
    )(page_table, lamv, gsub, q, knew, vnew, *([cache_k] * n_pg), *([cache_v] * n_pg))


def _fox_sample_kernel(pt_ref, q_ref, lfc_ref, lft_ref, knew_ref, vnew_ref, *rest, n_pg):
    kpages, vpages, fpages = rest[:n_pg], rest[n_pg:2 * n_pg], rest[2 * n_pg:3 * n_pg]
    o_ref, m_ref, l_ref, acc_ref, carry_ref = rest[3 * n_pg:]
    st = pl.program_id(1)

    @pl.when(st == 0)
    def _():
        _softmax_init(m_ref, l_ref, acc_ref)
        carry_ref[...] = jnp.zeros(carry_ref.shape, F32)

    q = q_ref[...]
    lfc = lfc_ref[...]
    run = lfc[0:8]
    cn = [run]
    for t in range(1, 8):
        run = run + lfc[8 * t:8 * t + 8]
        cn.append(run)
    cn_col = jnp.concatenate(cn, axis=0)
    tri_after = _tri(strict_lower=True)

    for pg in range(n_pg):
        x = fpages[pg][...]
        sfx = _lane_prefix(x, tri_after) + carry_ref[...]
        carry_ref[...] = carry_ref[...] + jnp.sum(x, axis=1, keepdims=True)
        lg = _nt(q, kpages[pg][...].astype(BF16)) + cn_col + jnp.concatenate([sfx] * 8, axis=0)
        _softmax_step(lg, vpages[pg][...].astype(BF16), m_ref, l_ref, acc_ref)

    @pl.when(st == pl.num_programs(1) - 1)
    def _():
        cn_row = _lane_prefix(lft_ref[...], _tri(incl_upper=True))
        lg = _nt(q, knew_ref[...]) + cn_col - jnp.concatenate([cn_row] * 8, axis=0)
        lg = jnp.where(_new_key_mask(q.shape[0], token_major=True), lg, NEG_INF)
        _softmax_step(lg, vnew_ref[...], m_ref, l_ref, acc_ref)
        o = acc_ref[...] / l_ref[...]
        hrow = lax.broadcasted_iota(jnp.int32, (8, 512), 0)
        hcol = lax.broadcasted_iota(jnp.int32, (8, 512), 1) // HEAD_DIM
        own = hrow == hcol
        for t in range(8):
            o_ref[t:t + 1, :] = jnp.sum(jnp.where(own, o[8 * t:8 * t + 8], 0.0), axis=0, keepdims=True)


def _fox_sample(page_table, q, lf_col, lf_t, knew, vnew, cache_k, cache_v, cache_lf_t):
    bd, n_pages = page_table.shape
    n_pg = min(PAGES_PER_STEP, n_pages)
    rows = q.shape[1]
    last = n_pages - 1
    return pl.pallas_call(
        functools.partial(_fox_sample_kernel, n_pg=n_pg),
        grid_spec=pltpu.PrefetchScalarGridSpec(
            num_scalar_prefetch=1,
            grid=(bd, n_pages // n_pg),
            in_specs=[_per_seq(q.shape), _per_seq(lf_col.shape), _per_seq(lf_t.shape),
                      _per_seq(knew.shape), _per_seq(vnew.shape)]
                     + _page_specs(cache_k.shape, n_pg, last) + _page_specs(cache_v.shape, n_pg, last)
                     + _page_specs(cache_lf_t.shape, n_pg, last),
            out_specs=pl.BlockSpec((None, 8, 512), lambda b, st, pt: (b, 0, 0)),
            scratch_shapes=[pltpu.VMEM((rows, 1), F32), pltpu.VMEM((rows, 1), F32), pltpu.VMEM((rows, 512), F32),
                            pltpu.VMEM((8, 1), F32)],
        ),
        out_shape=jax.ShapeDtypeStruct((bd, 8, 512), F32),
        compiler_params=_cparams("parallel", "arbitrary"),
        name="fox_sample",
    )(page_table, q, lf_col, lf_t, knew, vnew, *([cache_k] * n_pg), *([cache_v] * n_pg), *([cache_lf_t] * n_pg))


def _rope_tables(pos):
    rot = HEAD_DIM // 4
    half = rot // 2
    inv = ROPE_THETA ** (-np.arange(half, dtype=np.float64) / half)
    ang = np.asarray(pos, np.float64)[:, None] * inv[None, :]
    lane = np.arange(LANES) % HEAD_DIM
    c = np.ones((len(pos), LANES), np.float64)
    s1 = np.zeros_like(c)
    s2 = np.zeros_like(c)
    first, second = lane < half, (lane >= half) & (lane < rot)
    c[:, first] = np.cos(ang)[:, lane[first]]
    c[:, second] = np.cos(ang)[:, lane[second] - half]
    s1[:, first] = -np.sin(ang)[:, lane[first]]
    s2[:, second] = np.sin(ang)[:, lane[second] - half]
    return tuple(jnp.asarray(a, F32) for a in (c, s1, s2))


def _pack_w_ab(w):
    q, k, v, qi, ki, wi, u, gv = jnp.split(w, np.cumsum(
        (512, 128, 128, 512, IDX_DIM, IDX_HEADS, GM_WIDTH, GM_WIDTH))[:-1].tolist(), axis=1)
    q = q.reshape(D_MODEL, A_HEADS, HEAD_DIM)[:, np.array(A_HEAD_PERM)].reshape(D_MODEL, 512)
    pad = jnp.zeros((D_MODEL, LANES - IDX_DIM - IDX_HEADS), w.dtype)
    return jnp.concatenate([q, k, v, qi, ki, wi, pad, u, gv], axis=1).astype(BF16)


def _pack_w_cd(w):
    pad = jnp.zeros((D_MODEL, CD_COLS - w.shape[1]), w.dtype)
    return jnp.concatenate([w, pad], axis=1).astype(BF16)


def _spatial_blockdiag(w_s, b_s, tm, rows_per_chunk):
    wt = (w_s * jnp.tril(jnp.ones((CHUNK, CHUNK), w_s.dtype)))[:, :rows_per_chunk, :rows_per_chunk]
    n_blk = tm // rows_per_chunk
    eye = jnp.eye(n_blk, dtype=w_s.dtype)
    wsp = jnp.einsum("ab,gts->gatbs", eye, wt).reshape(GM_GROUPS, tm, tm).astype(BF16)
    bias = jnp.repeat(b_s[:, :rows_per_chunk].T, HEAD_DIM, axis=1)
    return wsp, jnp.tile(bias, (n_blk, 1))


def _pad_rows(x, rows):
    return jnp.pad(x, ((0, 0), (0, rows - x.shape[1]), (0, 0)))


def kernel(x_prompt, x_sample, cache_a_k, cache_a_v, cache_a_idx, cache_c_k, cache_c_v, cache_d_k, cache_d_v,
           cache_d_logf, page_table, c_prompt, c_sample, w_mod, b_mod, g_norm, w_in_ab, w_out_ab, gm_norm_v,
           gm_spatial_w, gm_spatial_b, w_in_cd, w_out_cd, fox_bias, lam_q1, lam_k1, lam_q2, lam_k2, g_subln,
           w_gate, w_up, w_down):
    b, s_len, _ = x_prompt.shape
    bd, t_new, _ = x_sample.shape
    n_pages = page_table.shape[1]
    past_len = n_pages * PAGE_SIZE
    n_pool = cache_a_k.shape[1]
    assert t_new == 8 and s_len % CHUNK == 0 and (bd * t_new) % 8 == 0
    mp, ms = b * s_len, bd * t_new

    mod = _modulation(jnp.concatenate([c_prompt, c_sample], axis=0), w_mod, b_mod)
    mod = mod.reshape(mod.shape[0], b + bd, 6, D_MODEL)

    def mods(layer):
        p = [mod[layer, :b, i].reshape(b, 1, D_MODEL) for i in range(6)]
        s = [jnp.repeat(mod[layer, b:, i], t_new, axis=0) for i in range(6)]
        return p, s

    tps_p = lambda tm: s_len // tm
    tps_s = lambda tm: 1
    rope_p = _rope_tables(np.arange(s_len))
    rope_s = _rope_tables(np.tile(past_len + np.arange(t_new), bd))
    gvec = lambda layer, i: g_norm[layer, i].reshape(1, D_MODEL)

    xp = x_prompt.reshape(mp, D_MODEL)
    xs = x_sample.reshape(ms, D_MODEL)
    pt = page_table.astype(jnp.int32)

    li = 0
    mod_p, mod_s = mods(0)
    w_ab = _pack_w_ab(w_in_ab[li])
    gm_g = gm_norm_v[li].reshape(1, GM_WIDTH)
    tm_p, tm_s = min(ROW_TILE, mp), min(ROW_TILE, ms)
    wsp_p, bsp_p = _spatial_blockdiag(gm_spatial_w[li], gm_spatial_b[li], tm_p, CHUNK)
    wsp_s, bsp_s = _spatial_blockdiag(gm_spatial_w[li], gm_spatial_b[li], tm_s, t_new)
    wo = w_out_ab[li]
    wo_a = wo[:512].reshape(A_HEADS, HEAD_DIM, D_MODEL)[np.array(A_HEAD_PERM)].reshape(512, D_MODEL).astype(BF16)
    wo_b = wo[512:].astype(BF16)
    n_sel_p = min(TOPK_MAX, s_len // 4)
    n_sel_s = min(TOPK_MAX, (past_len + t_new) // 4)

    (q_p, k32_p, v32_p, kiw_p, kbf_p, vbf_p, qi_p, kid_p, ob_p) = _inproj_ab(
        xp, gvec(0, 0), mod_p[0], mod_p[1], w_ab, rope_p, gm_g, wsp_p, bsp_p, tps_p(tm_p), False)
    (q_s, k32_s, v32_s, kiw_s, kbf_s, vbf_s, qi_s, kid_s, ob_s, gv_s) = _inproj_ab(
        xs, gvec(0, 0), mod_s[0], mod_s[1], w_ab, rope_s, gm_g, wsp_s, bsp_s, 1, True)

    r3 = lambda a: a.reshape(b, s_len, a.shape[-1])
    wi_p = kiw_p[:, IDX_DIM:IDX_DIM + IDX_HEADS]
    oa_p = _dsa_prompt(r3(qi_p), r3(wi_p), r3(kid_p), r3(q_p), r3(kbf_p), r3(vbf_p), n_sel_p)

    qi_stk = qi_s.reshape(bd, t_new, IDX_HEADS, IDX_DIM).transpose(0, 2, 1, 3).reshape(bd, 64, IDX_DIM)
    wi_col = kiw_s[:, IDX_DIM:IDX_DIM + IDX_HEADS].reshape(bd, t_new, IDX_HEADS).transpose(0, 2, 1).reshape(bd, 64, 1)
    ki_new = _pad_rows(kid_s[:, :IDX_DIM].reshape(bd, t_new, IDX_DIM), PAGE_SIZE)
    sel_bias = _dsa_select(pt, qi_stk, wi_col, ki_new, cache_a_idx[li], n_sel_s)

    q5 = q_s.reshape(bd, t_new, 4, 2, HEAD_DIM).transpose(0, 3, 2, 1, 4)
    zeros = jnp.zeros_like(q5[:, 0])
    q_stk = jnp.concatenate([jnp.concatenate([q5[:, 0], zeros], axis=-1),
                             jnp.concatenate([zeros, q5[:, 1]], axis=-1)], axis=1).reshape(bd, 64, LANES)
    o_s = _dsa_sample(pt, q_stk, sel_bias,
                      _pad_rows(kbf_s.reshape(bd, t_new, LANES), PAGE_SIZE),
                      _pad_rows(vbf_s.reshape(bd, t_new, LANES), PAGE_SIZE),
                      cache_a_k[li].reshape(n_pool, PAGE_SIZE, LANES), cache_a_v[li].reshape(n_pool, PAGE_SIZE, LANES))
    oa_s = o_s.reshape(ms, 512).astype(BF16)

    xp = _outproj(oa_p.reshape(mp, 512), ob_p, wo_a, wo_b, xp, mod_p[2], gvec(0, 1), tps_p)
    xs = _outproj(oa_s, ob_s, wo_a, wo_b, xs, mod_s[2], gvec(0, 1), tps_s)
    wg, wu, wd = w_gate[0].astype(BF16), w_up[0].astype(BF16), w_down[0].astype(BF16)
    xp = _ffn(xp, gvec(0, 2), mod_p[3], mod_p[4], wg, wu, wd, mod_p[5], gvec(0, 3), tps_p)
    xs = _ffn(xs, gvec(0, 2), mod_s[3], mod_s[4], wg, wu, wd, mod_s[5], gvec(0, 3), tps_s)

    a_k_p = k32_p.reshape(1, b, s_len, A_KV_HEADS, HEAD_DIM)
    a_v_p = v32_p.reshape(1, b, s_len, A_KV_HEADS, HEAD_DIM)
    a_idx_p = kiw_p[:, :IDX_DIM].reshape(1, b, s_len, IDX_DIM)
    a_k_s = k32_s.reshape(1, bd, t_new, A_KV_HEADS, HEAD_DIM)
    a_v_s = v32_s.reshape(1, bd, t_new, A_KV_HEADS, HEAD_DIM)
    a_idx_s = kiw_s[:, :IDX_DIM].reshape(1, bd, t_new, IDX_DIM)
    gm_v_s = gv_s.reshape(1, bd, t_new, GM_WIDTH)

    layer = 1
    mod_p, mod_s = mods(layer)
    lam_init = 0.8 - 0.6 * math.exp(-0.3 * layer)
    w_cd = _pack_w_cd(w_in_cd[li])
    fox_b = jnp.pad(fox_bias[li], (0, LANES - D_HEADS)).reshape(1, LANES)
    lamv = jnp.stack([lam_q1[li], lam_k1[li], lam_q2[li], lam_k2[li]])
    gsub = g_subln[li].reshape(1, 2 * HEAD_DIM)
    wo = w_out_cd[li]
    wo_c, wo_d = wo[:512].astype(BF16), wo[512:].astype(BF16)

    (cq_p, ck32_p, cv32_p, ckbf_p, cvbf_p, dq_p, dk32_p, dv32_p, dkbf_p, dvbf_p, lf_p) = _inproj_cd(
        xp, gvec(1, 0), mod_p[0], mod_p[1], w_cd, rope_p, fox_b, tps_p(tm_p))
    (cq_s, ck32_s, cv32_s, ckbf_s, cvbf_s, dq_s, dk32_s, dv32_s, dkbf_s, dvbf_s, lf_s) = _inproj_cd(
        xs, gvec(1, 0), mod_s[0], mod_s[1], w_cd, rope_s, fox_b, 1)

    oc_p = _diff_prompt(lamv, gsub, r3(cq_p), r3(ckbf_p), r3(cvbf_p), lam_init)
    logf_p = lf_p[:, :D_HEADS].reshape(b, s_len, D_HEADS)
    cum_t = _cumsum_lanes(logf_p.transpose(0, 2, 1))
    od_p = _fox_prompt(r3(dq_p), r3(dkbf_p), r3(dvbf_p), cum_t.transpose(0, 2, 1), cum_t)

    cq5 = cq_s.reshape(bd, t_new, 2 * C_HEADS, HEAD_DIM).transpose(0, 2, 1, 3)
    eye8 = jnp.eye(8, dtype=cq5.dtype)
    cq_stk = jnp.einsum("bhtd,hk->bhtkd", cq5, eye8).reshape(bd, 64, 512)
    pad3 = lambda a: _pad_rows(a.reshape(bd, t_new, 512), PAGE_SIZE)
    oc_s = _diff_sample(pt, lamv, gsub, cq_stk, pad3(ckbf_s), pad3(cvbf_s),
                        cache_c_k[li].reshape(n_pool, PAGE_SIZE, 512), cache_c_v[li].reshape(n_pool, PAGE_SIZE, 512),
                        lam_init)

    dq4 = dq_s.reshape(bd, t_new, D_HEADS, HEAD_DIM)
    dq_stk = jnp.einsum("bthd,hk->bthkd", dq4, eye8.astype(dq4.dtype)).reshape(bd, 64, 512)
    logf_s = lf_s[:, :D_HEADS].reshape(bd, t_new, D_HEADS)
    lf_col = logf_s.reshape(bd, 64, 1)
    lf_t = jnp.pad(logf_s.transpose(0, 2, 1), ((0, 0), (0, 0), (0, LANES - t_new)))
    od_s = _fox_sample(pt, dq_stk, lf_col, lf_t, pad3(dkbf_s), pad3(dvbf_s),
                       cache_d_k[li].reshape(n_pool, PAGE_SIZE, 512), cache_d_v[li].reshape(n_pool, PAGE_SIZE, 512),
                       cache_d_logf[li].transpose(0, 2, 1))

    xp = _outproj(oc_p.reshape(mp, 512), od_p.reshape(mp, 512), wo_c, wo_d, xp, mod_p[2], gvec(1, 1), tps_p)
    xs = _outproj(oc_s.reshape(ms, 512).astype(BF16), od_s.reshape(ms, 512).astype(BF16), wo_c, wo_d, xs,
                  mod_s[2], gvec(1, 1), tps_s)
    wg, wu, wd = w_gate[1].astype(BF16), w_up[1].astype(BF16), w_down[1].astype(BF16)
    xp = _ffn(xp, gvec(1, 2), mod_p[3], mod_p[4], wg, wu, wd, mod_p[5], gvec(1, 3), tps_p)
    xs = _ffn(xs, gvec(1, 2), mod_s[3], mod_s[4], wg, wu, wd, mod_s[5], gvec(1, 3), tps_s)

    def shp(a, g, *tail):
        return a.reshape((1,) + g + tail)

    gp, gs = (b, s_len), (bd, t_new)
    return (xp.reshape(b, s_len, D_MODEL), xs.reshape(bd, t_new, D_MODEL),
            a_k_p, a_v_p, a_idx_p,
            shp(ck32_p, gp, C_HEADS, 2, HEAD_DIM), shp(cv32_p, gp, C_HEADS, 2 * HEAD_DIM),
            shp(dk32_p, gp, D_HEADS, HEAD_DIM), shp(dv32_p, gp, D_HEADS, HEAD_DIM), shp(logf_p, gp, D_HEADS),
            a_k_s, a_v_s, a_idx_s,
            shp(ck32_s, gs, C_HEADS, 2, HEAD_DIM), shp(cv32_s, gs, C_HEADS, 2 * HEAD_DIM),
            shp(dk32_s, gs, D_HEADS, HEAD_DIM), shp(dv32_s, gs, D_HEADS, HEAD_DIM), shp(logf_s, gs, D_HEADS),
            gm_v_s)
```

```python
import functools
import math

import numpy as np
import jax
import jax.numpy as jnp
from jax import lax
from jax.experimental import pallas as pl
from jax.experimental.pallas import tpu as pltpu

F32 = jnp.float32
BF16 = jnp.bfloat16
NEG_INF = float("-inf")

D_MODEL = 1024
HEAD_DIM = 64
ROPE_THETA = 500000.0
A_HEADS = 8
A_KV_HEADS = 2
IDX_HEADS = 8
IDX_DIM = 64
TOPK_MAX = 256
GM_GROUPS = 8
GM_WIDTH = GM_GROUPS * HEAD_DIM
CHUNK = 128
C_HEADS = 4
D_HEADS = 8
PAGE_SIZE = 128
EPS = 1e-6
D_FF = -(-(8 * D_MODEL) // (3 * 256)) * 256

LANES = 128
VMEM_LIMIT_BYTES = 56 * 1024 * 1024

ROW_TILE = 256
FFN_ROW_TILE = 256
Q_TILE = 128
CAUSAL_SEGMENTS = 4
PAGES_PER_STEP = 8
MOD_COL_TILE = 1536
BISECT_STEPS = 28

AB_Q, AB_K, AB_V, AB_QI, AB_KIW, AB_U, AB_GV, AB_COLS = 0, 512, 640, 768, 1280, 1408, 1920, 2432
CD_CQ, CD_CK, CD_CV, CD_DQ, CD_DK, CD_DV, CD_DF, CD_COLS = 0, 512, 1024, 1536, 2048, 2560, 3072, 3200
A_HEAD_PERM = (0, 4, 1, 5, 2, 6, 3, 7)


def _cparams(*sem):
    return pltpu.CompilerParams(dimension_semantics=sem, vmem_limit_bytes=VMEM_LIMIT_BYTES)


def _nt(a, b):
    return lax.dot_general(a, b, (((1,), (1,)), ((), ())), preferred_element_type=F32)


def _mm(a, b):
    return jnp.dot(a, b, preferred_element_type=F32)


def _rms(x, g):
    return x * lax.rsqrt(jnp.mean(x * x, axis=-1, keepdims=True) + EPS) * g


def _silu(x):
    return x / (1.0 + jnp.exp(-x))


def _gelu_tanh(x):
    return 0.5 * x * (1.0 + jnp.tanh(math.sqrt(2.0 / math.pi) * (x + 0.044715 * (x * x * x))))


def _half_masks(dtype):
    lane = lax.broadcasted_iota(jnp.int32, (1, LANES), 1)
    lo = (lane < HEAD_DIM).astype(dtype)
    return lo, (1 - lo).astype(dtype)


def _rope(z, c, s1, s2):
    outs = []
    for j in range(z.shape[1] // LANES):
        x = z[:, j * LANES:(j + 1) * LANES]
        outs.append(x * c + pltpu.roll(x, LANES - 8, 1) * s1 + pltpu.roll(x, 8, 1) * s2)
    return outs[0] if len(outs) == 1 else jnp.concatenate(outs, axis=1)


def _split3(x):
    hi = x.astype(BF16)
    r1 = x - hi.astype(F32)
    mid = r1.astype(BF16)
    lo = (r1 - mid.astype(F32)).astype(BF16)
    return hi, mid, lo


def _lane_prefix(x, tri):
    hi, mid, lo = _split3(x)
    r = _mm(jnp.concatenate([hi, mid, lo], axis=0), tri)
    return r[0:8] + r[8:16] + r[16:24]


def _tri(incl_upper=None, strict_lower=None):
    r = lax.broadcasted_iota(jnp.int32, (LANES, LANES), 0)
    c = lax.broadcasted_iota(jnp.int32, (LANES, LANES), 1)
    if incl_upper:
        return (r <= c).astype(BF16)
    return (r > c).astype(BF16)


def _count_ge(sm, x):
    return jnp.sum(jnp.where(sm >= x, 1.0, 0.0), axis=1, keepdims=True)


def _topk_bias(sm_ref, bias_ref, n_sel):
    kf = float(n_sel)
    sm = sm_ref[...]
    allowed = sm > NEG_INF
    mx = jnp.max(sm, axis=1, keepdims=True)
    mn = jnp.min(jnp.where(allowed, sm, jnp.inf), axis=1, keepdims=True)
    n_allowed = jnp.sum(jnp.where(allowed, 1.0, 0.0), axis=1, keepdims=True)
    few = n_allowed <= kf
    top_tie = _count_ge(sm, mx) >= kf

    def bisect(_, c):
        lo, hi = c
        mid = lo + (hi - lo) * 0.5
        ge = _count_ge(sm_ref[...], mid) >= kf
        return jnp.where(ge, mid, lo), jnp.where(ge, hi, mid)

    _, hi = lax.fori_loop(0, BISECT_STEPS, bisect, (mn, mx))

    done0 = jnp.where(few, 1.0, jnp.where(top_tie, 1.0, 0.0))
    thr0 = jnp.where(few, NEG_INF, mx)

    def not_done(c):
        return c[3] > 0.0

    def step(c):
        hi, thr, done, _ = c
        s = sm_ref[...]
        m = jnp.max(jnp.where(s < hi, s, NEG_INF), axis=1, keepdims=True)
        found = _count_ge(s, m) >= kf
        is_done = done > 0.5
        thr = jnp.where(is_done, thr, jnp.where(found, m, thr))
        hi = jnp.where(is_done, hi, jnp.where(found, hi, m))
        done = jnp.where(found, 1.0, done)
        return hi, thr, done, jnp.sum(1.0 - done)

    _, thr, _, _ = lax.while_loop(not_done, step, (hi, thr0, done0, jnp.sum(1.0 - done0)))

    is_gt = sm > thr
    is_eq = jnp.where(sm == thr, jnp.where(allowed, 1.0, 0.0), 0.0)
    need = kf - jnp.sum(jnp.where(is_gt, 1.0, 0.0), axis=1, keepdims=True)
    n_eq = jnp.sum(is_eq, axis=1, keepdims=True)
    bias_ref[...] = jnp.where(sm >= thr, jnp.where(allowed, 0.0, NEG_INF), NEG_INF)

    @pl.when(jnp.max(n_eq - need) > 0.0)
    def _():
        tri = _tri(incl_upper=True)

        def blk(j, run):
            cols = pl.ds(pl.multiple_of(j * LANES, LANES), LANES)
            s = sm_ref[:, cols]
            e = jnp.where(s == thr, jnp.where(s > NEG_INF, 1.0, 0.0), 0.0)
            rank = run + _mm(e.astype(BF16), tri)
            keep = jnp.where(e > 0.5, jnp.where(rank <= need, 1.0, 0.0), 0.0)
            bias_ref[:, cols] = jnp.where(s > thr, 0.0, jnp.where(keep > 0.5, 0.0, NEG_INF))
            return run + jnp.sum(e, axis=1, keepdims=True)

        lax.fori_loop(0, sm_ref.shape[1] // LANES, blk, jnp.zeros_like(need))


def _mod_kernel(c_ref, w_ref, b_ref, o_ref):
    a = _silu(c_ref[...]).astype(BF16)
    o_ref[...] = _mm(a, w_ref[...].astype(BF16)) + b_ref[...]


def _modulation(c_all, w_mod, b_mod):
    n_layers, _, n_out = w_mod.shape
    rows = c_all.shape[0]
    tn = MOD_COL_TILE
    return pl.pallas_call(
        _mod_kernel,
        grid=(n_layers, n_out // tn),
        in_specs=[
            pl.BlockSpec((rows, D_MODEL), lambda l, n: (0, 0)),
            pl.BlockSpec((None, D_MODEL, tn), lambda l, n: (l, 0, n)),
            pl.BlockSpec((None, 1, tn), lambda l, n: (l, 0, n)),
        ],
        out_specs=pl.BlockSpec((None, rows, tn), lambda l, n: (l, 0, n)),
        out_shape=jax.ShapeDtypeStruct((n_layers, rows, n_out), F32),
        compiler_params=_cparams("arbitrary", "arbitrary"),
        name="modulation",
    )(c_all, w_mod, b_mod.reshape(n_layers, 1, n_out))


def _rowvec_spec(vec, tm, tiles_per_seq):
    if vec.ndim == 3:
        return pl.BlockSpec((None, 1, vec.shape[-1]), lambda i: (i // tiles_per_seq, 0, 0))
    return pl.BlockSpec((tm, vec.shape[-1]), lambda i: (i, 0))


def _const_spec(arr):
    nd = arr.ndim
    return pl.BlockSpec(arr.shape, lambda i: (0,) * nd)


def _inproj_ab_kernel(x_ref, g_ref, sh_ref, sc_ref, w_ref, rc_ref, rs1_ref, rs2_ref, gmg_ref, wsp_ref, bsp_ref,
                      q_ref, k32_ref, v32_ref, kiw_ref, kbf_ref, vbf_ref, qi_ref, kid_ref, ob_ref, *gv_ref):
    h = (_rms(x_ref[...], g_ref[...]) * (1.0 + sc_ref[...]) + sh_ref[...]).astype(BF16)
    c, s1, s2 = rc_ref[...], rs1_ref[...], rs2_ref[...]

    def seg(lo, hi):
        return _mm(h, w_ref[:, lo:hi])

    q_ref[...] = (_rope(seg(AB_Q, AB_K), c, s1, s2) * HEAD_DIM ** -0.5).astype(BF16)
    k = _rope(seg(AB_K, AB_V), c, s1, s2)
    k32_ref[...] = k
    kbf_ref[...] = k.astype(BF16)
    v = seg(AB_V, AB_QI)
    v32_ref[...] = v
    vbf_ref[...] = v.astype(BF16)
    qi_ref[...] = _rope(seg(AB_QI, AB_KIW), c, s1, s2).astype(BF16)
    kz = seg(AB_KIW, AB_U)
    kr = _rope(kz, c, s1, s2)
    lane = lax.broadcasted_iota(jnp.int32, kz.shape, 1)
    kiw_ref[...] = jnp.where(lane < IDX_DIM, kr, kz)
    kid_ref[...] = jnp.where(lane < IDX_DIM, kr, pltpu.roll(kr, IDX_DIM, 1)).astype(BF16)

    u = _gelu_tanh(seg(AB_U, AB_GV))
    gv = _gelu_tanh(seg(AB_GV, AB_COLS))
    gc = gv - jnp.mean(gv, axis=-1, keepdims=True)
    gvn = gc * lax.rsqrt(jnp.mean(gc * gc, axis=-1, keepdims=True) + EPS) * gmg_ref[...]
    if gv_ref:
        gv_ref[0][...] = gvn
    gvb = gvn.astype(BF16)
    lane = lax.broadcasted_iota(jnp.int32, (1, LANES), 1)
    mix = []
    for j in range(GM_GROUPS // 2):
        slab = gvb[:, j * LANES:(j + 1) * LANES]
        mix.append(jnp.where(lane < HEAD_DIM, _mm(wsp_ref[2 * j], slab), _mm(wsp_ref[2 * j + 1], slab)))
    ob_ref[...] = (u * (jnp.concatenate(mix, axis=1) + bsp_ref[...])).astype(BF16)


def _inproj_ab(x, g, shift, scale, w, rope_tabs, gm_g, wsp, bsp, tiles_per_seq, want_gv):
    m = x.shape[0]
    tm = min(ROW_TILE, m)
    n_tab_tiles = rope_tabs[0].shape[0] // tm
    row = lambda width: pl.BlockSpec((tm, width), lambda i: (i, 0))
    tab = pl.BlockSpec((tm, LANES), lambda i: (i % n_tab_tiles, 0))
    sds = lambda width, dt: jax.ShapeDtypeStruct((m, width), dt)
    out_shape = [sds(512, BF16), sds(128, F32), sds(128, F32), sds(128, F32), sds(128, BF16), sds(128, BF16),
                 sds(512, BF16), sds(128, BF16), sds(512, BF16)]
    out_specs = [row(512), row(128), row(128), row(128), row(128), row(128), row(512), row(128), row(512)]
    if want_gv:
        out_shape.append(sds(512, F32))
        out_specs.append(row(512))
    return pl.pallas_call(
        _inproj_ab_kernel,
        grid=(m // tm,),
        in_specs=[row(D_MODEL), _const_spec(g), _rowvec_spec(shift, tm, tiles_per_seq),
                  _rowvec_spec(scale, tm, tiles_per_seq), _const_spec(w), tab, tab, tab,
                  _const_spec(gm_g), _const_spec(wsp), _const_spec(bsp)],
        out_specs=out_specs,
        out_shape=out_shape,
        compiler_params=_cparams("parallel"),
        name="inproj_ab",
    )(x, g, shift, scale, w, *rope_tabs, gm_g, wsp, bsp)


def _inproj_cd_kernel(x_ref, g_ref, sh_ref, sc_ref, w_ref, rc_ref, rs1_ref, rs2_ref, fb_ref,
                      cq_ref, ck32_ref, cv32_ref, ckbf_ref, cvbf_ref,
                      dq_ref, dk32_ref, dv32_ref, dkbf_ref, dvbf_ref, lf_ref):
    h = (_rms(x_ref[...], g_ref[...]) * (1.0 + sc_ref[...]) + sh_ref[...]).astype(BF16)
    c, s1, s2 = rc_ref[...], rs1_ref[...], rs2_ref[...]

    def seg(lo, hi):
        return _mm(h, w_ref[:, lo:hi])

    cq_ref[...] = (_rope(seg(CD_CQ, CD_CK), c, s1, s2) * HEAD_DIM ** -0.5).astype(BF16)
    ck = _rope(seg(CD_CK, CD_CV), c, s1, s2)
    ck32_ref[...] = ck
    ckbf_ref[...] = ck.astype(BF16)
    cv = seg(CD_CV, CD_DQ)
    cv32_ref[...] = cv
    cvbf_ref[...] = cv.astype(BF16)
    dq_ref[...] = (seg(CD_DQ, CD_DK) * HEAD_DIM ** -0.5).astype(BF16)
    dk = seg(CD_DK, CD_DV)
    dk32_ref[...] = dk
    dkbf_ref[...] = dk.astype(BF16)
    dv = seg(CD_DV, CD_DF)
    dv32_ref[...] = dv
    dvbf_ref[...] = dv.astype(BF16)
    f = seg(CD_DF, CD_COLS) + fb_ref[...]
    lf_ref[...] = jnp.minimum(f, 0.0) - jnp.log(1.0 + jnp.exp(-jnp.abs(f)))


def _inproj_cd(x, g, shift, scale, w, rope_tabs, fox_b, tiles_per_seq):
    m = x.shape[0]
    tm = min(ROW_TILE, m)
    n_tab_tiles = rope_tabs[0].shape[0] // tm
    row = lambda width: pl.BlockSpec((tm, width), lambda i: (i, 0))
    tab = pl.BlockSpec((tm, LANES), lambda i: (i % n_tab_tiles, 0))
    sds = lambda width, dt: jax.ShapeDtypeStruct((m, width), dt)
    per = [sds(512, BF16), sds(512, F32), sds(512, F32), sds(512, BF16), sds(512, BF16)]
    return pl.pallas_call(
        _inproj_cd_kernel,
        grid=(m // tm,),
        in_specs=[row(D_MODEL), _const_spec(g), _rowvec_spec(shift, tm, tiles_per_seq),
                  _rowvec_spec(scale, tm, tiles_per_seq), _const_spec(w), tab, tab, tab, _const_spec(fox_b)],
        out_specs=[row(512)] * 10 + [row(128)],
        out_shape=per + per + [sds(128, F32)],
        compiler_params=_cparams("parallel"),
        name="inproj_cd",
    )(x, g, shift, scale, w, *rope_tabs, fox_b)


def _outproj_kernel(a_ref, b_ref, wa_ref, wb_ref, x_ref, gate_ref, g_ref, o_ref):
    op = _mm(a_ref[...], wa_ref[...]) + _mm(b_ref[...], wb_ref[...])
    o_ref[...] = x_ref[...] + gate_ref[...] * _rms(op, g_ref[...])


def _outproj(a, b, wa, wb, x, gate, g, tiles_per_seq_of):
    m = x.shape[0]
    tm = min(FFN_ROW_TILE, m)
    row = lambda width: pl.BlockSpec((tm, width), lambda i: (i, 0))
    return pl.pallas_call(
        _outproj_kernel,
        grid=(m // tm,),
        in_specs=[row(a.shape[1]), row(b.shape[1]), _const_spec(wa), _const_spec(wb), row(D_MODEL),
                  _rowvec_spec(gate, tm, tiles_per_seq_of(tm)), _const_spec(g)],
        out_specs=row(D_MODEL),
        out_shape=jax.ShapeDtypeStruct((m, D_MODEL), F32),
        compiler_params=_cparams("parallel"),
        name="outproj",
    )(a, b, wa, wb, x, gate, g)


def _ffn_kernel(x_ref, g2_ref, sh_ref, sc_ref, wg_ref, wu_ref, wd_ref, gate_ref, g3_ref, o_ref):
    x = x_ref[...]
    h = (_rms(x, g2_ref[...]) * (1.0 + sc_ref[...]) + sh_ref[...]).astype(BF16)
    act = (_silu(_mm(h, wg_ref[...])) * _mm(h, wu_ref[...])).astype(BF16)
    o_ref[...] = x + gate_ref[...] * _rms(_mm(act, wd_ref[...]), g3_ref[...])


def _ffn(x, g2, shift, scale, wg, wu, wd, gate, g3, tiles_per_seq_of):
    m = x.shape[0]
    tm = min(FFN_ROW_TILE, m)
    tps = tiles_per_seq_of(tm)
    row = pl.BlockSpec((tm, D_MODEL), lambda i: (i, 0))
    return pl.pallas_call(
        _ffn_kernel,
        grid=(m // tm,),
        in_specs=[row, _const_spec(g2), _rowvec_spec(shift, tm, tps), _rowvec_spec(scale, tm, tps),
                  _const_spec(wg), _const_spec(wu), _const_spec(wd), _rowvec_spec(gate, tm, tps), _const_spec(g3)],
        out_specs=row,
        out_shape=jax.ShapeDtypeStruct((m, D_MODEL), F32),
        compiler_params=_cparams("parallel"),
        name="ffn",
    )(x, g2, shift, scale, wg, wu, wd, gate, g3)


def _dsa_prompt_kernel(qi_ref, wi_ref, kid_ref, q_ref, k_ref, v_ref, o_ref, sm_ref, bias_ref, *, n_sel, q_blk0):
    tq, s_len = sm_ref.shape
    i = q_blk0 + pl.program_id(1)
    m_lo, m_hi = _half_masks(BF16)
    kid = kid_ref[...]
    score = jnp.zeros((tq, s_len), F32)
    for j in range(IDX_HEADS // 2):
        slab = qi_ref[:, j * LANES:(j + 1) * LANES]
        s = _nt(jnp.concatenate([slab * m_lo, slab * m_hi], axis=0), kid)
        score = score + jnp.maximum(s[:tq], 0.0) * wi_ref[:, 2 * j:2 * j + 1]
        score = score + jnp.maximum(s[tq:], 0.0) * wi_ref[:, 2 * j + 1:2 * j + 2]
    score = score * (IDX_HEADS * IDX_DIM) ** -0.5
    row = i * tq + lax.broadcasted_iota(jnp.int32, (tq, s_len), 0)
    col = lax.broadcasted_iota(jnp.int32, (tq, s_len), 1)
    sm_ref[...] = jnp.where(col <= row, score, NEG_INF)
    _topk_bias(sm_ref, bias_ref, n_sel)

    k = k_ref[...]
    v = v_ref[...]
    n_slab = A_HEADS // 2
    outs = []
    for mask in (m_lo, m_hi):
        stk = jnp.concatenate([q_ref[:, j * LANES:(j + 1) * LANES] * mask for j in range(n_slab)], axis=0)
        lg = _nt(stk, k)
        per = []
        for j in range(n_slab):
            l = lg[j * tq:(j + 1) * tq] + bias_ref[...]
            p = jnp.exp(l - jnp.max(l, axis=1, keepdims=True))
            per.append(_mm(p.astype(BF16), v) / jnp.sum(p, axis=1, keepdims=True))
        outs.append(per)
    lane = lax.broadcasted_iota(jnp.int32, (1, LANES), 1)
    for j in range(n_slab):
        o_ref[:, j * LANES:(j + 1) * LANES] = jnp.where(lane < HEAD_DIM, outs[0][j], outs[1][j]).astype(BF16)


def _causal_segments(s_len, tq):
    n_seg = min(CAUSAL_SEGMENTS, s_len // tq)
    per = s_len // tq // n_seg
    return [(seg * per, per, (seg + 1) * per * tq) for seg in range(n_seg)]


def _dsa_prompt(qi, wi, kid, q, k, v, n_sel):
    b, s_len, _ = q.shape
    tq = min(Q_TILE, s_len)
    outs = []
    for q_blk0, n_blk, kext in _causal_segments(s_len, tq):
        qblk = lambda width: pl.BlockSpec((None, tq, width), lambda bb, i: (bb, q_blk0 + i, 0))
        keys = lambda width: pl.BlockSpec((None, kext, width), lambda bb, i: (bb, 0, 0))
        outs.append(pl.pallas_call(
            functools.partial(_dsa_prompt_kernel, n_sel=n_sel, q_blk0=q_blk0),
            grid=(b, n_blk),
            in_specs=[qblk(512), qblk(IDX_HEADS), keys(128), qblk(512), keys(128), keys(128)],
            out_specs=pl.BlockSpec((None, tq, 512), lambda bb, i: (bb, i, 0)),
            out_shape=jax.ShapeDtypeStruct((b, n_blk * tq, 512), BF16),
            scratch_shapes=[pltpu.VMEM((tq, kext), F32), pltpu.VMEM((tq, kext), F32)],
            compiler_params=_cparams("parallel", "arbitrary"),
            name="dsa_prompt",
        )(qi, wi, kid, q, k, v))
    return jnp.concatenate(outs, axis=1)


def _diff_lambda(lamv_ref, lam_init):
    lv = lamv_ref[...]
    d1 = jnp.sum(lv[0:1] * lv[1:2], axis=1, keepdims=True)
    d2 = jnp.sum(lv[2:3] * lv[3:4], axis=1, keepdims=True)
    return jnp.exp(d1) - jnp.exp(d2) + lam_init


def _diff_prompt_kernel(lamv_ref, gsub_ref, q_ref, k_ref, v_ref, o_ref, *, lam_init, q_blk0):
    tq = q_ref.shape[0]
    s_len = k_ref.shape[0]
    i = q_blk0 + pl.program_id(1)
    lam = _diff_lambda(lamv_ref, lam_init)
    m_lo, m_hi = _half_masks(BF16)
    row = i * tq + lax.broadcasted_iota(jnp.int32, (tq, s_len), 0)
    col = lax.broadcasted_iota(jnp.int32, (tq, s_len), 1)
    causal = col <= row
    for h in range(C_HEADS):
        cols = slice(h * LANES, (h + 1) * LANES)
        slab = q_ref[:, cols]
        lg = _nt(jnp.concatenate([slab * m_lo, slab * m_hi], axis=0), k_ref[:, cols])
        l0 = jnp.where(causal, lg[:tq], NEG_INF)
        l1 = jnp.where(causal, lg[tq:], NEG_INF)
        p0 = jnp.exp(l0 - jnp.max(l0, axis=1, keepdims=True))
        p1 = jnp.exp(l1 - jnp.max(l1, axis=1, keepdims=True))
        w = p0 * (1.0 / jnp.sum(p0, axis=1, keepdims=True)) - p1 * (lam / jnp.sum(p1, axis=1, keepdims=True))
        o = _mm(w.astype(BF16), v_ref[:, cols])
        o_ref[:, cols] = (_rms(o, gsub_ref[...]) * (1.0 - lam_init)).astype(BF16)


def _diff_prompt(lamv, gsub, q, k, v, lam_init):
    b, s_len, _ = q.shape
    tq = min(Q_TILE, s_len)
    cst = lambda arr: pl.BlockSpec(arr.shape, lambda bb, i: (0, 0))
    outs = []
    for q_blk0, n_blk, kext in _causal_segments(s_len, tq):
        qblk = pl.BlockSpec((None, tq, 512), lambda bb, i: (bb, q_blk0 + i, 0))
        keys = pl.BlockSpec((None, kext, 512), lambda bb, i: (bb, 0, 0))
        outs.append(pl.pallas_call(
            functools.partial(_diff_prompt_kernel, lam_init=lam_init, q_blk0=q_blk0),
            grid=(b, n_blk),
            in_specs=[cst(lamv), cst(gsub), qblk, keys, keys],
            out_specs=pl.BlockSpec((None, tq, 512), lambda bb, i: (bb, i, 0)),
            out_shape=jax.ShapeDtypeStruct((b, n_blk * tq, 512), BF16),
            compiler_params=_cparams("parallel", "arbitrary"),
            name="diff_prompt",
        )(lamv, gsub, q, k, v))
    return jnp.concatenate(outs, axis=1)


def _cumsum_kernel(x_ref, o_ref):
    tri = _tri(incl_upper=True)
    carry = jnp.zeros((8, 1), F32)
    for c in range(x_ref.shape[1] // LANES):
        cols = slice(c * LANES, (c + 1) * LANES)
        out = _lane_prefix(x_ref[:, cols], tri) + carry
        o_ref[:, cols] = out
        carry = out[:, LANES - 1:LANES]


def _cumsum_lanes(x):
    b, h, s_len = x.shape
    blk = pl.BlockSpec((None, h, s_len), lambda bb: (bb, 0, 0))
    return pl.pallas_call(
        _cumsum_kernel,
        grid=(b,),
        in_specs=[blk],
        out_specs=blk,
        out_shape=jax.ShapeDtypeStruct(x.shape, F32),
        compiler_params=_cparams("parallel"),
        name="fox_cumsum",
    )(x)


def _fox_prompt_kernel(q_ref, k_ref, v_ref, cq_ref, ck_ref, o_ref, *, q_blk0):
    tq = q_ref.shape[0]
    s_len = k_ref.shape[0]
    i = q_blk0 + pl.program_id(1)
    m_lo, m_hi = _half_masks(BF16)
    row = i * tq + lax.broadcasted_iota(jnp.int32, (tq, s_len), 0)
    col = lax.broadcasted_iota(jnp.int32, (tq, s_len), 1)
    causal = col <= row
    lane = lax.broadcasted_iota(jnp.int32, (1, LANES), 1)
    for j in range(D_HEADS // 2):
        cols = slice(j * LANES, (j + 1) * LANES)
        slab = q_ref[:, cols]
        lg = _nt(jnp.concatenate([slab * m_lo, slab * m_hi], axis=0), k_ref[:, cols])
        v = v_ref[:, cols]
        pv = []
        for hh in range(2):
            h = 2 * j + hh
            l = lg[hh * tq:(hh + 1) * tq] + cq_ref[:, h:h + 1] - ck_ref[h:h + 1, :]
            l = jnp.where(causal, l, NEG_INF)
            p = jnp.exp(l - jnp.max(l, axis=1, keepdims=True))
            pv.append(_mm(p.astype(BF16), v) / jnp.sum(p, axis=1, keepdims=True))
        o_ref[:, cols] = jnp.where(lane < HEAD_DIM, pv[0], pv[1]).astype(BF16)


def _fox_prompt(q, k, v, cum, cum_t):
    b, s_len, _ = q.shape
    tq = min(Q_TILE, s_len)
    outs = []
    for q_blk0, n_blk, kext in _causal_segments(s_len, tq):
        qblk = lambda width: pl.BlockSpec((None, tq, width), lambda bb, i: (bb, q_blk0 + i, 0))
        keys = pl.BlockSpec((None, kext, 512), lambda bb, i: (bb, 0, 0))
        outs.append(pl.pallas_call(
            functools.partial(_fox_prompt_kernel, q_blk0=q_blk0),
            grid=(b, n_blk),
            in_specs=[qblk(512), keys, keys, qblk(D_HEADS),
                      pl.BlockSpec((None, D_HEADS, kext), lambda bb, i: (bb, 0, 0))],
            out_specs=pl.BlockSpec((None, tq, 512), lambda bb, i: (bb, i, 0)),
            out_shape=jax.ShapeDtypeStruct((b, n_blk * tq, 512), BF16),
            compiler_params=_cparams("parallel", "arbitrary"),
            name="fox_prompt",
        )(q, k, v, cum, cum_t))
    return jnp.concatenate(outs, axis=1)


def _page_specs(block, n_pg, reverse_from=None):
    specs = []
    for pg in range(n_pg):
        if reverse_from is None:
            idx = lambda b, st, pt, pg=pg: (pt[b, st * n_pg + pg],) + (0,) * (len(block) - 1)
        else:
            idx = lambda b, st, pt, pg=pg: (pt[b, reverse_from - (st * n_pg + pg)],) + (0,) * (len(block) - 1)
        specs.append(pl.BlockSpec((None,) + tuple(block[1:]), idx))
    return specs


def _per_seq(shape):
    nd = len(shape)
    return pl.BlockSpec((None,) + tuple(shape[1:]), lambda b, st, pt: (b,) + (0,) * (nd - 1))


def _softmax_init(m_ref, l_ref, acc_ref):
    m_ref[...] = jnp.full(m_ref.shape, NEG_INF, F32)
    l_ref[...] = jnp.zeros(l_ref.shape, F32)
    acc_ref[...] = jnp.zeros(acc_ref.shape, F32)


def _attend(q_ref, keys_of, values_of, n_v, bias, m_ref, l_ref, acc_ref):
    n_k, r, _ = q_ref.shape
    lg = jnp.concatenate([_nt(q_ref[i], keys_of(i)) for i in range(n_k)], axis=0) + bias
    m_old = m_ref[...]
    m_new = jnp.maximum(m_old, jnp.max(lg, axis=1, keepdims=True))
    m_safe = jnp.where(m_new == NEG_INF, 0.0, m_new)
    alpha = jnp.exp(m_old - m_safe)
    p = jnp.exp(lg - m_safe)
    l_ref[...] = alpha * l_ref[...] + jnp.sum(p, axis=1, keepdims=True)
    m_ref[...] = m_new
    p = p.astype(BF16)
    rv = r * (n_k // n_v)
    for j in range(n_v):
        rows = slice(j * rv, (j + 1) * rv)
        acc_ref[rows, :] = alpha[rows] * acc_ref[rows, :] + _mm(p[rows], values_of(j))


def _page_rows(pages, *idx):
    take = (slice(None),) + idx + (slice(None),)
    return jnp.concatenate([p[take].astype(BF16) for p in pages], axis=0)


def _rows8(x):
    return jnp.concatenate([jnp.broadcast_to(x[h:h + 1], (8, x.shape[1])) for h in range(x.shape[0])], axis=0)


def _head_sum(w, n_groups):
    out = w[0:8]
    for h in range(1, n_groups):
        out = out + w[8 * h:8 * h + 8]
    return out


def _new_key_mask(rows):
    t = lax.broadcasted_iota(jnp.int32, (rows, LANES), 0) % 8
    c = lax.broadcasted_iota(jnp.int32, (rows, LANES), 1)
    return c <= t


def _dsa_select_kernel(pt_ref, q_ref, w_ref, knew_ref, *rest, n_pg, n_sel):
    pages = rest[:n_pg]
    bias_ref, sm_ref = rest[n_pg], rest[n_pg + 1]
    st = pl.program_id(1)
    q = q_ref[...]
    wcol = w_ref[...]
    scale = (IDX_HEADS * IDX_DIM) ** -0.5

    def scores(keys):
        return _head_sum(jnp.maximum(_nt(q, keys), 0.0) * wcol, IDX_HEADS) * scale

    step_keys = n_pg * PAGE_SIZE
    col0 = pl.multiple_of(st * step_keys, step_keys)
    sm_ref[:, pl.ds(col0, step_keys)] = scores(_page_rows(pages))

    @pl.when(st == pl.num_programs(1) - 1)
    def _():
        past = sm_ref.shape[1] - LANES
        sm_ref[:, past:] = jnp.where(_new_key_mask(8), scores(knew_ref[...]), NEG_INF)
        _topk_bias(sm_ref, bias_ref, n_sel)


def _dsa_select(page_table, q, wcol, knew, cache_idx, n_sel):
    bd, n_pages = page_table.shape
    n_pg = min(PAGES_PER_STEP, n_pages)
    width = n_pages * PAGE_SIZE + LANES
    return pl.pallas_call(
        functools.partial(_dsa_select_kernel, n_pg=n_pg, n_sel=n_sel),
        grid_spec=pltpu.PrefetchScalarGridSpec(
            num_scalar_prefetch=1,
            grid=(bd, n_pages // n_pg),
            in_specs=[_per_seq(q.shape), _per_seq(wcol.shape), _per_seq(knew.shape)]
                     + _page_specs(cache_idx.shape, n_pg),
            out_specs=pl.BlockSpec((None, 8, width), lambda b, st, pt: (b, 0, 0)),
            scratch_shapes=[pltpu.VMEM((8, width), F32)],
        ),
        out_shape=jax.ShapeDtypeStruct((bd, 8, width), F32),
        compiler_params=_cparams("parallel", "arbitrary"),
        name="dsa_select",
    )(page_table, q, wcol, knew, *([cache_idx] * n_pg))


def _dsa_sample_kernel(pt_ref, q_ref, bias_ref, knew_ref, vnew_ref, *rest, n_pg):
    kpages, vpages = rest[:n_pg], rest[n_pg:2 * n_pg]
    o_ref, m_ref, l_ref, acc_ref = rest[2 * n_pg:]
    st = pl.program_id(1)

    @pl.when(st == 0)
    def _():
        _softmax_init(m_ref, l_ref, acc_ref)

    n_g, rg, _ = q_ref.shape
    n_rep = n_g * rg // 8

    def bias_rows(cols):
        bt = bias_ref[:, cols]
        return jnp.concatenate([bt] * n_rep, axis=0)

    step_keys = n_pg * PAGE_SIZE
    col0 = pl.multiple_of(st * step_keys, step_keys)
    _attend(q_ref, lambda g: _page_rows(kpages, g), lambda g: _page_rows(vpages, g), n_g,
            bias_rows(pl.ds(col0, step_keys)), m_ref, l_ref, acc_ref)

    @pl.when(st == pl.num_programs(1) - 1)
    def _():
        past = bias_ref.shape[1] - LANES
        _attend(q_ref, lambda g: knew_ref[g], lambda g: vnew_ref[g], n_g,
                bias_rows(slice(past, past + LANES)), m_ref, l_ref, acc_ref)
        o = acc_ref[...] / l_ref[...]
        for g in range(n_g):
            for j in range(rg // 8):
                lo = j * LANES + g * HEAD_DIM
                o_ref[:, lo:lo + HEAD_DIM] = o[g * rg + 8 * j:g * rg + 8 * j + 8]


def _dsa_sample(page_table, q, bias, knew, vnew, cache_k, cache_v):
    bd, n_pages = page_table.shape
    n_pg = min(PAGES_PER_STEP, n_pages)
    rows = q.shape[1] * q.shape[2]
    return pl.pallas_call(
        functools.partial(_dsa_sample_kernel, n_pg=n_pg),
        grid_spec=pltpu.PrefetchScalarGridSpec(
            num_scalar_prefetch=1,
            grid=(bd, n_pages // n_pg),
            in_specs=[_per_seq(q.shape), _per_seq(bias.shape), _per_seq(knew.shape), _per_seq(vnew.shape)]
                     + _page_specs(cache_k.shape, n_pg) + _page_specs(cache_v.shape, n_pg),
            out_specs=pl.BlockSpec((None, 8, 512), lambda b, st, pt: (b, 0, 0)),
            scratch_shapes=[pltpu.VMEM((rows, 1), F32), pltpu.VMEM((rows, 1), F32),
                            pltpu.VMEM((rows, HEAD_DIM), F32)],
        ),
        out_shape=jax.ShapeDtypeStruct((bd, 8, 512), F32),
        compiler_params=_cparams("parallel", "arbitrary"),
        name="dsa_sample",
    )(page_table, q, bias, knew, vnew, *([cache_k] * n_pg), *([cache_v] * n_pg))


def _diff_sample_kernel(pt_ref, lamv_ref, gsub_ref, q_ref, knew_ref, vnew_ref, *rest, n_pg, lam_init):
    kpages, vpages = rest[:n_pg], rest[n_pg:2 * n_pg]
    o_ref, m_ref, l_ref, acc_ref = rest[2 * n_pg:]
    st = pl.program_id(1)

    @pl.when(st == 0)
    def _():
        _softmax_init(m_ref, l_ref, acc_ref)

    n_k = q_ref.shape[0]
    _attend(q_ref, lambda i: _page_rows(kpages, i // 2, i % 2), lambda h: _page_rows(vpages, h), C_HEADS,
            0.0, m_ref, l_ref, acc_ref)

    @pl.when(st == pl.num_programs(1) - 1)
    def _():
        bias = jnp.where(_new_key_mask(8 * n_k), 0.0, NEG_INF)
        _attend(q_ref, lambda i: knew_ref[i], lambda h: vnew_ref[h], C_HEADS, bias, m_ref, l_ref, acc_ref)
        lam = _diff_lambda(lamv_ref, lam_init)
        o = acc_ref[...] / l_ref[...]
        for h in range(C_HEADS):
            oh = o[16 * h:16 * h + 8] - lam * o[16 * h + 8:16 * h + 16]
            o_ref[:, h * LANES:(h + 1) * LANES] = _rms(oh, gsub_ref[...]) * (1.0 - lam_init)


def _diff_sample(page_table, lamv, gsub, q, knew, vnew, cache_k, cache_v, lam_init):
    bd, n_pages = page_table.shape
    n_pg = min(PAGES_PER_STEP, n_pages)
    rows = q.shape[1] * q.shape[2]
    cst = lambda arr: pl.BlockSpec(arr.shape, lambda b, st, pt: (0, 0))
    return pl.pallas_call(
        functools.partial(_diff_sample_kernel, n_pg=n_pg, lam_init=lam_init),
        grid_spec=pltpu.PrefetchScalarGridSpec(
            num_scalar_prefetch=1,
            grid=(bd, n_pages // n_pg),
            in_specs=[cst(lamv), cst(gsub), _per_seq(q.shape), _per_seq(knew.shape), _per_seq(vnew.shape)]
                     + _page_specs(cache_k.shape, n_pg) + _page_specs(cache_v.shape, n_pg),
            out_specs=pl.BlockSpec((None, 8, 512), lambda b, st, pt: (b, 0, 0)),
            scratch_shapes=[pltpu.VMEM((rows, 1), F32), pltpu.VMEM((rows, 1), F32),
                            pltpu.VMEM((rows, 2 * HEAD_DIM), F32)],
        ),
        out_shape=jax.ShapeDtypeStruct((bd, 8, 512), F32),
        compiler_params=_cparams("parallel", "arbitrary"),
        name="diff_sample",
    )(page_table, lamv, gsub, q, knew, vnew, *([cache_k] * n_pg), *([cache_v] * n_pg))


def _fox_sample_kernel(pt_ref, q_ref, lft_ref, knew_ref, vnew_ref, *rest, n_pg):
    kpages, vpages, fpages = rest[:n_pg], rest[n_pg:2 * n_pg], rest[2 * n_pg:3 * n_pg]
    o_ref, m_ref, l_ref, acc_ref, carry_ref = rest[3 * n_pg:]
    st = pl.program_id(1)

    @pl.when(st == 0)
    def _():
        _softmax_init(m_ref, l_ref, acc_ref)
        carry_ref[...] = jnp.zeros(carry_ref.shape, F32)

    n_h = q_ref.shape[0]
    cn64 = _rows8(_lane_prefix(lft_ref[...], _tri(incl_upper=True)))
    mine = lax.broadcasted_iota(jnp.int32, cn64.shape, 1) == lax.broadcasted_iota(jnp.int32, cn64.shape, 0) % 8
    cn_col = jnp.sum(jnp.where(mine, cn64, 0.0), axis=1, keepdims=True)
    tri_after = _tri(strict_lower=True)

    carry = carry_ref[...]
    sfx = []
    for pg in range(n_pg):
        x = fpages[pg][...]
        sfx.append(_lane_prefix(x, tri_after) + carry)
        carry = carry + jnp.sum(x, axis=1, keepdims=True)
    carry_ref[...] = carry
    sfx = jnp.concatenate(sfx, axis=1)
    _attend(q_ref, lambda h: _page_rows(kpages, h), lambda h: _page_rows(vpages, h), n_h,
            cn_col + _rows8(sfx), m_ref, l_ref, acc_ref)

    @pl.when(st == pl.num_programs(1) - 1)
    def _():
        bias = jnp.where(_new_key_mask(8 * n_h), cn_col - cn64, NEG_INF)
        _attend(q_ref, lambda h: knew_ref[h], lambda h: vnew_ref[h], n_h, bias, m_ref, l_ref, acc_ref)
        o = acc_ref[...] / l_ref[...]
        for h in range(n_h):
            o_ref[:, h * HEAD_DIM:(h + 1) * HEAD_DIM] = o[8 * h:8 * h + 8]


def _fox_sample(page_table, q, lf_t, knew, vnew, cache_k, cache_v, cache_lf_t):
    bd, n_pages = page_table.shape
    n_pg = min(PAGES_PER_STEP, n_pages)
    rows = q.shape[1] * q.shape[2]
    last = n_pages - 1
    return pl.pallas_call(
        functools.partial(_fox_sample_kernel, n_pg=n_pg),
        grid_spec=pltpu.PrefetchScalarGridSpec(
            num_scalar_prefetch=1,
            grid=(bd, n_pages // n_pg),
            in_specs=[_per_seq(q.shape), _per_seq(lf_t.shape), _per_seq(knew.shape), _per_seq(vnew.shape)]
                     + _page_specs(cache_k.shape, n_pg, last) + _page_specs(cache_v.shape, n_pg, last)
                     + _page_specs(cache_lf_t.shape, n_pg, last),
            out_specs=pl.BlockSpec((None, 8, 512), lambda b, st, pt: (b, 0, 0)),
            scratch_shapes=[pltpu.VMEM((rows, 1), F32), pltpu.VMEM((rows, 1), F32),
                            pltpu.VMEM((rows, HEAD_DIM), F32), pltpu.VMEM((8, 1), F32)],
        ),
        out_shape=jax.ShapeDtypeStruct((bd, 8, 512), F32),
        compiler_params=_cparams("parallel", "arbitrary"),
        name="fox_sample",
    )(page_table, q, lf_t, knew, vnew, *([cache_k] * n_pg), *([cache_v] * n_pg), *([cache_lf_t] * n_pg))


def _rope_tables(pos):
    rot = HEAD_DIM // 4
    half = rot // 2
    inv = ROPE_THETA ** (-np.arange(half, dtype=np.float64) / half)
    ang = np.asarray(pos, np.float64)[:, None] * inv[None, :]
    lane = np.arange(LANES) % HEAD_DIM
    c = np.ones((len(pos), LANES), np.float64)
    s1 = np.zeros_like(c)
    s2 = np.zeros_like(c)
    first, second = lane < half, (lane >= half) & (lane < rot)
    c[:, first] = np.cos(ang)[:, lane[first]]
    c[:, second] = np.cos(ang)[:, lane[second] - half]
    s1[:, first] = -np.sin(ang)[:, lane[first]]
    s2[:, second] = np.sin(ang)[:, lane[second] - half]
    return tuple(jnp.asarray(a, F32) for a in (c, s1, s2))


def _pack_w_ab(w):
    q, k, v, qi, ki, wi, u, gv = jnp.split(w, np.cumsum(
        (512, 128, 128, 512, IDX_DIM, IDX_HEADS, GM_WIDTH, GM_WIDTH))[:-1].tolist(), axis=1)
    q = q.reshape(D_MODEL, A_HEADS, HEAD_DIM)[:, np.array(A_HEAD_PERM)].reshape(D_MODEL, 512)
    pad = jnp.zeros((D_MODEL, LANES - IDX_DIM - IDX_HEADS), w.dtype)
    return jnp.concatenate([q, k, v, qi, ki, wi, pad, u, gv], axis=1).astype(BF16)


def _pack_w_cd(w):
    pad = jnp.zeros((D_MODEL, CD_COLS - w.shape[1]), w.dtype)
    return jnp.concatenate([w, pad], axis=1).astype(BF16)


def _spatial_blockdiag(w_s, b_s, tm, rows_per_chunk):
    wt = (w_s * jnp.tril(jnp.ones((CHUNK, CHUNK), w_s.dtype)))[:, :rows_per_chunk, :rows_per_chunk]
    n_blk = tm // rows_per_chunk
    eye = jnp.eye(n_blk, dtype=w_s.dtype)
    wsp = jnp.einsum("ab,gts->gatbs", eye, wt).reshape(GM_GROUPS, tm, tm).astype(BF16)
    bias = jnp.repeat(b_s[:, :rows_per_chunk].T, HEAD_DIM, axis=1)
    return wsp, jnp.tile(bias, (n_blk, 1))


def _pad_rows(x, rows):
    return jnp.pad(x, ((0, 0), (0, rows - x.shape[1]), (0, 0)))


def _group_major(x, bd, t_new, n_groups, width, pad_to=None):
    x = x.reshape(bd, t_new, n_groups, width).transpose(0, 2, 1, 3)
    if pad_to is not None:
        x = jnp.pad(x, ((0, 0), (0, 0), (0, pad_to - t_new), (0, 0)))
    return x


def kernel(x_prompt, x_sample, cache_a_k, cache_a_v, cache_a_idx, cache_c_k, cache_c_v, cache_d_k, cache_d_v,
           cache_d_logf, page_table, c_prompt, c_sample, w_mod, b_mod, g_norm, w_in_ab, w_out_ab, gm_norm_v,
           gm_spatial_w, gm_spatial_b, w_in_cd, w_out_cd, fox_bias, lam_q1, lam_k1, lam_q2, lam_k2, g_subln,
           w_gate, w_up, w_down):
    b, s_len, _ = x_prompt.shape
    bd, t_new, _ = x_sample.shape
    n_pages = page_table.shape[1]
    past_len = n_pages * PAGE_SIZE
    n_pool = cache_a_k.shape[1]
    assert t_new == 8 and s_len % CHUNK == 0 and (bd * t_new) % 8 == 0
    mp, ms = b * s_len, bd * t_new

    mod = _modulation(jnp.concatenate([c_prompt, c_sample], axis=0), w_mod, b_mod)
    mod = mod.reshape(mod.shape[0], b + bd, 6, D_MODEL)

    def mods(layer):
        p = [mod[layer, :b, i].reshape(b, 1, D_MODEL) for i in range(6)]
        s = [jnp.repeat(mod[layer, b:, i], t_new, axis=0) for i in range(6)]
        return p, s

    tps_p = lambda tm: s_len // tm
    tps_s = lambda tm: 1
    rope_p = _rope_tables(np.arange(s_len))
    rope_s = _rope_tables(np.tile(past_len + np.arange(t_new), bd))
    gvec = lambda layer, i: g_norm[layer, i].reshape(1, D_MODEL)

    xp = x_prompt.reshape(mp, D_MODEL)
    xs = x_sample.reshape(ms, D_MODEL)
    pt = page_table.astype(jnp.int32)

    li = 0
    mod_p, mod_s = mods(0)
    w_ab = _pack_w_ab(w_in_ab[li])
    gm_g = gm_norm_v[li].reshape(1, GM_WIDTH)
    tm_p, tm_s = min(ROW_TILE, mp), min(ROW_TILE, ms)
    wsp_p, bsp_p = _spatial_blockdiag(gm_spatial_w[li], gm_spatial_b[li], tm_p, CHUNK)
    wsp_s, bsp_s = _spatial_blockdiag(gm_spatial_w[li], gm_spatial_b[li], tm_s, t_new)
    wo = w_out_ab[li]
    wo_a = wo[:512].reshape(A_HEADS, HEAD_DIM, D_MODEL)[np.array(A_HEAD_PERM)].reshape(512, D_MODEL).astype(BF16)
    wo_b = wo[512:].astype(BF16)
    n_sel_p = min(TOPK_MAX, s_len // 4)
    n_sel_s = min(TOPK_MAX, (past_len + t_new) // 4)

    (q_p, k32_p, v32_p, kiw_p, kbf_p, vbf_p, qi_p, kid_p, ob_p) = _inproj_ab(
        xp, gvec(0, 0), mod_p[0], mod_p[1], w_ab, rope_p, gm_g, wsp_p, bsp_p, tps_p(tm_p), False)
    (q_s, k32_s, v32_s, kiw_s, kbf_s, vbf_s, qi_s, kid_s, ob_s, gv_s) = _inproj_ab(
        xs, gvec(0, 0), mod_s[0], mod_s[1], w_ab, rope_s, gm_g, wsp_s, bsp_s, 1, True)

    r3 = lambda a: a.reshape(b, s_len, a.shape[-1])
    wi_p = kiw_p[:, IDX_DIM:IDX_DIM + IDX_HEADS]
    oa_p = _dsa_prompt(r3(qi_p), r3(wi_p), r3(kid_p), r3(q_p), r3(kbf_p), r3(vbf_p), n_sel_p)

    qi_stk = qi_s.reshape(bd, t_new, IDX_HEADS, IDX_DIM).transpose(0, 2, 1, 3).reshape(bd, 64, IDX_DIM)
    wi_col = kiw_s[:, IDX_DIM:IDX_DIM + IDX_HEADS].reshape(bd, t_new, IDX_HEADS).transpose(0, 2, 1).reshape(bd, 64, 1)
    ki_new = _pad_rows(kid_s[:, :IDX_DIM].reshape(bd, t_new, IDX_DIM), PAGE_SIZE)
    sel_bias = _dsa_select(pt, qi_stk, wi_col, ki_new, cache_a_idx[li], n_sel_s)

    q_stk = q_s.reshape(bd, t_new, 4, 2, HEAD_DIM).transpose(0, 3, 2, 1, 4)
    q_stk = q_stk.reshape(bd, A_KV_HEADS, 4 * t_new, HEAD_DIM)
    o_s = _dsa_sample(pt, q_stk, sel_bias,
                      _group_major(kbf_s, bd, t_new, A_KV_HEADS, HEAD_DIM, PAGE_SIZE),
                      _group_major(vbf_s, bd, t_new, A_KV_HEADS, HEAD_DIM, PAGE_SIZE),
                      cache_a_k[li], cache_a_v[li])
    oa_s = o_s.reshape(ms, 512).astype(BF16)

    xp = _outproj(oa_p.reshape(mp, 512), ob_p, wo_a, wo_b, xp, mod_p[2], gvec(0, 1), tps_p)
    xs = _outproj(oa_s, ob_s, wo_a, wo_b, xs, mod_s[2], gvec(0, 1), tps_s)
    wg, wu, wd = w_gate[0].astype(BF16), w_up[0].astype(BF16), w_down[0].astype(BF16)
    xp = _ffn(xp, gvec(0, 2), mod_p[3], mod_p[4], wg, wu, wd, mod_p[5], gvec(0, 3), tps_p)
    xs = _ffn(xs, gvec(0, 2), mod_s[3], mod_s[4], wg, wu, wd, mod_s[5], gvec(0, 3), tps_s)

    a_k_p = k32_p.reshape(1, b, s_len, A_KV_HEADS, HEAD_DIM)
    a_v_p = v32_p.reshape(1, b, s_len, A_KV_HEADS, HEAD_DIM)
    a_idx_p = kiw_p[:, :IDX_DIM].reshape(1, b, s_len, IDX_DIM)
    a_k_s = k32_s.reshape(1, bd, t_new, A_KV_HEADS, HEAD_DIM)
    a_v_s = v32_s.reshape(1, bd, t_new, A_KV_HEADS, HEAD_DIM)
    a_idx_s = kiw_s[:, :IDX_DIM].reshape(1, bd, t_new, IDX_DIM)
    gm_v_s = gv_s.reshape(1, bd, t_new, GM_WIDTH)

    layer = 1
    mod_p, mod_s = mods(layer)
    lam_init = 0.8 - 0.6 * math.exp(-0.3 * layer)
    w_cd = _pack_w_cd(w_in_cd[li])
    fox_b = jnp.pad(fox_bias[li], (0, LANES - D_HEADS)).reshape(1, LANES)
    lamv = jnp.stack([lam_q1[li], lam_k1[li], lam_q2[li], lam_k2[li]])
    gsub = g_subln[li].reshape(1, 2 * HEAD_DIM)
    wo = w_out_cd[li]
    wo_c, wo_d = wo[:512].astype(BF16), wo[512:].astype(BF16)

    (cq_p, ck32_p, cv32_p, ckbf_p, cvbf_p, dq_p, dk32_p, dv32_p, dkbf_p, dvbf_p, lf_p) = _inproj_cd(
        xp, gvec(1, 0), mod_p[0], mod_p[1], w_cd, rope_p, fox_b, tps_p(tm_p))
    (cq_s, ck32_s, cv32_s, ckbf_s, cvbf_s, dq_s, dk32_s, dv32_s, dkbf_s, dvbf_s, lf_s) = _inproj_cd(
        xs, gvec(1, 0), mod_s[0], mod_s[1], w_cd, rope_s, fox_b, 1)

    oc_p = _diff_prompt(lamv, gsub, r3(cq_p), r3(ckbf_p), r3(cvbf_p), lam_init)
    logf_p = lf_p[:, :D_HEADS].reshape(b, s_len, D_HEADS)
    cum_t = _cumsum_lanes(logf_p.transpose(0, 2, 1))
    od_p = _fox_prompt(r3(dq_p), r3(dkbf_p), r3(dvbf_p), cum_t.transpose(0, 2, 1), cum_t)

    oc_s = _diff_sample(pt, lamv, gsub, _group_major(cq_s, bd, t_new, 2 * C_HEADS, HEAD_DIM),
                        _group_major(ckbf_s, bd, t_new, 2 * C_HEADS, HEAD_DIM, PAGE_SIZE),
                        _group_major(cvbf_s, bd, t_new, C_HEADS, 2 * HEAD_DIM, PAGE_SIZE),
                        cache_c_k[li], cache_c_v[li], lam_init)

    logf_s = lf_s[:, :D_HEADS].reshape(bd, t_new, D_HEADS)
    lf_t = jnp.pad(logf_s.transpose(0, 2, 1), ((0, 0), (0, 0), (0, LANES - t_new)))
    od_s = _fox_sample(pt, _group_major(dq_s, bd, t_new, D_HEADS, HEAD_DIM), lf_t,
                       _group_major(dkbf_s, bd, t_new, D_HEADS, HEAD_DIM, PAGE_SIZE),
                       _group_major(dvbf_s, bd, t_new, D_HEADS, HEAD_DIM, PAGE_SIZE),
                       cache_d_k[li], cache_d_v[li], cache_d_logf[li].transpose(0, 2, 1))

    xp = _outproj(oc_p.reshape(mp, 512), od_p.reshape(mp, 512), wo_c, wo_d, xp, mod_p[2], gvec(1, 1), tps_p)
    xs = _outproj(oc_s.reshape(ms, 512).astype(BF16), od_s.reshape(ms, 512).astype(BF16), wo_c, wo_d, xs,
                  mod_s[2], gvec(1, 1), tps_s)
    wg, wu, wd = w_gate[1].astype(BF16), w_up[1].astype(BF16), w_down[1].astype(BF16)
    xp = _ffn(xp, gvec(1, 2), mod_p[3], mod_p[4], wg, wu, wd, mod_p[5], gvec(1, 3), tps_p)
    xs = _ffn(xs, gvec(1, 2), mod_s[3], mod_s[4], wg, wu, wd, mod_s[5], gvec(1, 3), tps_s)

    def shp(a, g, *tail):
        return a.reshape((1,) + g + tail)

    gp, gs = (b, s_len), (bd, t_new)
    return (xp.reshape(b, s_len, D_MODEL), xs.reshape(bd, t_new, D_MODEL),
            a_k_p, a_v_p, a_idx_p,
            shp(ck32_p, gp, C_HEADS, 2, HEAD_DIM), shp(cv32_p, gp, C_HEADS, 2 * HEAD_DIM),
            shp(dk32_p, gp, D_HEADS, HEAD_DIM), shp(dv32_p, gp, D_HEADS, HEAD_DIM), shp(logf_p, gp, D_HEADS),
            a_k_s, a_v_s, a_idx_s,
            shp(ck32_s, gs, C_HEADS, 2, HEAD_DIM), shp(cv32_s, gs, C_HEADS, 2 * HEAD_DIM),
            shp(dk32_s, gs, D_HEADS, HEAD_DIM), shp(dv32_s, gs, D_HEADS, HEAD_DIM), shp(logf_s, gs, D_HEADS),
            gm_v_s)
```

```python
import functools
import math

import numpy as np
import jax
import jax.numpy as jnp
from jax import lax
from jax.experimental import pallas as pl
from jax.experimental.pallas import tpu as pltpu

F32 = jnp.float32
BF16 = jnp.bfloat16
NEG_INF = float("-inf")

D_MODEL = 1024
HEAD_DIM = 64
ROPE_THETA = 500000.0
A_HEADS = 8
A_KV_HEADS = 2
IDX_HEADS = 8
IDX_DIM = 64
TOPK_MAX = 256
GM_GROUPS = 8
GM_WIDTH = GM_GROUPS * HEAD_DIM
CHUNK = 128
C_HEADS = 4
D_HEADS = 8
PAGE_SIZE = 128
EPS = 1e-6
D_FF = -(-(8 * D_MODEL) // (3 * 256)) * 256

LANES = 128
VMEM_LIMIT_BYTES = 56 * 1024 * 1024

ROW_TILE = 256
FFN_ROW_TILE = 256
Q_TILE = 128
CAUSAL_SEGMENTS = 8
PAGES_PER_STEP = 8
MOD_COL_TILE = 1536
BISECT_STEPS = 20

AB_Q, AB_K, AB_V, AB_QI, AB_KIW, AB_U, AB_GV, AB_COLS = 0, 512, 640, 768, 1280, 1408, 1920, 2432
CD_CQ, CD_CK, CD_CV, CD_DQ, CD_DK, CD_DV, CD_DF, CD_COLS = 0, 512, 1024, 1536, 2048, 2560, 3072, 3200
A_HEAD_PERM = (0, 4, 1, 5, 2, 6, 3, 7)


def _cparams(*sem):
    return pltpu.CompilerParams(dimension_semantics=sem, vmem_limit_bytes=VMEM_LIMIT_BYTES)


def _nt(a, b):
    return lax.dot_general(a, b, (((1,), (1,)), ((), ())), preferred_element_type=F32)


def _mm(a, b):
    return jnp.dot(a, b, preferred_element_type=F32)


def _rms(x, g):
    return x * lax.rsqrt(jnp.mean(x * x, axis=-1, keepdims=True) + EPS) * g


def _silu(x):
    return x / (1.0 + jnp.exp(-x))


def _gelu_tanh(x):
    return 0.5 * x * (1.0 + jnp.tanh(math.sqrt(2.0 / math.pi) * (x + 0.044715 * (x * x * x))))


def _half_masks(dtype):
    lane = lax.broadcasted_iota(jnp.int32, (1, LANES), 1)
    lo = (lane < HEAD_DIM).astype(dtype)
    return lo, (1 - lo).astype(dtype)


def _rope(z, c, s1, s2):
    outs = []
    for j in range(z.shape[1] // LANES):
        x = z[:, j * LANES:(j + 1) * LANES]
        outs.append(x * c + pltpu.roll(x, LANES - 8, 1) * s1 + pltpu.roll(x, 8, 1) * s2)
    return outs[0] if len(outs) == 1 else jnp.concatenate(outs, axis=1)


def _split3(x):
    hi = x.astype(BF16)
    r1 = x - hi.astype(F32)
    mid = r1.astype(BF16)
    lo = (r1 - mid.astype(F32)).astype(BF16)
    return hi, mid, lo


def _lane_prefix(x, tri):
    hi, mid, lo = _split3(x)
    r = _mm(jnp.concatenate([hi, mid, lo], axis=0), tri)
    return r[0:8] + r[8:16] + r[16:24]


def _tri(incl_upper=None, strict_lower=None):
    r = lax.broadcasted_iota(jnp.int32, (LANES, LANES), 0)
    c = lax.broadcasted_iota(jnp.int32, (LANES, LANES), 1)
    if incl_upper:
        return (r <= c).astype(BF16)
    return (r > c).astype(BF16)


def _count_ge(sm, x):
    return jnp.sum(jnp.where(sm >= x, 1.0, 0.0), axis=1, keepdims=True)


def _topk_bias(sm_ref, bias_ref, n_sel):
    kf = float(n_sel)
    sm = sm_ref[...]
    allowed = sm > NEG_INF
    mx = jnp.max(sm, axis=1, keepdims=True)
    mn = jnp.min(jnp.where(allowed, sm, jnp.inf), axis=1, keepdims=True)
    n_allowed = jnp.sum(jnp.where(allowed, 1.0, 0.0), axis=1, keepdims=True)
    few = n_allowed <= kf
    top_tie = _count_ge(sm, mx) >= kf

    def bisect(_, c):
        lo, hi = c
        mid = lo + (hi - lo) * 0.5
        ge = _count_ge(sm_ref[...], mid) >= kf
        return jnp.where(ge, mid, lo), jnp.where(ge, hi, mid)

    _, hi = lax.fori_loop(0, BISECT_STEPS, bisect, (mn, mx))

    done0 = jnp.where(few, 1.0, jnp.where(top_tie, 1.0, 0.0))
    thr0 = jnp.where(few, NEG_INF, mx)

    def not_done(c):
        return c[3] > 0.0

    def step(c):
        hi, thr, done, _ = c
        s = sm_ref[...]
        m = jnp.max(jnp.where(s < hi, s, NEG_INF), axis=1, keepdims=True)
        found = _count_ge(s, m) >= kf
        is_done = done > 0.5
        thr = jnp.where(is_done, thr, jnp.where(found, m, thr))
        hi = jnp.where(is_done, hi, jnp.where(found, hi, m))
        done = jnp.where(found, 1.0, done)
        return hi, thr, done, jnp.sum(1.0 - done)

    _, thr, _, _ = lax.while_loop(not_done, step, (hi, thr0, done0, jnp.sum(1.0 - done0)))

    is_gt = sm > thr
    is_eq = jnp.where(sm == thr, jnp.where(allowed, 1.0, 0.0), 0.0)
    need = kf - jnp.sum(jnp.where(is_gt, 1.0, 0.0), axis=1, keepdims=True)
    n_eq = jnp.sum(is_eq, axis=1, keepdims=True)
    bias_ref[...] = jnp.where(sm >= thr, jnp.where(allowed, 0.0, NEG_INF), NEG_INF)

    @pl.when(jnp.max(n_eq - need) > 0.0)
    def _():
        tri = _tri(incl_upper=True)

        def blk(j, run):
            cols = pl.ds(pl.multiple_of(j * LANES, LANES), LANES)
            s = sm_ref[:, cols]
            e = jnp.where(s == thr, jnp.where(s > NEG_INF, 1.0, 0.0), 0.0)
            rank = run + _mm(e.astype(BF16), tri)
            keep = jnp.where(e > 0.5, jnp.where(rank <= need, 1.0, 0.0), 0.0)
            bias_ref[:, cols] = jnp.where(s > thr, 0.0, jnp.where(keep > 0.5, 0.0, NEG_INF))
            return run + jnp.sum(e, axis=1, keepdims=True)

        lax.fori_loop(0, sm_ref.shape[1] // LANES, blk, jnp.zeros_like(need))


def _mod_kernel(c_ref, w_ref, b_ref, o_ref):
    a = _silu(c_ref[...]).astype(BF16)
    o_ref[...] = _mm(a, w_ref[...].astype(BF16)) + b_ref[...]


def _modulation(c_all, w_mod, b_mod):
    n_layers, _, n_out = w_mod.shape
    rows = c_all.shape[0]
    tn = MOD_COL_TILE
    return pl.pallas_call(
        _mod_kernel,
        grid=(n_layers, n_out // tn),
        in_specs=[
            pl.BlockSpec((rows, D_MODEL), lambda l, n: (0, 0)),
            pl.BlockSpec((None, D_MODEL, tn), lambda l, n: (l, 0, n)),
            pl.BlockSpec((None, 1, tn), lambda l, n: (l, 0, n)),
        ],
        out_specs=pl.BlockSpec((None, rows, tn), lambda l, n: (l, 0, n)),
        out_shape=jax.ShapeDtypeStruct((n_layers, rows, n_out), F32),
        compiler_params=_cparams("arbitrary", "arbitrary"),
        name="modulation",
    )(c_all, w_mod, b_mod.reshape(n_layers, 1, n_out))


def _rowvec_spec(vec, tm, tiles_per_seq):
    if vec.ndim == 3:
        return pl.BlockSpec((None, 1, vec.shape[-1]), lambda i: (i // tiles_per_seq, 0, 0))
    return pl.BlockSpec((tm, vec.shape[-1]), lambda i: (i, 0))


def _const_spec(arr):
    nd = arr.ndim
    return pl.BlockSpec(arr.shape, lambda i: (0,) * nd)


def _inproj_ab_kernel(x_ref, g_ref, sh_ref, sc_ref, w_ref, rc_ref, rs1_ref, rs2_ref, gmg_ref, wsp_ref, bsp_ref,
                      q_ref, k32_ref, v32_ref, kiw_ref, kbf_ref, vbf_ref, qi_ref, kid_ref, ob_ref, *gv_ref):
    h = (_rms(x_ref[...], g_ref[...]) * (1.0 + sc_ref[...]) + sh_ref[...]).astype(BF16)
    c, s1, s2 = rc_ref[...], rs1_ref[...], rs2_ref[...]

    def seg(lo, hi):
        return _mm(h, w_ref[:, lo:hi])

    q_ref[...] = (_rope(seg(AB_Q, AB_K), c, s1, s2) * HEAD_DIM ** -0.5).astype(BF16)
    k = _rope(seg(AB_K, AB_V), c, s1, s2)
    k32_ref[...] = k
    kbf_ref[...] = k.astype(BF16)
    v = seg(AB_V, AB_QI)
    v32_ref[...] = v
    vbf_ref[...] = v.astype(BF16)
    qi_ref[...] = _rope(seg(AB_QI, AB_KIW), c, s1, s2).astype(BF16)
    kz = seg(AB_KIW, AB_U)
    kr = _rope(kz, c, s1, s2)
    lane = lax.broadcasted_iota(jnp.int32, kz.shape, 1)
    kiw_ref[...] = jnp.where(lane < IDX_DIM, kr, kz)
    kid_ref[...] = jnp.where(lane < IDX_DIM, kr, pltpu.roll(kr, IDX_DIM, 1)).astype(BF16)

    u = _gelu_tanh(seg(AB_U, AB_GV))
    gv = _gelu_tanh(seg(AB_GV, AB_COLS))
    gc = gv - jnp.mean(gv, axis=-1, keepdims=True)
    gvn = gc * lax.rsqrt(jnp.mean(gc * gc, axis=-1, keepdims=True) + EPS) * gmg_ref[...]
    if gv_ref:
        gv_ref[0][...] = gvn
    gvb = gvn.astype(BF16)
    lane = lax.broadcasted_iota(jnp.int32, (1, LANES), 1)
    mix = []
    for j in range(GM_GROUPS // 2):
        slab = gvb[:, j * LANES:(j + 1) * LANES]
        mix.append(jnp.where(lane < HEAD_DIM, _mm(wsp_ref[2 * j], slab), _mm(wsp_ref[2 * j + 1], slab)))
    ob_ref[...] = (u * (jnp.concatenate(mix, axis=1) + bsp_ref[...])).astype(BF16)


def _inproj_ab(x, g, shift, scale, w, rope_tabs, gm_g, wsp, bsp, tiles_per_seq, want_gv):
    m = x.shape[0]
    tm = min(ROW_TILE, m)
    n_tab_tiles = rope_tabs[0].shape[0] // tm
    row = lambda width: pl.BlockSpec((tm, width), lambda i: (i, 0))
    tab = pl.BlockSpec((tm, LANES), lambda i: (i % n_tab_tiles, 0))
    sds = lambda width, dt: jax.ShapeDtypeStruct((m, width), dt)
    out_shape = [sds(512, BF16), sds(128, F32), sds(128, F32), sds(128, F32), sds(128, BF16), sds(128, BF16),
                 sds(512, BF16), sds(128, BF16), sds(512, BF16)]
    out_specs = [row(512), row(128), row(128), row(128), row(128), row(128), row(512), row(128), row(512)]
    if want_gv:
        out_shape.append(sds(512, F32))
        out_specs.append(row(512))
    return pl.pallas_call(
        _inproj_ab_kernel,
        grid=(m // tm,),
        in_specs=[row(D_MODEL), _const_spec(g), _rowvec_spec(shift, tm, tiles_per_seq),
                  _rowvec_spec(scale, tm, tiles_per_seq), _const_spec(w), tab, tab, tab,
                  _const_spec(gm_g), _const_spec(wsp), _const_spec(bsp)],
        out_specs=out_specs,
        out_shape=out_shape,
        compiler_params=_cparams("parallel"),
        name="inproj_ab",
    )(x, g, shift, scale, w, *rope_tabs, gm_g, wsp, bsp)


def _inproj_cd_kernel(x_ref, g_ref, sh_ref, sc_ref, w_ref, rc_ref, rs1_ref, rs2_ref, fb_ref,
                      cq_ref, ck32_ref, cv32_ref, ckbf_ref, cvbf_ref,
                      dq_ref, dk32_ref, dv32_ref, dkbf_ref, dvbf_ref, lf_ref):
    h = (_rms(x_ref[...], g_ref[...]) * (1.0 + sc_ref[...]) + sh_ref[...]).astype(BF16)
    c, s1, s2 = rc_ref[...], rs1_ref[...], rs2_ref[...]

    def seg(lo, hi):
        return _mm(h, w_ref[:, lo:hi])

    cq_ref[...] = (_rope(seg(CD_CQ, CD_CK), c, s1, s2) * HEAD_DIM ** -0.5).astype(BF16)
    ck = _rope(seg(CD_CK, CD_CV), c, s1, s2)
    ck32_ref[...] = ck
    ckbf_ref[...] = ck.astype(BF16)
    cv = seg(CD_CV, CD_DQ)
    cv32_ref[...] = cv
    cvbf_ref[...] = cv.astype(BF16)
    dq_ref[...] = (seg(CD_DQ, CD_DK) * HEAD_DIM ** -0.5).astype(BF16)
    dk = seg(CD_DK, CD_DV)
    dk32_ref[...] = dk
    dkbf_ref[...] = dk.astype(BF16)
    dv = seg(CD_DV, CD_DF)
    dv32_ref[...] = dv
    dvbf_ref[...] = dv.astype(BF16)
    f = seg(CD_DF, CD_COLS) + fb_ref[...]
    lf_ref[...] = jnp.minimum(f, 0.0) - jnp.log(1.0 + jnp.exp(-jnp.abs(f)))


def _inproj_cd(x, g, shift, scale, w, rope_tabs, fox_b, tiles_per_seq):
    m = x.shape[0]
    tm = min(ROW_TILE, m)
    n_tab_tiles = rope_tabs[0].shape[0] // tm
    row = lambda width: pl.BlockSpec((tm, width), lambda i: (i, 0))
    tab = pl.BlockSpec((tm, LANES), lambda i: (i % n_tab_tiles, 0))
    sds = lambda width, dt: jax.ShapeDtypeStruct((m, width), dt)
    per = [sds(512, BF16), sds(512, F32), sds(512, F32), sds(512, BF16), sds(512, BF16)]
    return pl.pallas_call(
        _inproj_cd_kernel,
        grid=(m // tm,),
        in_specs=[row(D_MODEL), _const_spec(g), _rowvec_spec(shift, tm, tiles_per_seq),
                  _rowvec_spec(scale, tm, tiles_per_seq), _const_spec(w), tab, tab, tab, _const_spec(fox_b)],
        out_specs=[row(512)] * 10 + [row(128)],
        out_shape=per + per + [sds(128, F32)],
        compiler_params=_cparams("parallel"),
        name="inproj_cd",
    )(x, g, shift, scale, w, *rope_tabs, fox_b)


def _outproj_kernel(a_ref, b_ref, wa_ref, wb_ref, x_ref, gate_ref, g_ref, o_ref):
    op = _mm(a_ref[...], wa_ref[...]) + _mm(b_ref[...], wb_ref[...])
    o_ref[...] = x_ref[...] + gate_ref[...] * _rms(op, g_ref[...])


def _outproj(a, b, wa, wb, x, gate, g, tiles_per_seq_of):
    m = x.shape[0]
    tm = min(FFN_ROW_TILE, m)
    row = lambda width: pl.BlockSpec((tm, width), lambda i: (i, 0))
    return pl.pallas_call(
        _outproj_kernel,
        grid=(m // tm,),
        in_specs=[row(a.shape[1]), row(b.shape[1]), _const_spec(wa), _const_spec(wb), row(D_MODEL),
                  _rowvec_spec(gate, tm, tiles_per_seq_of(tm)), _const_spec(g)],
        out_specs=row(D_MODEL),
        out_shape=jax.ShapeDtypeStruct((m, D_MODEL), F32),
        compiler_params=_cparams("parallel"),
        name="outproj",
    )(a, b, wa, wb, x, gate, g)


def _ffn_kernel(x_ref, g2_ref, sh_ref, sc_ref, wg_ref, wu_ref, wd_ref, gate_ref, g3_ref, o_ref):
    x = x_ref[...]
    h = (_rms(x, g2_ref[...]) * (1.0 + sc_ref[...]) + sh_ref[...]).astype(BF16)
    act = (_silu(_mm(h, wg_ref[...])) * _mm(h, wu_ref[...])).astype(BF16)
    o_ref[...] = x + gate_ref[...] * _rms(_mm(act, wd_ref[...]), g3_ref[...])


def _ffn(x, g2, shift, scale, wg, wu, wd, gate, g3, tiles_per_seq_of):
    m = x.shape[0]
    tm = min(FFN_ROW_TILE, m)
    tps = tiles_per_seq_of(tm)
    row = pl.BlockSpec((tm, D_MODEL), lambda i: (i, 0))
    return pl.pallas_call(
        _ffn_kernel,
        grid=(m // tm,),
        in_specs=[row, _const_spec(g2), _rowvec_spec(shift, tm, tps), _rowvec_spec(scale, tm, tps),
                  _const_spec(wg), _const_spec(wu), _const_spec(wd), _rowvec_spec(gate, tm, tps), _const_spec(g3)],
        out_specs=row,
        out_shape=jax.ShapeDtypeStruct((m, D_MODEL), F32),
        compiler_params=_cparams("parallel"),
        name="ffn",
    )(x, g2, shift, scale, wg, wu, wd, gate, g3)


def _dsa_prompt_kernel(qi_ref, wi_ref, kid_ref, q_ref, k_ref, v_ref, o_ref, sm_ref, bias_ref, *, n_sel, q_blk0):
    tq, s_len = sm_ref.shape
    i = q_blk0 + pl.program_id(1)
    m_lo, m_hi = _half_masks(BF16)
    kid = kid_ref[...]
    score = jnp.zeros((tq, s_len), F32)
    for j in range(IDX_HEADS // 2):
        slab = qi_ref[:, j * LANES:(j + 1) * LANES]
        s = _nt(jnp.concatenate([slab * m_lo, slab * m_hi], axis=0), kid)
        score = score + jnp.maximum(s[:tq], 0.0) * wi_ref[:, 2 * j:2 * j + 1]
        score = score + jnp.maximum(s[tq:], 0.0) * wi_ref[:, 2 * j + 1:2 * j + 2]
    score = score * (IDX_HEADS * IDX_DIM) ** -0.5
    row = i * tq + lax.broadcasted_iota(jnp.int32, (tq, s_len), 0)
    col = lax.broadcasted_iota(jnp.int32, (tq, s_len), 1)
    sm_ref[...] = jnp.where(col <= row, score, NEG_INF)
    _topk_bias(sm_ref, bias_ref, n_sel)

    k = k_ref[...]
    v = v_ref[...]
    n_slab = A_HEADS // 2
    outs = []
    for mask in (m_lo, m_hi):
        stk = jnp.concatenate([q_ref[:, j * LANES:(j + 1) * LANES] * mask for j in range(n_slab)], axis=0)
        lg = _nt(stk, k)
        per = []
        for j in range(n_slab):
            l = lg[j * tq:(j + 1) * tq] + bias_ref[...]
            p = jnp.exp(l - jnp.max(l, axis=1, keepdims=True))
            per.append(_mm(p.astype(BF16), v) / jnp.sum(p, axis=1, keepdims=True))
        outs.append(per)
    lane = lax.broadcasted_iota(jnp.int32, (1, LANES), 1)
    for j in range(n_slab):
        o_ref[:, j * LANES:(j + 1) * LANES] = jnp.where(lane < HEAD_DIM, outs[0][j], outs[1][j]).astype(BF16)


def _causal_segments(s_len, tq):
    n_seg = min(CAUSAL_SEGMENTS, s_len // tq)
    per = s_len // tq // n_seg
    return [(seg * per, per, (seg + 1) * per * tq) for seg in range(n_seg)]


def _dsa_prompt(qi, wi, kid, q, k, v, n_sel):
    b, s_len, _ = q.shape
    tq = min(Q_TILE, s_len)
    outs = []
    for q_blk0, n_blk, kext in _causal_segments(s_len, tq):
        qblk = lambda width: pl.BlockSpec((None, tq, width), lambda bb, i: (bb, q_blk0 + i, 0))
        keys = lambda width: pl.BlockSpec((None, kext, width), lambda bb, i: (bb, 0, 0))
        outs.append(pl.pallas_call(
            functools.partial(_dsa_prompt_kernel, n_sel=n_sel, q_blk0=q_blk0),
            grid=(b, n_blk),
            in_specs=[qblk(512), qblk(IDX_HEADS), keys(128), qblk(512), keys(128), keys(128)],
            out_specs=pl.BlockSpec((None, tq, 512), lambda bb, i: (bb, i, 0)),
            out_shape=jax.ShapeDtypeStruct((b, n_blk * tq, 512), BF16),
            scratch_shapes=[pltpu.VMEM((tq, kext), F32), pltpu.VMEM((tq, kext), F32)],
            compiler_params=_cparams("parallel", "arbitrary"),
            name="dsa_prompt",
        )(qi, wi, kid, q, k, v))
    return jnp.concatenate(outs, axis=1)


def _diff_lambda(lamv_ref, lam_init):
    lv = lamv_ref[...]
    d1 = jnp.sum(lv[0:1] * lv[1:2], axis=1, keepdims=True)
    d2 = jnp.sum(lv[2:3] * lv[3:4], axis=1, keepdims=True)
    return jnp.exp(d1) - jnp.exp(d2) + lam_init


def _diff_prompt_kernel(lamv_ref, gsub_ref, q_ref, k_ref, v_ref, o_ref, *, lam_init, q_blk0):
    tq = q_ref.shape[0]
    s_len = k_ref.shape[0]
    i = q_blk0 + pl.program_id(1)
    lam = _diff_lambda(lamv_ref, lam_init)
    m_lo, m_hi = _half_masks(BF16)
    row = i * tq + lax.broadcasted_iota(jnp.int32, (tq, s_len), 0)
    col = lax.broadcasted_iota(jnp.int32, (tq, s_len), 1)
    causal = col <= row
    for h in range(C_HEADS):
        cols = slice(h * LANES, (h + 1) * LANES)
        slab = q_ref[:, cols]
        lg = _nt(jnp.concatenate([slab * m_lo, slab * m_hi], axis=0), k_ref[:, cols])
        l0 = jnp.where(causal, lg[:tq], NEG_INF)
        l1 = jnp.where(causal, lg[tq:], NEG_INF)
        p0 = jnp.exp(l0 - jnp.max(l0, axis=1, keepdims=True))
        p1 = jnp.exp(l1 - jnp.max(l1, axis=1, keepdims=True))
        w = p0 * (1.0 / jnp.sum(p0, axis=1, keepdims=True)) - p1 * (lam / jnp.sum(p1, axis=1, keepdims=True))
        o = _mm(w.astype(BF16), v_ref[:, cols])
        o_ref[:, cols] = (_rms(o, gsub_ref[...]) * (1.0 - lam_init)).astype(BF16)


def _diff_prompt(lamv, gsub, q, k, v, lam_init):
    b, s_len, _ = q.shape
    tq = min(Q_TILE, s_len)
    cst = lambda arr: pl.BlockSpec(arr.shape, lambda bb, i: (0, 0))
    outs = []
    for q_blk0, n_blk, kext in _causal_segments(s_len, tq):
        qblk = pl.BlockSpec((None, tq, 512), lambda bb, i: (bb, q_blk0 + i, 0))
        keys = pl.BlockSpec((None, kext, 512), lambda bb, i: (bb, 0, 0))
        outs.append(pl.pallas_call(
            functools.partial(_diff_prompt_kernel, lam_init=lam_init, q_blk0=q_blk0),
            grid=(b, n_blk),
            in_specs=[cst(lamv), cst(gsub), qblk, keys, keys],
            out_specs=pl.BlockSpec((None, tq, 512), lambda bb, i: (bb, i, 0)),
            out_shape=jax.ShapeDtypeStruct((b, n_blk * tq, 512), BF16),
            compiler_params=_cparams("parallel", "arbitrary"),
            name="diff_prompt",
        )(lamv, gsub, q, k, v))
    return jnp.concatenate(outs, axis=1)


def _cumsum_kernel(x_ref, o_ref):
    tri = _tri(incl_upper=True)
    carry = jnp.zeros((8, 1), F32)
    for c in range(x_ref.shape[1] // LANES):
        cols = slice(c * LANES, (c + 1) * LANES)
        out = _lane_prefix(x_ref[:, cols], tri) + carry
        o_ref[:, cols] = out
        carry = out[:, LANES - 1:LANES]


def _cumsum_lanes(x):
    b, h, s_len = x.shape
    blk = pl.BlockSpec((None, h, s_len), lambda bb: (bb, 0, 0))
    return pl.pallas_call(
        _cumsum_kernel,
        grid=(b,),
        in_specs=[blk],
        out_specs=blk,
        out_shape=jax.ShapeDtypeStruct(x.shape, F32),
        compiler_params=_cparams("parallel"),
        name="fox_cumsum",
    )(x)


def _fox_prompt_kernel(q_ref, k_ref, v_ref, cq_ref, ck_ref, o_ref, *, q_blk0):
    tq = q_ref.shape[0]
    s_len = k_ref.shape[0]
    i = q_blk0 + pl.program_id(1)
    m_lo, m_hi = _half_masks(BF16)
    row = i * tq + lax.broadcasted_iota(jnp.int32, (tq, s_len), 0)
    col = lax.broadcasted_iota(jnp.int32, (tq, s_len), 1)
    causal = col <= row
    lane = lax.broadcasted_iota(jnp.int32, (1, LANES), 1)
    for j in range(D_HEADS // 2):
        cols = slice(j * LANES, (j + 1) * LANES)
        slab = q_ref[:, cols]
        lg = _nt(jnp.concatenate([slab * m_lo, slab * m_hi], axis=0), k_ref[:, cols])
        v = v_ref[:, cols]
        pv = []
        for hh in range(2):
            h = 2 * j + hh
            l = lg[hh * tq:(hh + 1) * tq] + cq_ref[:, h:h + 1] - ck_ref[h:h + 1, :]
            l = jnp.where(causal, l, NEG_INF)
            p = jnp.exp(l - jnp.max(l, axis=1, keepdims=True))
            pv.append(_mm(p.astype(BF16), v) / jnp.sum(p, axis=1, keepdims=True))
        o_ref[:, cols] = jnp.where(lane < HEAD_DIM, pv[0], pv[1]).astype(BF16)


def _fox_prompt(q, k, v, cum, cum_t):
    b, s_len, _ = q.shape
    tq = min(Q_TILE, s_len)
    outs = []
    for q_blk0, n_blk, kext in _causal_segments(s_len, tq):
        qblk = lambda width: pl.BlockSpec((None, tq, width), lambda bb, i: (bb, q_blk0 + i, 0))
        keys = pl.BlockSpec((None, kext, 512), lambda bb, i: (bb, 0, 0))
        outs.append(pl.pallas_call(
            functools.partial(_fox_prompt_kernel, q_blk0=q_blk0),
            grid=(b, n_blk),
            in_specs=[qblk(512), keys, keys, qblk(D_HEADS),
                      pl.BlockSpec((None, D_HEADS, kext), lambda bb, i: (bb, 0, 0))],
            out_specs=pl.BlockSpec((None, tq, 512), lambda bb, i: (bb, i, 0)),
            out_shape=jax.ShapeDtypeStruct((b, n_blk * tq, 512), BF16),
            compiler_params=_cparams("parallel", "arbitrary"),
            name="fox_prompt",
        )(q, k, v, cum, cum_t))
    return jnp.concatenate(outs, axis=1)


def _page_specs(block, n_pg, reverse_from=None):
    specs = []
    for pg in range(n_pg):
        if reverse_from is None:
            idx = lambda b, st, pt, pg=pg: (pt[b, st * n_pg + pg],) + (0,) * (len(block) - 1)
        else:
            idx = lambda b, st, pt, pg=pg: (pt[b, reverse_from - (st * n_pg + pg)],) + (0,) * (len(block) - 1)
        specs.append(pl.BlockSpec((None,) + tuple(block[1:]), idx))
    return specs


def _per_seq(shape):
    nd = len(shape)
    return pl.BlockSpec((None,) + tuple(shape[1:]), lambda b, st, pt: (b,) + (0,) * (nd - 1))


def _softmax_init(m_ref, l_ref, acc_ref):
    m_ref[...] = jnp.full(m_ref.shape, NEG_INF, F32)
    l_ref[...] = jnp.zeros(l_ref.shape, F32)
    acc_ref[...] = jnp.zeros(acc_ref.shape, F32)


def _attend(q_ref, keys_t_of, values_of, n_v, values_t, bias, m_ref, l_ref, acc_ref):
    n_k, r, _ = q_ref.shape
    pv = _nt if values_t else _mm
    lg = jnp.concatenate([_mm(q_ref[i], keys_t_of(i)) for i in range(n_k)], axis=0) + bias
    m_old = m_ref[...]
    m_new = jnp.maximum(m_old, jnp.max(lg, axis=1, keepdims=True))
    m_safe = jnp.where(m_new == NEG_INF, 0.0, m_new)
    alpha = jnp.exp(m_old - m_safe)
    p = jnp.exp(lg - m_safe)
    l_ref[...] = alpha * l_ref[...] + jnp.sum(p, axis=1, keepdims=True)
    m_ref[...] = m_new
    p = p.astype(BF16)
    rv = r * (n_k // n_v)
    for j in range(n_v):
        rows = slice(j * rv, (j + 1) * rv)
        acc_ref[rows, :] = alpha[rows] * acc_ref[rows, :] + pv(p[rows], values_of(j))


def _page_rows(pages, *idx):
    take = (slice(None),) + idx + (slice(None),)
    return jnp.concatenate([p[take].astype(BF16) for p in pages], axis=0)


def _page_lanes(pages, *idx):
    return jnp.concatenate([(p[idx] if idx else p[...]).astype(BF16) for p in pages], axis=1)


def _rows8(x):
    return jnp.concatenate([jnp.broadcast_to(x[h:h + 1], (8, x.shape[1])) for h in range(x.shape[0])], axis=0)


def _head_sum(w, n_groups):
    out = w[0:8]
    for h in range(1, n_groups):
        out = out + w[8 * h:8 * h + 8]
    return out


def _new_key_mask(rows):
    t = lax.broadcasted_iota(jnp.int32, (rows, LANES), 0) % 8
    c = lax.broadcasted_iota(jnp.int32, (rows, LANES), 1)
    return c <= t


def _dsa_select_kernel(pt_ref, q_ref, w_ref, knew_ref, *rest, n_pg, n_sel):
    pages = rest[:n_pg]
    bias_ref, sm_ref = rest[n_pg], rest[n_pg + 1]
    st = pl.program_id(1)
    q = q_ref[...]
    wcol = w_ref[...]
    scale = (IDX_HEADS * IDX_DIM) ** -0.5

    def scores(keys_t):
        return _head_sum(jnp.maximum(_mm(q, keys_t), 0.0) * wcol, IDX_HEADS) * scale

    step_keys = n_pg * PAGE_SIZE
    col0 = pl.multiple_of(st * step_keys, step_keys)
    sm_ref[:, pl.ds(col0, step_keys)] = scores(_page_lanes(pages))

    @pl.when(st == pl.num_programs(1) - 1)
    def _():
        past = sm_ref.shape[1] - LANES
        sm_ref[:, past:] = jnp.where(_new_key_mask(8), scores(knew_ref[...]), NEG_INF)
        _topk_bias(sm_ref, bias_ref, n_sel)


def _dsa_select(page_table, q, wcol, knew, cache_idx, n_sel):
    bd, n_pages = page_table.shape
    n_pg = min(PAGES_PER_STEP, n_pages)
    width = n_pages * PAGE_SIZE + LANES
    return pl.pallas_call(
        functools.partial(_dsa_select_kernel, n_pg=n_pg, n_sel=n_sel),
        grid_spec=pltpu.PrefetchScalarGridSpec(
            num_scalar_prefetch=1,
            grid=(bd, n_pages // n_pg),
            in_specs=[_per_seq(q.shape), _per_seq(wcol.shape), _per_seq(knew.shape)]
                     + _page_specs(cache_idx.shape, n_pg),
            out_specs=pl.BlockSpec((None, 8, width), lambda b, st, pt: (b, 0, 0)),
            scratch_shapes=[pltpu.VMEM((8, width), F32)],
        ),
        out_shape=jax.ShapeDtypeStruct((bd, 8, width), F32),
        compiler_params=_cparams("parallel", "arbitrary"),
        name="dsa_select",
    )(page_table, q, wcol, knew, *([cache_idx] * n_pg))


def _dsa_sample_kernel(pt_ref, q_ref, bias_ref, knew_ref, vnew_ref, *rest, n_pg):
    kpages, vpages = rest[:n_pg], rest[n_pg:2 * n_pg]
    o_ref, m_ref, l_ref, acc_ref = rest[2 * n_pg:]
    st = pl.program_id(1)

    @pl.when(st == 0)
    def _():
        _softmax_init(m_ref, l_ref, acc_ref)

    n_g, rg, _ = q_ref.shape
    n_rep = n_g * rg // 8

    def bias_rows(cols):
        bt = bias_ref[:, cols]
        return jnp.concatenate([bt] * n_rep, axis=0)

    step_keys = n_pg * PAGE_SIZE
    col0 = pl.multiple_of(st * step_keys, step_keys)
    _attend(q_ref, lambda g: _page_lanes(kpages, g), lambda g: _page_lanes(vpages, g), n_g, True,
            bias_rows(pl.ds(col0, step_keys)), m_ref, l_ref, acc_ref)

    @pl.when(st == pl.num_programs(1) - 1)
    def _():
        past = bias_ref.shape[1] - LANES
        _attend(q_ref, lambda g: knew_ref[g], lambda g: vnew_ref[g], n_g, True,
                bias_rows(slice(past, past + LANES)), m_ref, l_ref, acc_ref)
        o = acc_ref[...] / l_ref[...]
        for g in range(n_g):
            for j in range(rg // 8):
                lo = j * LANES + g * HEAD_DIM
                o_ref[:, lo:lo + HEAD_DIM] = o[g * rg + 8 * j:g * rg + 8 * j + 8]


def _dsa_sample(page_table, q, bias, knew, vnew, cache_k, cache_v):
    bd, n_pages = page_table.shape
    n_pg = min(PAGES_PER_STEP, n_pages)
    rows = q.shape[1] * q.shape[2]
    return pl.pallas_call(
        functools.partial(_dsa_sample_kernel, n_pg=n_pg),
        grid_spec=pltpu.PrefetchScalarGridSpec(
            num_scalar_prefetch=1,
            grid=(bd, n_pages // n_pg),
            in_specs=[_per_seq(q.shape), _per_seq(bias.shape), _per_seq(knew.shape), _per_seq(vnew.shape)]
                     + _page_specs(cache_k.shape, n_pg) + _page_specs(cache_v.shape, n_pg),
            out_specs=pl.BlockSpec((None, 8, 512), lambda b, st, pt: (b, 0, 0)),
            scratch_shapes=[pltpu.VMEM((rows, 1), F32), pltpu.VMEM((rows, 1), F32),
                            pltpu.VMEM((rows, HEAD_DIM), F32)],
        ),
        out_shape=jax.ShapeDtypeStruct((bd, 8, 512), F32),
        compiler_params=_cparams("parallel", "arbitrary"),
        name="dsa_sample",
    )(page_table, q, bias, knew, vnew, *([cache_k] * n_pg), *([cache_v] * n_pg))


def _diff_sample_kernel(pt_ref, lamv_ref, gsub_ref, q_ref, knew_ref, vnew_ref, *rest, n_pg, lam_init):
    kpages, vpages = rest[:n_pg], rest[n_pg:2 * n_pg]
    o_ref, m_ref, l_ref, acc_ref = rest[2 * n_pg:]
    st = pl.program_id(1)

    @pl.when(st == 0)
    def _():
        _softmax_init(m_ref, l_ref, acc_ref)

    n_k = q_ref.shape[0]
    _attend(q_ref, lambda i: _page_lanes(kpages, i // 2, i % 2), lambda h: _page_rows(vpages, h), C_HEADS, False,
            0.0, m_ref, l_ref, acc_ref)

    @pl.when(st == pl.num_programs(1) - 1)
    def _():
        bias = jnp.where(_new_key_mask(8 * n_k), 0.0, NEG_INF)
        _attend(q_ref, lambda i: knew_ref[i], lambda h: vnew_ref[h], C_HEADS, False, bias, m_ref, l_ref, acc_ref)
        lam = _diff_lambda(lamv_ref, lam_init)
        o = acc_ref[...] / l_ref[...]
        for h in range(C_HEADS):
            oh = o[16 * h:16 * h + 8] - lam * o[16 * h + 8:16 * h + 16]
            o_ref[:, h * LANES:(h + 1) * LANES] = _rms(oh, gsub_ref[...]) * (1.0 - lam_init)


def _diff_sample(page_table, lamv, gsub, q, knew, vnew, cache_k, cache_v, lam_init):
    bd, n_pages = page_table.shape
    n_pg = min(PAGES_PER_STEP, n_pages)
    rows = q.shape[1] * q.shape[2]
    cst = lambda arr: pl.BlockSpec(arr.shape, lambda b, st, pt: (0, 0))
    return pl.pallas_call(
        functools.partial(_diff_sample_kernel, n_pg=n_pg, lam_init=lam_init),
        grid_spec=pltpu.PrefetchScalarGridSpec(
            num_scalar_prefetch=1,
            grid=(bd, n_pages // n_pg),
            in_specs=[cst(lamv), cst(gsub), _per_seq(q.shape), _per_seq(knew.shape), _per_seq(vnew.shape)]
                     + _page_specs(cache_k.shape, n_pg) + _page_specs(cache_v.shape, n_pg),
            out_specs=pl.BlockSpec((None, 8, 512), lambda b, st, pt: (b, 0, 0)),
            scratch_shapes=[pltpu.VMEM((rows, 1), F32), pltpu.VMEM((rows, 1), F32),
                            pltpu.VMEM((rows, 2 * HEAD_DIM), F32)],
        ),
        out_shape=jax.ShapeDtypeStruct((bd, 8, 512), F32),
        compiler_params=_cparams("parallel", "arbitrary"),
        name="diff_sample",
    )(page_table, lamv, gsub, q, knew, vnew, *([cache_k] * n_pg), *([cache_v] * n_pg))


def _fox_sample_kernel(pt_ref, q_ref, lft_ref, knew_ref, vnew_ref, *rest, n_pg):
    kpages, vpages, fpages = rest[:n_pg], rest[n_pg:2 * n_pg], rest[2 * n_pg:3 * n_pg]
    o_ref, m_ref, l_ref, acc_ref, carry_ref = rest[3 * n_pg:]
    st = pl.program_id(1)

    @pl.when(st == 0)
    def _():
        _softmax_init(m_ref, l_ref, acc_ref)
        carry_ref[...] = jnp.zeros(carry_ref.shape, F32)

    n_h = q_ref.shape[0]
    cn64 = _rows8(_lane_prefix(lft_ref[...], _tri(incl_upper=True)))
    mine = lax.broadcasted_iota(jnp.int32, cn64.shape, 1) == lax.broadcasted_iota(jnp.int32, cn64.shape, 0) % 8
    cn_col = jnp.sum(jnp.where(mine, cn64, 0.0), axis=1, keepdims=True)
    tri_after = _tri(strict_lower=True)

    carry = carry_ref[...]
    sfx = []
    for pg in range(n_pg):
        x = fpages[pg][...]
        sfx.append(_lane_prefix(x, tri_after) + carry)
        carry = carry + jnp.sum(x, axis=1, keepdims=True)
    carry_ref[...] = carry
    sfx = jnp.concatenate(sfx, axis=1)
    _attend(q_ref, lambda h: _page_lanes(kpages, h), lambda h: _page_lanes(vpages, h), n_h, True,
            cn_col + _rows8(sfx), m_ref, l_ref, acc_ref)

    @pl.when(st == pl.num_programs(1) - 1)
    def _():
        bias = jnp.where(_new_key_mask(8 * n_h), cn_col - cn64, NEG_INF)
        _attend(q_ref, lambda h: knew_ref[h], lambda h: vnew_ref[h], n_h, True, bias, m_ref, l_ref, acc_ref)
        o = acc_ref[...] / l_ref[...]
        for h in range(n_h):
            o_ref[:, h * HEAD_DIM:(h + 1) * HEAD_DIM] = o[8 * h:8 * h + 8]


def _fox_sample(page_table, q, lf_t, knew, vnew, cache_k, cache_v, cache_lf_t):
    bd, n_pages = page_table.shape
    n_pg = min(PAGES_PER_STEP, n_pages)
    rows = q.shape[1] * q.shape[2]
    last = n_pages - 1
    return pl.pallas_call(
        functools.partial(_fox_sample_kernel, n_pg=n_pg),
        grid_spec=pltpu.PrefetchScalarGridSpec(
            num_scalar_prefetch=1,
            grid=(bd, n_pages // n_pg),
            in_specs=[_per_seq(q.shape), _per_seq(lf_t.shape), _per_seq(knew.shape), _per_seq(vnew.shape)]
                     + _page_specs(cache_k.shape, n_pg, last) + _page_specs(cache_v.shape, n_pg, last)
                     + _page_specs(cache_lf_t.shape, n_pg, last),
            out_specs=pl.BlockSpec((None, 8, 512), lambda b, st, pt: (b, 0, 0)),
            scratch_shapes=[pltpu.VMEM((rows, 1), F32), pltpu.VMEM((rows, 1), F32),
                            pltpu.VMEM((rows, HEAD_DIM), F32), pltpu.VMEM((8, 1), F32)],
        ),
        out_shape=jax.ShapeDtypeStruct((bd, 8, 512), F32),
        compiler_params=_cparams("parallel", "arbitrary"),
        name="fox_sample",
    )(page_table, q, lf_t, knew, vnew, *([cache_k] * n_pg), *([cache_v] * n_pg), *([cache_lf_t] * n_pg))


def _rope_tables(pos):
    rot = HEAD_DIM // 4
    half = rot // 2
    inv = ROPE_THETA ** (-np.arange(half, dtype=np.float64) / half)
    ang = np.asarray(pos, np.float64)[:, None] * inv[None, :]
    lane = np.arange(LANES) % HEAD_DIM
    c = np.ones((len(pos), LANES), np.float64)
    s1 = np.zeros_like(c)
    s2 = np.zeros_like(c)
    first, second = lane < half, (lane >= half) & (lane < rot)
    c[:, first] = np.cos(ang)[:, lane[first]]
    c[:, second] = np.cos(ang)[:, lane[second] - half]
    s1[:, first] = -np.sin(ang)[:, lane[first]]
    s2[:, second] = np.sin(ang)[:, lane[second] - half]
    return tuple(jnp.asarray(a, F32) for a in (c, s1, s2))


def _pack_w_ab(w):
    q, k, v, qi, ki, wi, u, gv = jnp.split(w, np.cumsum(
        (512, 128, 128, 512, IDX_DIM, IDX_HEADS, GM_WIDTH, GM_WIDTH))[:-1].tolist(), axis=1)
    q = q.reshape(D_MODEL, A_HEADS, HEAD_DIM)[:, np.array(A_HEAD_PERM)].reshape(D_MODEL, 512)
    pad = jnp.zeros((D_MODEL, LANES - IDX_DIM - IDX_HEADS), w.dtype)
    return jnp.concatenate([q, k, v, qi, ki, wi, pad, u, gv], axis=1).astype(BF16)


def _pack_w_cd(w):
    pad = jnp.zeros((D_MODEL, CD_COLS - w.shape[1]), w.dtype)
    return jnp.concatenate([w, pad], axis=1).astype(BF16)


def _spatial_blockdiag(w_s, b_s, tm, rows_per_chunk):
    wt = (w_s * jnp.tril(jnp.ones((CHUNK, CHUNK), w_s.dtype)))[:, :rows_per_chunk, :rows_per_chunk]
    n_blk = tm // rows_per_chunk
    eye = jnp.eye(n_blk, dtype=w_s.dtype)
    wsp = jnp.einsum("ab,gts->gatbs", eye, wt).reshape(GM_GROUPS, tm, tm).astype(BF16)
    bias = jnp.repeat(b_s[:, :rows_per_chunk].T, HEAD_DIM, axis=1)
    return wsp, jnp.tile(bias, (n_blk, 1))


def _pad_rows(x, rows):
    return jnp.pad(x, ((0, 0), (0, rows - x.shape[1]), (0, 0)))


def _position_minor(cache):
    return jnp.moveaxis(cache, 1, -1)


def _group_minor(x, bd, t_new, n_groups, width):
    x = x.reshape(bd, t_new, n_groups, width).transpose(0, 2, 3, 1)
    return jnp.pad(x, ((0, 0), (0, 0), (0, 0), (0, PAGE_SIZE - t_new)))


def _group_major(x, bd, t_new, n_groups, width, pad_to=None):
    x = x.reshape(bd, t_new, n_groups, width).transpose(0, 2, 1, 3)
    if pad_to is not None:
        x = jnp.pad(x, ((0, 0), (0, 0), (0, pad_to - t_new), (0, 0)))
    return x


def kernel(x_prompt, x_sample, cache_a_k, cache_a_v, cache_a_idx, cache_c_k, cache_c_v, cache_d_k, cache_d_v,
           cache_d_logf, page_table, c_prompt, c_sample, w_mod, b_mod, g_norm, w_in_ab, w_out_ab, gm_norm_v,
           gm_spatial_w, gm_spatial_b, w_in_cd, w_out_cd, fox_bias, lam_q1, lam_k1, lam_q2, lam_k2, g_subln,
           w_gate, w_up, w_down):
    b, s_len, _ = x_prompt.shape
    bd, t_new, _ = x_sample.shape
    n_pages = page_table.shape[1]
    past_len = n_pages * PAGE_SIZE
    n_pool = cache_a_k.shape[1]
    assert t_new == 8 and s_len % CHUNK == 0 and (bd * t_new) % 8 == 0
    mp, ms = b * s_len, bd * t_new

    mod = _modulation(jnp.concatenate([c_prompt, c_sample], axis=0), w_mod, b_mod)
    mod = mod.reshape(mod.shape[0], b + bd, 6, D_MODEL)

    def mods(layer):
        p = [mod[layer, :b, i].reshape(b, 1, D_MODEL) for i in range(6)]
        s = [jnp.repeat(mod[layer, b:, i], t_new, axis=0) for i in range(6)]
        return p, s

    tps_p = lambda tm: s_len // tm
    tps_s = lambda tm: 1
    rope_p = _rope_tables(np.arange(s_len))
    rope_s = _rope_tables(np.tile(past_len + np.arange(t_new), bd))
    gvec = lambda layer, i: g_norm[layer, i].reshape(1, D_MODEL)

    xp = x_prompt.reshape(mp, D_MODEL)
    xs = x_sample.reshape(ms, D_MODEL)
    pt = page_table.astype(jnp.int32)

    li = 0
    mod_p, mod_s = mods(0)
    w_ab = _pack_w_ab(w_in_ab[li])
    gm_g = gm_norm_v[li].reshape(1, GM_WIDTH)
    tm_p, tm_s = min(ROW_TILE, mp), min(ROW_TILE, ms)
    wsp_p, bsp_p = _spatial_blockdiag(gm_spatial_w[li], gm_spatial_b[li], tm_p, CHUNK)
    wsp_s, bsp_s = _spatial_blockdiag(gm_spatial_w[li], gm_spatial_b[li], tm_s, t_new)
    wo = w_out_ab[li]
    wo_a = wo[:512].reshape(A_HEADS, HEAD_DIM, D_MODEL)[np.array(A_HEAD_PERM)].reshape(512, D_MODEL).astype(BF16)
    wo_b = wo[512:].astype(BF16)
    n_sel_p = min(TOPK_MAX, s_len // 4)
    n_sel_s = min(TOPK_MAX, (past_len + t_new) // 4)

    (q_p, k32_p, v32_p, kiw_p, kbf_p, vbf_p, qi_p, kid_p, ob_p) = _inproj_ab(
        xp, gvec(0, 0), mod_p[0], mod_p[1], w_ab, rope_p, gm_g, wsp_p, bsp_p, tps_p(tm_p), False)
    (q_s, k32_s, v32_s, kiw_s, kbf_s, vbf_s, qi_s, kid_s, ob_s, gv_s) = _inproj_ab(
        xs, gvec(0, 0), mod_s[0], mod_s[1], w_ab, rope_s, gm_g, wsp_s, bsp_s, 1, True)

    r3 = lambda a: a.reshape(b, s_len, a.shape[-1])
    wi_p = kiw_p[:, IDX_DIM:IDX_DIM + IDX_HEADS]
    oa_p = _dsa_prompt(r3(qi_p), r3(wi_p), r3(kid_p), r3(q_p), r3(kbf_p), r3(vbf_p), n_sel_p)

    qi_stk = qi_s.reshape(bd, t_new, IDX_HEADS, IDX_DIM).transpose(0, 2, 1, 3).reshape(bd, 64, IDX_DIM)
    wi_col = kiw_s[:, IDX_DIM:IDX_DIM + IDX_HEADS].reshape(bd, t_new, IDX_HEADS).transpose(0, 2, 1).reshape(bd, 64, 1)
    ki_new = _group_minor(kid_s[:, :IDX_DIM], bd, t_new, 1, IDX_DIM)[:, 0]
    sel_bias = _dsa_select(pt, qi_stk, wi_col, ki_new, _position_minor(cache_a_idx[li]), n_sel_s)

    q_stk = q_s.reshape(bd, t_new, 4, 2, HEAD_DIM).transpose(0, 3, 2, 1, 4)
    q_stk = q_stk.reshape(bd, A_KV_HEADS, 4 * t_new, HEAD_DIM)
    o_s = _dsa_sample(pt, q_stk, sel_bias,
                      _group_minor(kbf_s, bd, t_new, A_KV_HEADS, HEAD_DIM),
                      _group_minor(vbf_s, bd, t_new, A_KV_HEADS, HEAD_DIM),
                      _position_minor(cache_a_k[li]), _position_minor(cache_a_v[li]))
    oa_s = o_s.reshape(ms, 512).astype(BF16)

    xp = _outproj(oa_p.reshape(mp, 512), ob_p, wo_a, wo_b, xp, mod_p[2], gvec(0, 1), tps_p)
    xs = _outproj(oa_s, ob_s, wo_a, wo_b, xs, mod_s[2], gvec(0, 1), tps_s)
    wg, wu, wd = w_gate[0].astype(BF16), w_up[0].astype(BF16), w_down[0].astype(BF16)
    xp = _ffn(xp, gvec(0, 2), mod_p[3], mod_p[4], wg, wu, wd, mod_p[5], gvec(0, 3), tps_p)
    xs = _ffn(xs, gvec(0, 2), mod_s[3], mod_s[4], wg, wu, wd, mod_s[5], gvec(0, 3), tps_s)

    a_k_p = k32_p.reshape(1, b, s_len, A_KV_HEADS, HEAD_DIM)
    a_v_p = v32_p.reshape(1, b, s_len, A_KV_HEADS, HEAD_DIM)
    a_idx_p = kiw_p[:, :IDX_DIM].reshape(1, b, s_len, IDX_DIM)
    a_k_s = k32_s.reshape(1, bd, t_new, A_KV_HEADS, HEAD_DIM)
    a_v_s = v32_s.reshape(1, bd, t_new, A_KV_HEADS, HEAD_DIM)
    a_idx_s = kiw_s[:, :IDX_DIM].reshape(1, bd, t_new, IDX_DIM)
    gm_v_s = gv_s.reshape(1, bd, t_new, GM_WIDTH)

    layer = 1
    mod_p, mod_s = mods(layer)
    lam_init = 0.8 - 0.6 * math.exp(-0.3 * layer)
    w_cd = _pack_w_cd(w_in_cd[li])
    fox_b = jnp.pad(fox_bias[li], (0, LANES - D_HEADS)).reshape(1, LANES)
    lamv = jnp.stack([lam_q1[li], lam_k1[li], lam_q2[li], lam_k2[li]])
    gsub = g_subln[li].reshape(1, 2 * HEAD_DIM)
    wo = w_out_cd[li]
    wo_c, wo_d = wo[:512].astype(BF16), wo[512:].astype(BF16)

    (cq_p, ck32_p, cv32_p, ckbf_p, cvbf_p, dq_p, dk32_p, dv32_p, dkbf_p, dvbf_p, lf_p) = _inproj_cd(
        xp, gvec(1, 0), mod_p[0], mod_p[1], w_cd, rope_p, fox_b, tps_p(tm_p))
    (cq_s, ck32_s, cv32_s, ckbf_s, cvbf_s, dq_s, dk32_s, dv32_s, dkbf_s, dvbf_s, lf_s) = _inproj_cd(
        xs, gvec(1, 0), mod_s[0], mod_s[1], w_cd, rope_s, fox_b, 1)

    oc_p = _diff_prompt(lamv, gsub, r3(cq_p), r3(ckbf_p), r3(cvbf_p), lam_init)
    logf_p = lf_p[:, :D_HEADS].reshape(b, s_len, D_HEADS)
    cum_t = _cumsum_lanes(logf_p.transpose(0, 2, 1))
    od_p = _fox_prompt(r3(dq_p), r3(dkbf_p), r3(dvbf_p), cum_t.transpose(0, 2, 1), cum_t)

    oc_s = _diff_sample(pt, lamv, gsub, _group_major(cq_s, bd, t_new, 2 * C_HEADS, HEAD_DIM),
                        _group_minor(ckbf_s, bd, t_new, 2 * C_HEADS, HEAD_DIM),
                        _group_major(cvbf_s, bd, t_new, C_HEADS, 2 * HEAD_DIM, PAGE_SIZE),
                        _position_minor(cache_c_k[li]), cache_c_v[li], lam_init)

    logf_s = lf_s[:, :D_HEADS].reshape(bd, t_new, D_HEADS)
    lf_t = jnp.pad(logf_s.transpose(0, 2, 1), ((0, 0), (0, 0), (0, LANES - t_new)))
    od_s = _fox_sample(pt, _group_major(dq_s, bd, t_new, D_HEADS, HEAD_DIM), lf_t,
                       _group_minor(dkbf_s, bd, t_new, D_HEADS, HEAD_DIM),
                       _group_minor(dvbf_s, bd, t_new, D_HEADS, HEAD_DIM),
                       _position_minor(cache_d_k[li]), _position_minor(cache_d_v[li]),
                       _position_minor(cache_d_logf[li]))

    xp = _outproj(oc_p.reshape(mp, 512), od_p.reshape(mp, 512), wo_c, wo_d, xp, mod_p[2], gvec(1, 1), tps_p)
    xs = _outproj(oc_s.reshape(ms, 512).astype(BF16), od_s.reshape(ms, 512).astype(BF16), wo_c, wo_d, xs,
                  mod_s[2], gvec(1, 1), tps_s)
    wg, wu, wd = w_gate[1].astype(BF16), w_up[1].astype(BF16), w_down[1].astype(BF16)
    xp = _ffn(xp, gvec(1, 2), mod_p[3], mod_p[4], wg, wu, wd, mod_p[5], gvec(1, 3), tps_p)
    xs = _ffn(xs, gvec(1, 2), mod_s[3], mod_s[4], wg, wu, wd, mod_s[5], gvec(1, 3), tps_s)

    def shp(a, g, *tail):
        return a.reshape((1,) + g + tail)

    gp, gs = (b, s_len), (bd, t_new)
    return (xp.reshape(b, s_len, D_MODEL), xs.reshape(bd, t_new, D_MODEL),
            a_k_p, a_v_p, a_idx_p,
            shp(ck32_p, gp, C_HEADS, 2, HEAD_DIM), shp(cv32_p, gp, C_HEADS, 2 * HEAD_DIM),
            shp(dk32_p, gp, D_HEADS, HEAD_DIM), shp(dv32_p, gp, D_HEADS, HEAD_DIM), shp(logf_p, gp, D_HEADS),
            a_k_s, a_v_s, a_idx_s,
            shp(ck32_s, gs, C_HEADS, 2, HEAD_DIM), shp(cv32_s, gs, C_HEADS, 2 * HEAD_DIM),
            shp(dk32_s, gs, D_HEADS, HEAD_DIM), shp(dv32_s, gs, D_HEADS, HEAD_DIM), shp(logf_s, gs, D_HEADS),
            gm_v_s)
```

```python
import functools
import math

import numpy as np
import jax
import jax.numpy as jnp
from jax import lax
from jax.experimental import pallas as pl
from jax.experimental.pallas import tpu as pltpu

F32 = jnp.float32
BF16 = jnp.bfloat16
NEG_INF = float("-inf")

D_MODEL = 1024
HEAD_DIM = 64
ROPE_THETA = 500000.0
A_HEADS = 8
A_KV_HEADS = 2
IDX_HEADS = 8
IDX_DIM = 64
TOPK_MAX = 256
GM_GROUPS = 8
GM_WIDTH = GM_GROUPS * HEAD_DIM
CHUNK = 128
C_HEADS = 4
D_HEADS = 8
PAGE_SIZE = 128
EPS = 1e-6
D_FF = -(-(8 * D_MODEL) // (3 * 256)) * 256

LANES = 128
VMEM_LIMIT_BYTES = 56 * 1024 * 1024

ROW_TILE = 256
FFN_ROW_TILE = 256
Q_TILE = 128
CAUSAL_SEGMENTS = 8
SELECT_PAGES_PER_STEP = 32
DSA_PAGES_PER_STEP = 16
CD_PAGES_PER_STEP = 16
MOD_COL_TILE = 1536
BISECT_STEPS = 20

AB_Q, AB_K, AB_V, AB_QI, AB_KIW, AB_U, AB_GV, AB_COLS = 0, 512, 640, 768, 1280, 1408, 1920, 2432
CD_CQ, CD_CK, CD_CV, CD_DQ, CD_DK, CD_DV, CD_DF, CD_COLS = 0, 512, 1024, 1536, 2048, 2560, 3072, 3200
A_HEAD_PERM = (0, 4, 1, 5, 2, 6, 3, 7)


def _cparams(*sem):
    return pltpu.CompilerParams(dimension_semantics=sem, vmem_limit_bytes=VMEM_LIMIT_BYTES)


def _nt(a, b):
    return lax.dot_general(a, b, (((1,), (1,)), ((), ())), preferred_element_type=F32)


def _mm(a, b):
    return jnp.dot(a, b, preferred_element_type=F32)


def _rms(x, g):
    return x * lax.rsqrt(jnp.mean(x * x, axis=-1, keepdims=True) + EPS) * g


def _silu(x):
    return x / (1.0 + jnp.exp(-x))


def _gelu_tanh(x):
    return 0.5 * x * (1.0 + jnp.tanh(math.sqrt(2.0 / math.pi) * (x + 0.044715 * (x * x * x))))


def _half_masks(dtype):
    lane = lax.broadcasted_iota(jnp.int32, (1, LANES), 1)
    lo = (lane < HEAD_DIM).astype(dtype)
    return lo, (1 - lo).astype(dtype)


def _rope(z, c, s1, s2):
    outs = []
    for j in range(z.shape[1] // LANES):
        x = z[:, j * LANES:(j + 1) * LANES]
        outs.append(x * c + pltpu.roll(x, LANES - 8, 1) * s1 + pltpu.roll(x, 8, 1) * s2)
    return outs[0] if len(outs) == 1 else jnp.concatenate(outs, axis=1)


def _split3(x):
    hi = x.astype(BF16)
    r1 = x - hi.astype(F32)
    mid = r1.astype(BF16)
    lo = (r1 - mid.astype(F32)).astype(BF16)
    return hi, mid, lo


def _lane_prefix(x, tri):
    hi, mid, lo = _split3(x)
    r = _mm(jnp.concatenate([hi, mid, lo], axis=0), tri)
    return r[0:8] + r[8:16] + r[16:24]


def _tri(incl_upper=None, strict_lower=None):
    r = lax.broadcasted_iota(jnp.int32, (LANES, LANES), 0)
    c = lax.broadcasted_iota(jnp.int32, (LANES, LANES), 1)
    if incl_upper:
        return (r <= c).astype(BF16)
    return (r > c).astype(BF16)


def _count_ge(sm, x):
    return jnp.sum(jnp.where(sm >= x, 1.0, 0.0), axis=1, keepdims=True)


def _topk_bias(sm_ref, bias_ref, n_sel):
    kf = float(n_sel)
    sm = sm_ref[...]
    allowed = sm > NEG_INF
    mx = jnp.max(sm, axis=1, keepdims=True)
    mn = jnp.min(jnp.where(allowed, sm, jnp.inf), axis=1, keepdims=True)
    n_allowed = jnp.sum(jnp.where(allowed, 1.0, 0.0), axis=1, keepdims=True)
    few = n_allowed <= kf
    top_tie = _count_ge(sm, mx) >= kf

    def bisect(_, c):
        lo, hi = c
        mid = lo + (hi - lo) * 0.5
        ge = _count_ge(sm_ref[...], mid) >= kf
        return jnp.where(ge, mid, lo), jnp.where(ge, hi, mid)

    _, hi = lax.fori_loop(0, BISECT_STEPS, bisect, (mn, mx))

    done0 = jnp.where(few, 1.0, jnp.where(top_tie, 1.0, 0.0))
    thr0 = jnp.where(few, NEG_INF, mx)

    def not_done(c):
        return c[3] > 0.0

    def step(c):
        hi, thr, done, _ = c
        s = sm_ref[...]
        m = jnp.max(jnp.where(s < hi, s, NEG_INF), axis=1, keepdims=True)
        found = _count_ge(s, m) >= kf
        is_done = done > 0.5
        thr = jnp.where(is_done, thr, jnp.where(found, m, thr))
        hi = jnp.where(is_done, hi, jnp.where(found, hi, m))
        done = jnp.where(found, 1.0, done)
        return hi, thr, done, jnp.sum(1.0 - done)

    _, thr, _, _ = lax.while_loop(not_done, step, (hi, thr0, done0, jnp.sum(1.0 - done0)))

    is_gt = sm > thr
    is_eq = jnp.where(sm == thr, jnp.where(allowed, 1.0, 0.0), 0.0)
    need = kf - jnp.sum(jnp.where(is_gt, 1.0, 0.0), axis=1, keepdims=True)
    n_eq = jnp.sum(is_eq, axis=1, keepdims=True)
    bias_ref[...] = jnp.where(sm >= thr, jnp.where(allowed, 0.0, NEG_INF), NEG_INF)

    @pl.when(jnp.max(n_eq - need) > 0.0)
    def _():
        tri = _tri(incl_upper=True)

        def blk(j, run):
            cols = pl.ds(pl.multiple_of(j * LANES, LANES), LANES)
            s = sm_ref[:, cols]
            e = jnp.where(s == thr, jnp.where(s > NEG_INF, 1.0, 0.0), 0.0)
            rank = run + _mm(e.astype(BF16), tri)
            keep = jnp.where(e > 0.5, jnp.where(rank <= need, 1.0, 0.0), 0.0)
            bias_ref[:, cols] = jnp.where(s > thr, 0.0, jnp.where(keep > 0.5, 0.0, NEG_INF))
            return run + jnp.sum(e, axis=1, keepdims=True)

        lax.fori_loop(0, sm_ref.shape[1] // LANES, blk, jnp.zeros_like(need))


def _mod_kernel(c_ref, w_ref, b_ref, o_ref):
    a = _silu(c_ref[...]).astype(BF16)
    o_ref[...] = _mm(a, w_ref[...].astype(BF16)) + b_ref[...]


def _modulation(c_all, w_mod, b_mod):
    n_layers, _, n_out = w_mod.shape
    rows = c_all.shape[0]
    tn = MOD_COL_TILE
    return pl.pallas_call(
        _mod_kernel,
        grid=(n_layers, n_out // tn),
        in_specs=[
            pl.BlockSpec((rows, D_MODEL), lambda l, n: (0, 0)),
            pl.BlockSpec((None, D_MODEL, tn), lambda l, n: (l, 0, n)),
            pl.BlockSpec((None, 1, tn), lambda l, n: (l, 0, n)),
        ],
        out_specs=pl.BlockSpec((None, rows, tn), lambda l, n: (l, 0, n)),
        out_shape=jax.ShapeDtypeStruct((n_layers, rows, n_out), F32),
        compiler_params=_cparams("arbitrary", "arbitrary"),
        name="modulation",
    )(c_all, w_mod, b_mod.reshape(n_layers, 1, n_out))


def _rowvec_spec(vec, tm, tiles_per_seq):
    if vec.ndim == 3:
        return pl.BlockSpec((None, 1, vec.shape[-1]), lambda i: (i // tiles_per_seq, 0, 0))
    return pl.BlockSpec((tm, vec.shape[-1]), lambda i: (i, 0))


def _const_spec(arr):
    nd = arr.ndim
    return pl.BlockSpec(arr.shape, lambda i: (0,) * nd)


def _position_minor_out(m, tm, tiles_per_seq):
    seq_len = tm * tiles_per_seq
    spec = lambda width: pl.BlockSpec((None, width, tm), lambda i: (i // tiles_per_seq, 0, i % tiles_per_seq))
    shape = lambda width: jax.ShapeDtypeStruct((m // seq_len, width, seq_len), F32)
    return spec, shape


def _inproj_ab_kernel(x_ref, g_ref, sh_ref, sc_ref, w_ref, rc_ref, rs1_ref, rs2_ref, gmg_ref, wsp_ref, bsp_ref,
                      q_ref, k32_ref, v32_ref, kiw_ref, kbf_ref, vbf_ref, qi_ref, kid_ref, ob_ref, *gv_ref):
    h = (_rms(x_ref[...], g_ref[...]) * (1.0 + sc_ref[...]) + sh_ref[...]).astype(BF16)
    c, s1, s2 = rc_ref[...], rs1_ref[...], rs2_ref[...]

    def seg(lo, hi):
        return _mm(h, w_ref[:, lo:hi])

    q_ref[...] = (_rope(seg(AB_Q, AB_K), c, s1, s2) * HEAD_DIM ** -0.5).astype(BF16)
    k = _rope(seg(AB_K, AB_V), c, s1, s2)
    k32_ref[...] = k.T
    kbf_ref[...] = k.astype(BF16)
    v = seg(AB_V, AB_QI)
    v32_ref[...] = v.T
    vbf_ref[...] = v.astype(BF16)
    qi_ref[...] = _rope(seg(AB_QI, AB_KIW), c, s1, s2).astype(BF16)
    kz = seg(AB_KIW, AB_U)
    kr = _rope(kz, c, s1, s2)
    lane = lax.broadcasted_iota(jnp.int32, kz.shape, 1)
    kiw_ref[...] = jnp.where(lane < IDX_DIM, kr, kz).T
    kid_ref[...] = jnp.where(lane < IDX_DIM, kr, pltpu.roll(kr, IDX_DIM, 1)).astype(BF16)

    u = _gelu_tanh(seg(AB_U, AB_GV))
    gv = _gelu_tanh(seg(AB_GV, AB_COLS))
    gc = gv - jnp.mean(gv, axis=-1, keepdims=True)
    gvn = gc * lax.rsqrt(jnp.mean(gc * gc, axis=-1, keepdims=True) + EPS) * gmg_ref[...]
    if gv_ref:
        gv_ref[0][...] = gvn
    gvb = gvn.astype(BF16)
    lane = lax.broadcasted_iota(jnp.int32, (1, LANES), 1)
    mix = []
    for j in range(GM_GROUPS // 2):
        slab = gvb[:, j * LANES:(j + 1) * LANES]
        mix.append(jnp.where(lane < HEAD_DIM, _mm(wsp_ref[2 * j], slab), _mm(wsp_ref[2 * j + 1], slab)))
    ob_ref[...] = (u * (jnp.concatenate(mix, axis=1) + bsp_ref[...])).astype(BF16)


def _inproj_ab(x, g, shift, scale, w, rope_tabs, gm_g, wsp, bsp, tiles_per_seq, want_gv):
    m = x.shape[0]
    tm = min(ROW_TILE, m)
    n_tab_tiles = rope_tabs[0].shape[0] // tm
    row = lambda width: pl.BlockSpec((tm, width), lambda i: (i, 0))
    tab = pl.BlockSpec((tm, LANES), lambda i: (i % n_tab_tiles, 0))
    sds = lambda width, dt: jax.ShapeDtypeStruct((m, width), dt)
    col, col_sds = _position_minor_out(m, tm, tiles_per_seq)
    out_shape = [sds(512, BF16), col_sds(128), col_sds(128), col_sds(128), sds(128, BF16), sds(128, BF16),
                 sds(512, BF16), sds(128, BF16), sds(512, BF16)]
    out_specs = [row(512), col(128), col(128), col(128), row(128), row(128), row(512), row(128), row(512)]
    if want_gv:
        out_shape.append(sds(512, F32))
        out_specs.append(row(512))
    return pl.pallas_call(
        _inproj_ab_kernel,
        grid=(m // tm,),
        in_specs=[row(D_MODEL), _const_spec(g), _rowvec_spec(shift, tm, tiles_per_seq),
                  _rowvec_spec(scale, tm, tiles_per_seq), _const_spec(w), tab, tab, tab,
                  _const_spec(gm_g), _const_spec(wsp), _const_spec(bsp)],
        out_specs=out_specs,
        out_shape=out_shape,
        compiler_params=_cparams("parallel"),
        name="inproj_ab",
    )(x, g, shift, scale, w, *rope_tabs, gm_g, wsp, bsp)


def _inproj_cd_kernel(x_ref, g_ref, sh_ref, sc_ref, w_ref, rc_ref, rs1_ref, rs2_ref, fb_ref,
                      cq_ref, ck32_ref, cv32_ref, ckbf_ref, cvbf_ref,
                      dq_ref, dk32_ref, dv32_ref, dkbf_ref, dvbf_ref, lf_ref):
    h = (_rms(x_ref[...], g_ref[...]) * (1.0 + sc_ref[...]) + sh_ref[...]).astype(BF16)
    c, s1, s2 = rc_ref[...], rs1_ref[...], rs2_ref[...]

    def seg(lo, hi):
        return _mm(h, w_ref[:, lo:hi])

    cq_ref[...] = (_rope(seg(CD_CQ, CD_CK), c, s1, s2) * HEAD_DIM ** -0.5).astype(BF16)
    ck = _rope(seg(CD_CK, CD_CV), c, s1, s2)
    ck32_ref[...] = ck.T
    ckbf_ref[...] = ck.astype(BF16)
    cv = seg(CD_CV, CD_DQ)
    cv32_ref[...] = cv
    cvbf_ref[...] = cv.astype(BF16)
    dq_ref[...] = (seg(CD_DQ, CD_DK) * HEAD_DIM ** -0.5).astype(BF16)
    dk = seg(CD_DK, CD_DV)
    dk32_ref[...] = dk.T
    dkbf_ref[...] = dk.astype(BF16)
    dv = seg(CD_DV, CD_DF)
    dv32_ref[...] = dv.T
    dvbf_ref[...] = dv.astype(BF16)
    f = seg(CD_DF, CD_COLS) + fb_ref[...]
    lf_ref[...] = (jnp.minimum(f, 0.0) - jnp.log(1.0 + jnp.exp(-jnp.abs(f)))).T


def _inproj_cd(x, g, shift, scale, w, rope_tabs, fox_b, tiles_per_seq):
    m = x.shape[0]
    tm = min(ROW_TILE, m)
    n_tab_tiles = rope_tabs[0].shape[0] // tm
    row = lambda width: pl.BlockSpec((tm, width), lambda i: (i, 0))
    tab = pl.BlockSpec((tm, LANES), lambda i: (i % n_tab_tiles, 0))
    sds = lambda width, dt: jax.ShapeDtypeStruct((m, width), dt)
    col, col_sds = _position_minor_out(m, tm, tiles_per_seq)
    return pl.pallas_call(
        _inproj_cd_kernel,
        grid=(m // tm,),
        in_specs=[row(D_MODEL), _const_spec(g), _rowvec_spec(shift, tm, tiles_per_seq),
                  _rowvec_spec(scale, tm, tiles_per_seq), _const_spec(w), tab, tab, tab, _const_spec(fox_b)],
        out_specs=[row(512), col(512), row(512), row(512), row(512),
                   row(512), col(512), col(512), row(512), row(512), col(128)],
        out_shape=[sds(512, BF16), col_sds(512), sds(512, F32), sds(512, BF16), sds(512, BF16),
                   sds(512, BF16), col_sds(512), col_sds(512), sds(512, BF16), sds(512, BF16), col_sds(128)],
        compiler_params=_cparams("parallel"),
        name="inproj_cd",
    )(x, g, shift, scale, w, *rope_tabs, fox_b)


def _outproj_kernel(a_ref, b_ref, wa_ref, wb_ref, x_ref, gate_ref, g_ref, o_ref):
    op = _mm(a_ref[...], wa_ref[...]) + _mm(b_ref[...], wb_ref[...])
    o_ref[...] = x_ref[...] + gate_ref[...] * _rms(op, g_ref[...])


def _outproj(a, b, wa, wb, x, gate, g, tiles_per_seq_of):
    m = x.shape[0]
    tm = min(FFN_ROW_TILE, m)
    row = lambda width: pl.BlockSpec((tm, width), lambda i: (i, 0))
    return pl.pallas_call(
        _outproj_kernel,
        grid=(m // tm,),
        in_specs=[row(a.shape[1]), row(b.shape[1]), _const_spec(wa), _const_spec(wb), row(D_MODEL),
                  _rowvec_spec(gate, tm, tiles_per_seq_of(tm)), _const_spec(g)],
        out_specs=row(D_MODEL),
        out_shape=jax.ShapeDtypeStruct((m, D_MODEL), F32),
        compiler_params=_cparams("parallel"),
        name="outproj",
    )(a, b, wa, wb, x, gate, g)


def _ffn_kernel(x_ref, g2_ref, sh_ref, sc_ref, wg_ref, wu_ref, wd_ref, gate_ref, g3_ref, o_ref):
    x = x_ref[...]
    h = (_rms(x, g2_ref[...]) * (1.0 + sc_ref[...]) + sh_ref[...]).astype(BF16)
    act = (_silu(_mm(h, wg_ref[...])) * _mm(h, wu_ref[...])).astype(BF16)
    o_ref[...] = x + gate_ref[...] * _rms(_mm(act, wd_ref[...]), g3_ref[...])


def _ffn(x, g2, shift, scale, wg, wu, wd, gate, g3, tiles_per_seq_of):
    m = x.shape[0]
    tm = min(FFN_ROW_TILE, m)
    tps = tiles_per_seq_of(tm)
    row = pl.BlockSpec((tm, D_MODEL), lambda i: (i, 0))
    return pl.pallas_call(
        _ffn_kernel,
        grid=(m // tm,),
        in_specs=[row, _const_spec(g2), _rowvec_spec(shift, tm, tps), _rowvec_spec(scale, tm, tps),
                  _const_spec(wg), _const_spec(wu), _const_spec(wd), _rowvec_spec(gate, tm, tps), _const_spec(g3)],
        out_specs=row,
        out_shape=jax.ShapeDtypeStruct((m, D_MODEL), F32),
        compiler_params=_cparams("parallel"),
        name="ffn",
    )(x, g2, shift, scale, wg, wu, wd, gate, g3)


def _dsa_prompt_kernel(qi_ref, wi_ref, kid_ref, q_ref, k_ref, v_ref, o_ref, sm_ref, bias_ref, *, n_sel, q_blk0):
    tq, s_len = sm_ref.shape
    i = q_blk0 + pl.program_id(1)
    m_lo, m_hi = _half_masks(BF16)
    kid = kid_ref[...]
    score = jnp.zeros((tq, s_len), F32)
    for j in range(IDX_HEADS // 2):
        slab = qi_ref[:, j * LANES:(j + 1) * LANES]
        s = _nt(jnp.concatenate([slab * m_lo, slab * m_hi], axis=0), kid)
        score = score + jnp.maximum(s[:tq], 0.0) * wi_ref[:, 2 * j:2 * j + 1]
        score = score + jnp.maximum(s[tq:], 0.0) * wi_ref[:, 2 * j + 1:2 * j + 2]
    score = score * (IDX_HEADS * IDX_DIM) ** -0.5
    row = i * tq + lax.broadcasted_iota(jnp.int32, (tq, s_len), 0)
    col = lax.broadcasted_iota(jnp.int32, (tq, s_len), 1)
    sm_ref[...] = jnp.where(col <= row, score, NEG_INF)
    _topk_bias(sm_ref, bias_ref, n_sel)

    k = k_ref[...]
    v = v_ref[...]
    n_slab = A_HEADS // 2
    outs = []
    for mask in (m_lo, m_hi):
        stk = jnp.concatenate([q_ref[:, j * LANES:(j + 1) * LANES] * mask for j in range(n_slab)], axis=0)
        lg = _nt(stk, k)
        per = []
        for j in range(n_slab):
            l = lg[j * tq:(j + 1) * tq] + bias_ref[...]
            p = jnp.exp(l - jnp.max(l, axis=1, keepdims=True))
            per.append(_mm(p.astype(BF16), v) / jnp.sum(p, axis=1, keepdims=True))
        outs.append(per)
    lane = lax.broadcasted_iota(jnp.int32, (1, LANES), 1)
    for j in range(n_slab):
        o_ref[:, j * LANES:(j + 1) * LANES] = jnp.where(lane < HEAD_DIM, outs[0][j], outs[1][j]).astype(BF16)


def _causal_segments(s_len, tq):
    n_seg = min(CAUSAL_SEGMENTS, s_len // tq)
    per = s_len // tq // n_seg
    return [(seg * per, per, (seg + 1) * per * tq) for seg in range(n_seg)]


def _dsa_prompt(qi, wi, kid, q, k, v, n_sel):
    b, s_len, _ = q.shape
    tq = min(Q_TILE, s_len)
    outs = []
    for q_blk0, n_blk, kext in _causal_segments(s_len, tq):
        qblk = lambda width: pl.BlockSpec((None, tq, width), lambda bb, i: (bb, q_blk0 + i, 0))
        keys = lambda width: pl.BlockSpec((None, kext, width), lambda bb, i: (bb, 0, 0))
        outs.append(pl.pallas_call(
            functools.partial(_dsa_prompt_kernel, n_sel=n_sel, q_blk0=q_blk0),
            grid=(b, n_blk),
            in_specs=[qblk(512), qblk(IDX_HEADS), keys(128), qblk(512), keys(128), keys(128)],
            out_specs=pl.BlockSpec((None, tq, 512), lambda bb, i: (bb, i, 0)),
            out_shape=jax.ShapeDtypeStruct((b, n_blk * tq, 512), BF16),
            scratch_shapes=[pltpu.VMEM((tq, kext), F32), pltpu.VMEM((tq, kext), F32)],
            compiler_params=_cparams("parallel", "arbitrary"),
            name="dsa_prompt",
        )(qi, wi, kid, q, k, v))
    return jnp.concatenate(outs, axis=1)


def _diff_lambda(lamv_ref, lam_init):
    lv = lamv_ref[...]
    d1 = jnp.sum(lv[0:1] * lv[1:2], axis=1, keepdims=True)
    d2 = jnp.sum(lv[2:3] * lv[3:4], axis=1, keepdims=True)
    return jnp.exp(d1) - jnp.exp(d2) + lam_init


def _diff_prompt_kernel(lamv_ref, gsub_ref, q_ref, k_ref, v_ref, o_ref, *, lam_init, q_blk0):
    tq = q_ref.shape[0]
    s_len = k_ref.shape[0]
    i = q_blk0 + pl.program_id(1)
    lam = _diff_lambda(lamv_ref, lam_init)
    m_lo, m_hi = _half_masks(BF16)
    row = i * tq + lax.broadcasted_iota(jnp.int32, (2 * tq, s_len), 0) % tq
    col = lax.broadcasted_iota(jnp.int32, (2 * tq, s_len), 1)
    causal = col <= row
    for h in range(C_HEADS):
        cols = slice(h * LANES, (h + 1) * LANES)
        slab = q_ref[:, cols]
        lg = _nt(jnp.concatenate([slab * m_lo, slab * m_hi], axis=0), k_ref[:, cols])
        lg = jnp.where(causal, lg, NEG_INF)
        p = jnp.exp(lg - jnp.max(lg, axis=1, keepdims=True))
        pv = _mm(p.astype(BF16), v_ref[:, cols]) / jnp.sum(p, axis=1, keepdims=True)
        o = pv[:tq] - lam * pv[tq:]
        o_ref[:, cols] = (_rms(o, gsub_ref[...]) * (1.0 - lam_init)).astype(BF16)


def _diff_prompt(lamv, gsub, q, k, v, lam_init):
    b, s_len, _ = q.shape
    tq = min(Q_TILE, s_len)
    cst = lambda arr: pl.BlockSpec(arr.shape, lambda bb, i: (0, 0))
    outs = []
    for q_blk0, n_blk, kext in _causal_segments(s_len, tq):
        qblk = pl.BlockSpec((None, tq, 512), lambda bb, i: (bb, q_blk0 + i, 0))
        keys = pl.BlockSpec((None, kext, 512), lambda bb, i: (bb, 0, 0))
        outs.append(pl.pallas_call(
            functools.partial(_diff_prompt_kernel, lam_init=lam_init, q_blk0=q_blk0),
            grid=(b, n_blk),
            in_specs=[cst(lamv), cst(gsub), qblk, keys, keys],
            out_specs=pl.BlockSpec((None, tq, 512), lambda bb, i: (bb, i, 0)),
            out_shape=jax.ShapeDtypeStruct((b, n_blk * tq, 512), BF16),
            compiler_params=_cparams("parallel", "arbitrary"),
            name="diff_prompt",
        )(lamv, gsub, q, k, v))
    return jnp.concatenate(outs, axis=1)


def _cumsum_kernel(x_ref, o_ref):
    tri = _tri(incl_upper=True)
    carry = jnp.zeros((8, 1), F32)
    for c in range(x_ref.shape[1] // LANES):
        cols = slice(c * LANES, (c + 1) * LANES)
        out = _lane_prefix(x_ref[:, cols], tri) + carry
        o_ref[:, cols] = out
        carry = out[:, LANES - 1:LANES]


def _cumsum_lanes(x):
    b, h, s_len = x.shape
    blk = pl.BlockSpec((None, h, s_len), lambda bb: (bb, 0, 0))
    return pl.pallas_call(
        _cumsum_kernel,
        grid=(b,),
        in_specs=[blk],
        out_specs=blk,
        out_shape=jax.ShapeDtypeStruct(x.shape, F32),
        compiler_params=_cparams("parallel"),
        name="fox_cumsum",
    )(x)


def _fox_prompt_kernel(q_ref, k_ref, v_ref, cq_ref, ck_ref, o_ref, *, q_blk0):
    tq = q_ref.shape[0]
    s_len = k_ref.shape[0]
    i = q_blk0 + pl.program_id(1)
    m_lo, m_hi = _half_masks(BF16)
    row = i * tq + lax.broadcasted_iota(jnp.int32, (tq, s_len), 0)
    col = lax.broadcasted_iota(jnp.int32, (tq, s_len), 1)
    causal = col <= row
    lane = lax.broadcasted_iota(jnp.int32, (1, LANES), 1)
    for j in range(D_HEADS // 2):
        cols = slice(j * LANES, (j + 1) * LANES)
        slab = q_ref[:, cols]
        lg = _nt(jnp.concatenate([slab * m_lo, slab * m_hi], axis=0), k_ref[:, cols])
        v = v_ref[:, cols]
        pv = []
        for hh in range(2):
            h = 2 * j + hh
            l = lg[hh * tq:(hh + 1) * tq] + cq_ref[:, h:h + 1] - ck_ref[h:h + 1, :]
            l = jnp.where(causal, l, NEG_INF)
            p = jnp.exp(l - jnp.max(l, axis=1, keepdims=True))
            pv.append(_mm(p.astype(BF16), v) / jnp.sum(p, axis=1, keepdims=True))
        o_ref[:, cols] = jnp.where(lane < HEAD_DIM, pv[0], pv[1]).astype(BF16)


def _fox_prompt(q, k, v, cum, cum_t):
    b, s_len, _ = q.shape
    tq = min(Q_TILE, s_len)
    outs = []
    for q_blk0, n_blk, kext in _causal_segments(s_len, tq):
        qblk = lambda width: pl.BlockSpec((None, tq, width), lambda bb, i: (bb, q_blk0 + i, 0))
        keys = pl.BlockSpec((None, kext, 512), lambda bb, i: (bb, 0, 0))
        outs.append(pl.pallas_call(
            functools.partial(_fox_prompt_kernel, q_blk0=q_blk0),
            grid=(b, n_blk),
            in_specs=[qblk(512), keys, keys, qblk(D_HEADS),
                      pl.BlockSpec((None, D_HEADS, kext), lambda bb, i: (bb, 0, 0))],
            out_specs=pl.BlockSpec((None, tq, 512), lambda bb, i: (bb, i, 0)),
            out_shape=jax.ShapeDtypeStruct((b, n_blk * tq, 512), BF16),
            compiler_params=_cparams("parallel", "arbitrary"),
            name="fox_prompt",
        )(q, k, v, cum, cum_t))
    return jnp.concatenate(outs, axis=1)


def _pages_per_step(n_pages, cap):
    return max(d for d in range(1, min(cap, n_pages) + 1) if n_pages % d == 0)


def _page_specs(block, n_pg, reverse_from=None):
    specs = []
    for pg in range(n_pg):
        if reverse_from is None:
            idx = lambda b, st, pt, pg=pg: (pt[b, st * n_pg + pg],) + (0,) * (len(block) - 1)
        else:
            idx = lambda b, st, pt, pg=pg: (pt[b, reverse_from - (st * n_pg + pg)],) + (0,) * (len(block) - 1)
        specs.append(pl.BlockSpec((None,) + tuple(block[1:]), idx))
    return specs


def _per_seq(shape):
    nd = len(shape)
    return pl.BlockSpec((None,) + tuple(shape[1:]), lambda b, st, pt: (b,) + (0,) * (nd - 1))


def _softmax_init(m_ref, l_ref, acc_ref):
    m_ref[...] = jnp.full(m_ref.shape, NEG_INF, F32)
    l_ref[...] = jnp.zeros(l_ref.shape, F32)
    acc_ref[...] = jnp.zeros(acc_ref.shape, F32)


def _attend(q_ref, keys_t_of, values_of, n_v, values_t, bias, m_ref, l_ref, acc_ref):
    n_k, r, _ = q_ref.shape
    pv = _nt if values_t else _mm
    lg = jnp.concatenate([_mm(q_ref[i], keys_t_of(i)) for i in range(n_k)], axis=0) + bias
    m_old = m_ref[...]
    m_new = jnp.maximum(m_old, jnp.max(lg, axis=1, keepdims=True))
    m_safe = jnp.where(m_new == NEG_INF, 0.0, m_new)
    alpha = jnp.exp(m_old - m_safe)
    p = jnp.exp(lg - m_safe)
    l_ref[...] = alpha * l_ref[...] + jnp.sum(p, axis=1, keepdims=True)
    m_ref[...] = m_new
    p = p.astype(BF16)
    rv = r * (n_k // n_v)
    for j in range(n_v):
        rows = slice(j * rv, (j + 1) * rv)
        acc_ref[rows, :] = alpha[rows] * acc_ref[rows, :] + pv(p[rows], values_of(j))


def _page_lanes(pages, *idx):
    return jnp.concatenate([(p[idx] if idx else p[...]).astype(BF16) for p in pages], axis=1)


def _rows8(x):
    return jnp.concatenate([jnp.broadcast_to(x[h:h + 1], (8, x.shape[1])) for h in range(x.shape[0])], axis=0)


def _head_sum(w, n_groups):
    out = w[0:8]
    for h in range(1, n_groups):
        out = out + w[8 * h:8 * h + 8]
    return out


def _new_key_mask(rows):
    t = lax.broadcasted_iota(jnp.int32, (rows, LANES), 0) % 8
    c = lax.broadcasted_iota(jnp.int32, (rows, LANES), 1)
    return c <= t


def _dsa_select_kernel(pt_ref, q_ref, w_ref, knew_ref, *rest, n_pg, n_sel):
    pages = rest[:n_pg]
    bias_ref, sm_ref = rest[n_pg], rest[n_pg + 1]
    st = pl.program_id(1)
    q = q_ref[...]
    wcol = w_ref[...]
    scale = (IDX_HEADS * IDX_DIM) ** -0.5

    def scores(keys_t):
        return _head_sum(jnp.maximum(_mm(q, keys_t), 0.0) * wcol, IDX_HEADS) * scale

    step_keys = n_pg * PAGE_SIZE
    col0 = pl.multiple_of(st * step_keys, step_keys)
    sm_ref[:, pl.ds(col0, step_keys)] = scores(_page_lanes(pages))

    @pl.when(st == pl.num_programs(1) - 1)
    def _():
        past = sm_ref.shape[1] - LANES
        sm_ref[:, past:] = jnp.where(_new_key_mask(8), scores(knew_ref[...]), NEG_INF)
        _topk_bias(sm_ref, bias_ref, n_sel)


def _dsa_select(page_table, q, wcol, knew, cache_idx, n_sel):
    bd, n_pages = page_table.shape
    n_pg = _pages_per_step(n_pages, SELECT_PAGES_PER_STEP)
    width = n_pages * PAGE_SIZE + LANES
    return pl.pallas_call(
        functools.partial(_dsa_select_kernel, n_pg=n_pg, n_sel=n_sel),
        grid_spec=pltpu.PrefetchScalarGridSpec(
            num_scalar_prefetch=1,
            grid=(bd, n_pages // n_pg),
            in_specs=[_per_seq(q.shape), _per_seq(wcol.shape), _per_seq(knew.shape)]
                     + _page_specs(cache_idx.shape, n_pg),
            out_specs=pl.BlockSpec((None, 8, width), lambda b, st, pt: (b, 0, 0)),
            scratch_shapes=[pltpu.VMEM((8, width), F32)],
        ),
        out_shape=jax.ShapeDtypeStruct((bd, 8, width), F32),
        compiler_params=_cparams("parallel", "arbitrary"),
        name="dsa_select",
    )(page_table, q, wcol, knew, *([cache_idx] * n_pg))


def _dsa_sample_kernel(pt_ref, q_ref, bias_ref, knew_ref, vnew_ref, *rest, n_pg):
    kpages, vpages = rest[:n_pg], rest[n_pg:2 * n_pg]
    o_ref, m_ref, l_ref, acc_ref = rest[2 * n_pg:]
    st = pl.program_id(1)

    @pl.when(st == 0)
    def _():
        _softmax_init(m_ref, l_ref, acc_ref)

    n_g, rg, _ = q_ref.shape
    n_rep = n_g * rg // 8

    def bias_rows(cols):
        bt = bias_ref[:, cols]
        return jnp.concatenate([bt] * n_rep, axis=0)

    step_keys = n_pg * PAGE_SIZE
    col0 = pl.multiple_of(st * step_keys, step_keys)
    _attend(q_ref, lambda g: _page_lanes(kpages, g), lambda g: _page_lanes(vpages, g), n_g, True,
            bias_rows(pl.ds(col0, step_keys)), m_ref, l_ref, acc_ref)

    @pl.when(st == pl.num_programs(1) - 1)
    def _():
        past = bias_ref.shape[1] - LANES
        _attend(q_ref, lambda g: knew_ref[g], lambda g: vnew_ref[g], n_g, True,
                bias_rows(slice(past, past + LANES)), m_ref, l_ref, acc_ref)
        o = acc_ref[...] / l_ref[...]
        for g in range(n_g):
            for j in range(rg // 8):
                lo = j * LANES + g * HEAD_DIM
                o_ref[:, lo:lo + HEAD_DIM] = o[g * rg + 8 * j:g * rg + 8 * j + 8]


def _dsa_sample(page_table, q, bias, knew, vnew, cache_k, cache_v):
    bd, n_pages = page_table.shape
    n_pg = _pages_per_step(n_pages, DSA_PAGES_PER_STEP)
    rows = q.shape[1] * q.shape[2]
    return pl.pallas_call(
        functools.partial(_dsa_sample_kernel, n_pg=n_pg),
        grid_spec=pltpu.PrefetchScalarGridSpec(
            num_scalar_prefetch=1,
            grid=(bd, n_pages // n_pg),
            in_specs=[_per_seq(q.shape), _per_seq(bias.shape), _per_seq(knew.shape), _per_seq(vnew.shape)]
                     + _page_specs(cache_k.shape, n_pg) + _page_specs(cache_v.shape, n_pg),
            out_specs=pl.BlockSpec((None, 8, 512), lambda b, st, pt: (b, 0, 0)),
            scratch_shapes=[pltpu.VMEM((rows, 1), F32), pltpu.VMEM((rows, 1), F32),
                            pltpu.VMEM((rows, HEAD_DIM), F32)],
        ),
        out_shape=jax.ShapeDtypeStruct((bd, 8, 512), F32),
        compiler_params=_cparams("parallel", "arbitrary"),
        name="dsa_sample",
    )(page_table, q, bias, knew, vnew, *([cache_k] * n_pg), *([cache_v] * n_pg))


def _diff_sample_kernel(pt_ref, lamv_ref, gsub_ref, q_ref, knew_ref, vnew_ref, *rest, n_pg, lam_init):
    kpages, vpages = rest[:n_pg], rest[n_pg:2 * n_pg]
    o_ref, m_ref, l_ref, acc_ref = rest[2 * n_pg:]
    st = pl.program_id(1)

    @pl.when(st == 0)
    def _():
        _softmax_init(m_ref, l_ref, acc_ref)

    n_k = q_ref.shape[0]

    def values(h):
        rows = pl.ds(h, PAGE_SIZE, stride=C_HEADS)
        return jnp.concatenate([p[rows, :].astype(BF16) for p in vpages], axis=0)

    _attend(q_ref, lambda i: _page_lanes(kpages, i // 2, i % 2), values, C_HEADS, False,
            0.0, m_ref, l_ref, acc_ref)

    @pl.when(st == pl.num_programs(1) - 1)
    def _():
        bias = jnp.where(_new_key_mask(8 * n_k), 0.0, NEG_INF)
        _attend(q_ref, lambda i: knew_ref[i], lambda h: vnew_ref[h], C_HEADS, False, bias, m_ref, l_ref, acc_ref)
        lam = _diff_lambda(lamv_ref, lam_init)
        o = acc_ref[...] / l_ref[...]
        for h in range(C_HEADS):
            oh = o[16 * h:16 * h + 8] - lam * o[16 * h + 8:16 * h + 16]
            o_ref[:, h * LANES:(h + 1) * LANES] = _rms(oh, gsub_ref[...]) * (1.0 - lam_init)


def _diff_sample(page_table, lamv, gsub, q, knew, vnew, cache_k, cache_v, lam_init):
    bd, n_pages = page_table.shape
    n_pg = _pages_per_step(n_pages, CD_PAGES_PER_STEP)
    rows = q.shape[1] * q.shape[2]
    cst = lambda arr: pl.BlockSpec(arr.shape, lambda b, st, pt: (0, 0))
    return pl.pallas_call(
        functools.partial(_diff_sample_kernel, n_pg=n_pg, lam_init=lam_init),
        grid_spec=pltpu.PrefetchScalarGridSpec(
            num_scalar_prefetch=1,
            grid=(bd, n_pages // n_pg),
            in_specs=[cst(lamv), cst(gsub), _per_seq(q.shape), _per_seq(knew.shape), _per_seq(vnew.shape)]
                     + _page_specs(cache_k.shape, n_pg) + _page_specs(cache_v.shape, n_pg),
            out_specs=pl.BlockSpec((None, 8, 512), lambda b, st, pt: (b, 0, 0)),
            scratch_shapes=[pltpu.VMEM((rows, 1), F32), pltpu.VMEM((rows, 1), F32),
                            pltpu.VMEM((rows, 2 * HEAD_DIM), F32)],
        ),
        out_shape=jax.ShapeDtypeStruct((bd, 8, 512), F32),
        compiler_params=_cparams("parallel", "arbitrary"),
        name="diff_sample",
    )(page_table, lamv, gsub, q, knew, vnew, *([cache_k] * n_pg), *([cache_v] * n_pg))


def _fox_sample_kernel(pt_ref, q_ref, lft_ref, knew_ref, vnew_ref, *rest, n_pg):
    kpages, vpages, fpages = rest[:n_pg], rest[n_pg:2 * n_pg], rest[2 * n_pg:3 * n_pg]
    o_ref, m_ref, l_ref, acc_ref, carry_ref = rest[3 * n_pg:]
    st = pl.program_id(1)

    @pl.when(st == 0)
    def _():
        _softmax_init(m_ref, l_ref, acc_ref)
        carry_ref[...] = jnp.zeros(carry_ref.shape, F32)

    n_h = q_ref.shape[0]
    cn64 = _rows8(_lane_prefix(lft_ref[...], _tri(incl_upper=True)))
    mine = lax.broadcasted_iota(jnp.int32, cn64.shape, 1) == lax.broadcasted_iota(jnp.int32, cn64.shape, 0) % 8
    cn_col = jnp.sum(jnp.where(mine, cn64, 0.0), axis=1, keepdims=True)
    tri_after = _tri(strict_lower=True)

    carry = carry_ref[...]
    sfx = []
    for pg in range(n_pg):
        x = fpages[pg][...]
        sfx.append(_lane_prefix(x, tri_after) + carry)
        carry = carry + jnp.sum(x, axis=1, keepdims=True)
    carry_ref[...] = carry
    sfx = jnp.concatenate(sfx, axis=1)
    _attend(q_ref, lambda h: _page_lanes(kpages, h), lambda h: _page_lanes(vpages, h), n_h, True,
            cn_col + _rows8(sfx), m_ref, l_ref, acc_ref)

    @pl.when(st == pl.num_programs(1) - 1)
    def _():
        bias = jnp.where(_new_key_mask(8 * n_h), cn_col - cn64, NEG_INF)
        _attend(q_ref, lambda h: knew_ref[h], lambda h: vnew_ref[h], n_h, True, bias, m_ref, l_ref, acc_ref)
        o = acc_ref[...] / l_ref[...]
        for h in range(n_h):
            o_ref[:, h * HEAD_DIM:(h + 1) * HEAD_DIM] = o[8 * h:8 * h + 8]


def _fox_sample(page_table, q, lf_t, knew, vnew, cache_k, cache_v, cache_lf_t):
    bd, n_pages = page_table.shape
    n_pg = _pages_per_step(n_pages, CD_PAGES_PER_STEP)
    rows = q.shape[1] * q.shape[2]
    last = n_pages - 1
    return pl.pallas_call(
        functools.partial(_fox_sample_kernel, n_pg=n_pg),
        grid_spec=pltpu.PrefetchScalarGridSpec(
            num_scalar_prefetch=1,
            grid=(bd, n_pages // n_pg),
            in_specs=[_per_seq(q.shape), _per_seq(lf_t.shape), _per_seq(knew.shape), _per_seq(vnew.shape)]
                     + _page_specs(cache_k.shape, n_pg, last) + _page_specs(cache_v.shape, n_pg, last)
                     + _page_specs(cache_lf_t.shape, n_pg, last),
            out_specs=pl.BlockSpec((None, 8, 512), lambda b, st, pt: (b, 0, 0)),
            scratch_shapes=[pltpu.VMEM((rows, 1), F32), pltpu.VMEM((rows, 1), F32),
                            pltpu.VMEM((rows, HEAD_DIM), F32), pltpu.VMEM((8, 1), F32)],
        ),
        out_shape=jax.ShapeDtypeStruct((bd, 8, 512), F32),
        compiler_params=_cparams("parallel", "arbitrary"),
        name="fox_sample",
    )(page_table, q, lf_t, knew, vnew, *([cache_k] * n_pg), *([cache_v] * n_pg), *([cache_lf_t] * n_pg))


def _rope_tables(pos):
    rot = HEAD_DIM // 4
    half = rot // 2
    inv = ROPE_THETA ** (-np.arange(half, dtype=np.float64) / half)
    ang = np.asarray(pos, np.float64)[:, None] * inv[None, :]
    lane = np.arange(LANES) % HEAD_DIM
    c = np.ones((len(pos), LANES), np.float64)
    s1 = np.zeros_like(c)
    s2 = np.zeros_like(c)
    first, second = lane < half, (lane >= half) & (lane < rot)
    c[:, first] = np.cos(ang)[:, lane[first]]
    c[:, second] = np.cos(ang)[:, lane[second] - half]
    s1[:, first] = -np.sin(ang)[:, lane[first]]
    s2[:, second] = np.sin(ang)[:, lane[second] - half]
    return tuple(jnp.asarray(a, F32) for a in (c, s1, s2))


def _pack_w_ab(w):
    q, k, v, qi, ki, wi, u, gv = jnp.split(w, np.cumsum(
        (512, 128, 128, 512, IDX_DIM, IDX_HEADS, GM_WIDTH, GM_WIDTH))[:-1].tolist(), axis=1)
    q = q.reshape(D_MODEL, A_HEADS, HEAD_DIM)[:, np.array(A_HEAD_PERM)].reshape(D_MODEL, 512)
    pad = jnp.zeros((D_MODEL, LANES - IDX_DIM - IDX_HEADS), w.dtype)
    return jnp.concatenate([q, k, v, qi, ki, wi, pad, u, gv], axis=1).astype(BF16)


def _pack_w_cd(w):
    pad = jnp.zeros((D_MODEL, CD_COLS - w.shape[1]), w.dtype)
    return jnp.concatenate([w, pad], axis=1).astype(BF16)


def _spatial_blockdiag(w_s, b_s, tm, rows_per_chunk):
    wt = (w_s * jnp.tril(jnp.ones((CHUNK, CHUNK), w_s.dtype)))[:, :rows_per_chunk, :rows_per_chunk]
    n_blk = tm // rows_per_chunk
    eye = jnp.eye(n_blk, dtype=w_s.dtype)
    wsp = jnp.einsum("ab,gts->gatbs", eye, wt).reshape(GM_GROUPS, tm, tm).astype(BF16)
    bias = jnp.repeat(b_s[:, :rows_per_chunk].T, HEAD_DIM, axis=1)
    return wsp, jnp.tile(bias, (n_blk, 1))


def _pad_rows(x, rows):
    return jnp.pad(x, ((0, 0), (0, rows - x.shape[1]), (0, 0)))


def _cache_out(xt, lead, tail):
    n_seq, _, length = xt.shape
    x = jnp.moveaxis(xt.reshape((n_seq,) + tail + (length,)), -1, 1)
    return x.reshape((1,) + lead + tail)


def _position_minor(cache):
    return jnp.moveaxis(cache, 1, -1)


def _group_minor(x, bd, t_new, n_groups, width):
    x = x.reshape(bd, t_new, n_groups, width).transpose(0, 2, 3, 1)
    return jnp.pad(x, ((0, 0), (0, 0), (0, 0), (0, PAGE_SIZE - t_new)))


def _group_major(x, bd, t_new, n_groups, width, pad_to=None):
    x = x.reshape(bd, t_new, n_groups, width).transpose(0, 2, 1, 3)
    if pad_to is not None:
        x = jnp.pad(x, ((0, 0), (0, 0), (0, pad_to - t_new), (0, 0)))
    return x


def kernel(x_prompt, x_sample, cache_a_k, cache_a_v, cache_a_idx, cache_c_k, cache_c_v, cache_d_k, cache_d_v,
           cache_d_logf, page_table, c_prompt, c_sample, w_mod, b_mod, g_norm, w_in_ab, w_out_ab, gm_norm_v,
           gm_spatial_w, gm_spatial_b, w_in_cd, w_out_cd, fox_bias, lam_q1, lam_k1, lam_q2, lam_k2, g_subln,
           w_gate, w_up, w_down):
    b, s_len, _ = x_prompt.shape
    bd, t_new, _ = x_sample.shape
    n_pages = page_table.shape[1]
    past_len = n_pages * PAGE_SIZE
    n_pool = cache_a_k.shape[1]
    assert t_new == 8 and s_len % CHUNK == 0 and (bd * t_new) % 8 == 0
    mp, ms = b * s_len, bd * t_new

    mod = _modulation(jnp.concatenate([c_prompt, c_sample], axis=0), w_mod, b_mod)
    mod = mod.reshape(mod.shape[0], b + bd, 6, D_MODEL)

    def mods(layer):
        p = [mod[layer, :b, i].reshape(b, 1, D_MODEL) for i in range(6)]
        s = [jnp.repeat(mod[layer, b:, i], t_new, axis=0) for i in range(6)]
        return p, s

    tps_p = lambda tm: s_len // tm
    tps_s = lambda tm: 1
    rope_p = _rope_tables(np.arange(s_len))
    rope_s = _rope_tables(np.tile(past_len + np.arange(t_new), bd))
    gvec = lambda layer, i: g_norm[layer, i].reshape(1, D_MODEL)

    xp = x_prompt.reshape(mp, D_MODEL)
    xs = x_sample.reshape(ms, D_MODEL)
    pt = page_table.astype(jnp.int32)

    li = 0
    mod_p, mod_s = mods(0)
    w_ab = _pack_w_ab(w_in_ab[li])
    gm_g = gm_norm_v[li].reshape(1, GM_WIDTH)
    tm_p, tm_s = min(ROW_TILE, mp), min(ROW_TILE, ms)
    wsp_p, bsp_p = _spatial_blockdiag(gm_spatial_w[li], gm_spatial_b[li], tm_p, CHUNK)
    wsp_s, bsp_s = _spatial_blockdiag(gm_spatial_w[li], gm_spatial_b[li], tm_s, t_new)
    wo = w_out_ab[li]
    wo_a = wo[:512].reshape(A_HEADS, HEAD_DIM, D_MODEL)[np.array(A_HEAD_PERM)].reshape(512, D_MODEL).astype(BF16)
    wo_b = wo[512:].astype(BF16)
    n_sel_p = min(TOPK_MAX, s_len // 4)
    n_sel_s = min(TOPK_MAX, (past_len + t_new) // 4)

    (q_p, k32_p, v32_p, kiw_p, kbf_p, vbf_p, qi_p, kid_p, ob_p) = _inproj_ab(
        xp, gvec(0, 0), mod_p[0], mod_p[1], w_ab, rope_p, gm_g, wsp_p, bsp_p, tps_p(tm_p), False)
    (q_s, k32_s, v32_s, kiw_s, kbf_s, vbf_s, qi_s, kid_s, ob_s, gv_s) = _inproj_ab(
        xs, gvec(0, 0), mod_s[0], mod_s[1], w_ab, rope_s, gm_g, wsp_s, bsp_s, 1, True)

    r3 = lambda a: a.reshape(b, s_len, a.shape[-1])
    wi_rows = lambda kiw: kiw[:, IDX_DIM:IDX_DIM + IDX_HEADS].transpose(0, 2, 1).reshape(-1, IDX_HEADS)
    oa_p = _dsa_prompt(r3(qi_p), r3(wi_rows(kiw_p)), r3(kid_p), r3(q_p), r3(kbf_p), r3(vbf_p), n_sel_p)

    qi_stk = qi_s.reshape(bd, t_new, IDX_HEADS, IDX_DIM).transpose(0, 2, 1, 3).reshape(bd, 64, IDX_DIM)
    wi_col = wi_rows(kiw_s).reshape(bd, t_new, IDX_HEADS).transpose(0, 2, 1).reshape(bd, 64, 1)
    ki_new = _group_minor(kid_s[:, :IDX_DIM], bd, t_new, 1, IDX_DIM)[:, 0]
    sel_bias = _dsa_select(pt, qi_stk, wi_col, ki_new, _position_minor(cache_a_idx[li]), n_sel_s)

    q_stk = q_s.reshape(bd, t_new, 4, 2, HEAD_DIM).transpose(0, 3, 2, 1, 4)
    q_stk = q_stk.reshape(bd, A_KV_HEADS, 4 * t_new, HEAD_DIM)
    o_s = _dsa_sample(pt, q_stk, sel_bias,
                      _group_minor(kbf_s, bd, t_new, A_KV_HEADS, HEAD_DIM),
                      _group_minor(vbf_s, bd, t_new, A_KV_HEADS, HEAD_DIM),
                      _position_minor(cache_a_k[li]), _position_minor(cache_a_v[li]))
    oa_s = o_s.reshape(ms, 512).astype(BF16)

    xp = _outproj(oa_p.reshape(mp, 512), ob_p, wo_a, wo_b, xp, mod_p[2], gvec(0, 1), tps_p)
    xs = _outproj(oa_s, ob_s, wo_a, wo_b, xs, mod_s[2], gvec(0, 1), tps_s)
    wg, wu, wd = w_gate[0].astype(BF16), w_up[0].astype(BF16), w_down[0].astype(BF16)
    xp = _ffn(xp, gvec(0, 2), mod_p[3], mod_p[4], wg, wu, wd, mod_p[5], gvec(0, 3), tps_p)
    xs = _ffn(xs, gvec(0, 2), mod_s[3], mod_s[4], wg, wu, wd, mod_s[5], gvec(0, 3), tps_s)

    gp, gs = (b, s_len), (bd, t_new)
    a_k_p = _cache_out(k32_p, gp, (A_KV_HEADS, HEAD_DIM))
    a_v_p = _cache_out(v32_p, gp, (A_KV_HEADS, HEAD_DIM))
    a_idx_p = _cache_out(kiw_p[:, :IDX_DIM], gp, (IDX_DIM,))
    a_k_s = _cache_out(k32_s, gs, (A_KV_HEADS, HEAD_DIM))
    a_v_s = _cache_out(v32_s, gs, (A_KV_HEADS, HEAD_DIM))
    a_idx_s = _cache_out(kiw_s[:, :IDX_DIM], gs, (IDX_DIM,))
    gm_v_s = gv_s.reshape(1, bd, t_new, GM_WIDTH)

    layer = 1
    mod_p, mod_s = mods(layer)
    lam_init = 0.8 - 0.6 * math.exp(-0.3 * layer)
    w_cd = _pack_w_cd(w_in_cd[li])
    fox_b = jnp.pad(fox_bias[li], (0, LANES - D_HEADS)).reshape(1, LANES)
    lamv = jnp.stack([lam_q1[li], lam_k1[li], lam_q2[li], lam_k2[li]])
    gsub = g_subln[li].reshape(1, 2 * HEAD_DIM)
    wo = w_out_cd[li]
    wo_c, wo_d = wo[:512].astype(BF16), wo[512:].astype(BF16)

    (cq_p, ck32_p, cv32_p, ckbf_p, cvbf_p, dq_p, dk32_p, dv32_p, dkbf_p, dvbf_p, lf_p) = _inproj_cd(
        xp, gvec(1, 0), mod_p[0], mod_p[1], w_cd, rope_p, fox_b, tps_p(tm_p))
    (cq_s, ck32_s, cv32_s, ckbf_s, cvbf_s, dq_s, dk32_s, dv32_s, dkbf_s, dvbf_s, lf_s) = _inproj_cd(
        xs, gvec(1, 0), mod_s[0], mod_s[1], w_cd, rope_s, fox_b, 1)

    oc_p = _diff_prompt(lamv, gsub, r3(cq_p), r3(ckbf_p), r3(cvbf_p), lam_init)
    cum_t = _cumsum_lanes(lf_p[:, :D_HEADS])
    od_p = _fox_prompt(r3(dq_p), r3(dkbf_p), r3(dvbf_p), cum_t.transpose(0, 2, 1), cum_t)

    oc_s = _diff_sample(pt, lamv, gsub, _group_major(cq_s, bd, t_new, 2 * C_HEADS, HEAD_DIM),
                        _group_minor(ckbf_s, bd, t_new, 2 * C_HEADS, HEAD_DIM),
                        _group_major(cvbf_s, bd, t_new, C_HEADS, 2 * HEAD_DIM, PAGE_SIZE),
                        _position_minor(cache_c_k[li]),
                        cache_c_v[li].reshape(-1, PAGE_SIZE * C_HEADS, 2 * HEAD_DIM), lam_init)

    d_logf_s = _cache_out(lf_s[:, :D_HEADS], gs, (D_HEADS,))
    logf_s = d_logf_s[0]
    lf_t = jnp.pad(logf_s.transpose(0, 2, 1), ((0, 0), (0, 0), (0, LANES - t_new)))
    od_s = _fox_sample(pt, _group_major(dq_s, bd, t_new, D_HEADS, HEAD_DIM), lf_t,
                       _group_minor(dkbf_s, bd, t_new, D_HEADS, HEAD_DIM),
                       _group_minor(dvbf_s, bd, t_new, D_HEADS, HEAD_DIM),
                       _position_minor(cache_d_k[li]), _position_minor(cache_d_v[li]),
                       _position_minor(cache_d_logf[li]))

    xp = _outproj(oc_p.reshape(mp, 512), od_p.reshape(mp, 512), wo_c, wo_d, xp, mod_p[2], gvec(1, 1), tps_p)
    xs = _outproj(oc_s.reshape(ms, 512).astype(BF16), od_s.reshape(ms, 512).astype(BF16), wo_c, wo_d, xs,
                  mod_s[2], gvec(1, 1), tps_s)
    wg, wu, wd = w_gate[1].astype(BF16), w_up[1].astype(BF16), w_down[1].astype(BF16)
    xp = _ffn(xp, gvec(1, 2), mod_p[3], mod_p[4], wg, wu, wd, mod_p[5], gvec(1, 3), tps_p)
    xs = _ffn(xs, gvec(1, 2), mod_s[3], mod_s[4], wg, wu, wd, mod_s[5], gvec(1, 3), tps_s)

    return (xp.reshape(b, s_len, D_MODEL), xs.reshape(bd, t_new, D_MODEL),
            a_k_p, a_v_p, a_idx_p,
            _cache_out(ck32_p, gp, (C_HEADS, 2, HEAD_DIM)), cv32_p.reshape((1,) + gp + (C_HEADS, 2 * HEAD_DIM)),
            _cache_out(dk32_p, gp, (D_HEADS, HEAD_DIM)), _cache_out(dv32_p, gp, (D_HEADS, HEAD_DIM)),
            _cache_out(lf_p[:, :D_HEADS], gp, (D_HEADS,)),
            a_k_s, a_v_s, a_idx_s,
            _cache_out(ck32_s, gs, (C_HEADS, 2, HEAD_DIM)), cv32_s.reshape((1,) + gs + (C_HEADS, 2 * HEAD_DIM)),
            _cache_out(dk32_s, gs, (D_HEADS, HEAD_DIM)), _cache_out(dv32_s, gs, (D_HEADS, HEAD_DIM)),
            d_logf_s, gm_v_s)
```

```python
import functools
import math

import numpy as np
import jax
import jax.numpy as jnp
from jax import lax
from jax.experimental import pallas as pl
from jax.experimental.pallas import tpu as pltpu

F32 = jnp.float32
BF16 = jnp.bfloat16
NEG_INF = float("-inf")

D_MODEL = 1024
HEAD_DIM = 64
ROPE_THETA = 500000.0
A_HEADS = 8
A_KV_HEADS = 2
IDX_HEADS = 8
IDX_DIM = 64
TOPK_MAX = 256
GM_GROUPS = 8
GM_WIDTH = GM_GROUPS * HEAD_DIM
CHUNK = 128
C_HEADS = 4
D_HEADS = 8
PAGE_SIZE = 128
EPS = 1e-6
D_FF = -(-(8 * D_MODEL) // (3 * 256)) * 256

LANES = 128
VMEM_LIMIT_BYTES = 56 * 1024 * 1024

ROW_TILE = 256
FFN_ROW_TILE = 512
Q_TILE = 256
CAUSAL_SEGMENTS = 8
SELECT_PAGES_PER_STEP = 32
DSA_PAGES_PER_STEP = 16
CD_PAGES_PER_STEP = 16
MOD_COL_TILE = 1536
BISECT_STEPS = 20

AB_Q, AB_K, AB_V, AB_QI, AB_KIW, AB_U, AB_GV, AB_COLS = 0, 512, 640, 768, 1280, 1408, 1920, 2432
CD_CQ, CD_CK, CD_CV, CD_DQ, CD_DK, CD_DV, CD_DF, CD_COLS = 0, 512, 1024, 1536, 2048, 2560, 3072, 3200
A_HEAD_PERM = (0, 4, 1, 5, 2, 6, 3, 7)


def _cparams(*sem):
    return pltpu.CompilerParams(dimension_semantics=sem, vmem_limit_bytes=VMEM_LIMIT_BYTES)


def _nt(a, b):
    return lax.dot_general(a, b, (((1,), (1,)), ((), ())), preferred_element_type=F32)


def _mm(a, b):
    return jnp.dot(a, b, preferred_element_type=F32)


def _rms(x, g):
    return x * lax.rsqrt(jnp.mean(x * x, axis=-1, keepdims=True) + EPS) * g


def _silu(x):
    return x / (1.0 + jnp.exp(-x))


def _gelu_tanh(x):
    return 0.5 * x * (1.0 + jnp.tanh(math.sqrt(2.0 / math.pi) * (x + 0.044715 * (x * x * x))))


def _half_masks(dtype):
    lane = lax.broadcasted_iota(jnp.int32, (1, LANES), 1)
    lo = (lane < HEAD_DIM).astype(dtype)
    return lo, (1 - lo).astype(dtype)


def _rope(z, c, s1, s2):
    outs = []
    for j in range(z.shape[1] // LANES):
        x = z[:, j * LANES:(j + 1) * LANES]
        outs.append(x * c + pltpu.roll(x, LANES - 8, 1) * s1 + pltpu.roll(x, 8, 1) * s2)
    return outs[0] if len(outs) == 1 else jnp.concatenate(outs, axis=1)


def _split3(x):
    hi = x.astype(BF16)
    r1 = x - hi.astype(F32)
    mid = r1.astype(BF16)
    lo = (r1 - mid.astype(F32)).astype(BF16)
    return hi, mid, lo


def _lane_prefix(x, tri):
    hi, mid, lo = _split3(x)
    r = _mm(jnp.concatenate([hi, mid, lo], axis=0), tri)
    return r[0:8] + r[8:16] + r[16:24]


def _tri(incl_upper=None, strict_lower=None):
    r = lax.broadcasted_iota(jnp.int32, (LANES, LANES), 0)
    c = lax.broadcasted_iota(jnp.int32, (LANES, LANES), 1)
    if incl_upper:
        return (r <= c).astype(BF16)
    return (r > c).astype(BF16)


def _count_ge(sm, x):
    return jnp.sum(jnp.where(sm >= x, 1.0, 0.0), axis=1, keepdims=True)


def _topk_bias(sm_ref, bias_ref, n_sel):
    kf = float(n_sel)
    sm = sm_ref[...]
    allowed = sm > NEG_INF
    mx = jnp.max(sm, axis=1, keepdims=True)
    mn = jnp.min(jnp.where(allowed, sm, jnp.inf), axis=1, keepdims=True)
    n_allowed = jnp.sum(jnp.where(allowed, 1.0, 0.0), axis=1, keepdims=True)
    few = n_allowed <= kf
    top_tie = _count_ge(sm, mx) >= kf

    def bisect(_, c):
        lo, hi = c
        mid = lo + (hi - lo) * 0.5
        ge = _count_ge(sm_ref[...], mid) >= kf
        return jnp.where(ge, mid, lo), jnp.where(ge, hi, mid)

    _, hi = lax.fori_loop(0, BISECT_STEPS, bisect, (mn, mx))

    done0 = jnp.where(few, 1.0, jnp.where(top_tie, 1.0, 0.0))
    thr0 = jnp.where(few, NEG_INF, mx)

    def not_done(c):
        return c[3] > 0.0

    def step(c):
        hi, thr, done, _ = c
        s = sm_ref[...]
        m = jnp.max(jnp.where(s < hi, s, NEG_INF), axis=1, keepdims=True)
        found = _count_ge(s, m) >= kf
        is_done = done > 0.5
        thr = jnp.where(is_done, thr, jnp.where(found, m, thr))
        hi = jnp.where(is_done, hi, jnp.where(found, hi, m))
        done = jnp.where(found, 1.0, done)
        return hi, thr, done, jnp.sum(1.0 - done)

    _, thr, _, _ = lax.while_loop(not_done, step, (hi, thr0, done0, jnp.sum(1.0 - done0)))

    is_gt = sm > thr
    is_eq = jnp.where(sm == thr, jnp.where(allowed, 1.0, 0.0), 0.0)
    need = kf - jnp.sum(jnp.where(is_gt, 1.0, 0.0), axis=1, keepdims=True)
    n_eq = jnp.sum(is_eq, axis=1, keepdims=True)
    bias_ref[...] = jnp.where(sm >= thr, jnp.where(allowed, 0.0, NEG_INF), NEG_INF)

    @pl.when(jnp.max(n_eq - need) > 0.0)
    def _():
        tri = _tri(incl_upper=True)

        def blk(j, run):
            cols = pl.ds(pl.multiple_of(j * LANES, LANES), LANES)
            s = sm_ref[:, cols]
            e = jnp.where(s == thr, jnp.where(s > NEG_INF, 1.0, 0.0), 0.0)
            rank = run + _mm(e.astype(BF16), tri)
            keep = jnp.where(e > 0.5, jnp.where(rank <= need, 1.0, 0.0), 0.0)
            bias_ref[:, cols] = jnp.where(s > thr, 0.0, jnp.where(keep > 0.5, 0.0, NEG_INF))
            return run + jnp.sum(e, axis=1, keepdims=True)

        lax.fori_loop(0, sm_ref.shape[1] // LANES, blk, jnp.zeros_like(need))


def _mod_kernel(c_ref, w_ref, b_ref, o_ref):
    a = _silu(c_ref[...]).astype(BF16)
    o_ref[...] = _mm(a, w_ref[...].astype(BF16)) + b_ref[...]


def _modulation(c_all, w_mod, b_mod):
    n_layers, _, n_out = w_mod.shape
    rows = c_all.shape[0]
    tn = MOD_COL_TILE
    return pl.pallas_call(
        _mod_kernel,
        grid=(n_layers, n_out // tn),
        in_specs=[
            pl.BlockSpec((rows, D_MODEL), lambda l, n: (0, 0)),
            pl.BlockSpec((None, D_MODEL, tn), lambda l, n: (l, 0, n)),
            pl.BlockSpec((None, 1, tn), lambda l, n: (l, 0, n)),
        ],
        out_specs=pl.BlockSpec((None, rows, tn), lambda l, n: (l, 0, n)),
        out_shape=jax.ShapeDtypeStruct((n_layers, rows, n_out), F32),
        compiler_params=_cparams("arbitrary", "arbitrary"),
        name="modulation",
    )(c_all, w_mod, b_mod.reshape(n_layers, 1, n_out))


def _rowvec_spec(vec, tm, tiles_per_seq):
    if vec.ndim == 3:
        return pl.BlockSpec((None, 1, vec.shape[-1]), lambda i: (i // tiles_per_seq, 0, 0))
    return pl.BlockSpec((tm, vec.shape[-1]), lambda i: (i, 0))


def _const_spec(arr, single_buffer=False):
    nd = arr.ndim
    if single_buffer:
        return pl.BlockSpec(arr.shape, lambda i: (0,) * nd, pipeline_mode=pl.Buffered(1))
    return pl.BlockSpec(arr.shape, lambda i: (0,) * nd)


def _position_minor_out(m, tm, tiles_per_seq):
    seq_len = tm * tiles_per_seq
    spec = lambda width: pl.BlockSpec((None, width, tm), lambda i: (i // tiles_per_seq, 0, i % tiles_per_seq))
    shape = lambda width: jax.ShapeDtypeStruct((m // seq_len, width, seq_len), F32)
    return spec, shape


def _inproj_ab_kernel(x_ref, g_ref, sh_ref, sc_ref, w_ref, rc_ref, rs1_ref, rs2_ref, gmg_ref, wsp_ref, bsp_ref,
                      q_ref, k32_ref, v32_ref, kiw_ref, kbf_ref, vbf_ref, qi_ref, kid_ref, ob_ref, *gv_ref):
    h = (_rms(x_ref[...], g_ref[...]) * (1.0 + sc_ref[...]) + sh_ref[...]).astype(BF16)
    c, s1, s2 = rc_ref[...], rs1_ref[...], rs2_ref[...]

    def seg(lo, hi):
        return _mm(h, w_ref[:, lo:hi])

    q_ref[...] = (_rope(seg(AB_Q, AB_K), c, s1, s2) * HEAD_DIM ** -0.5).astype(BF16)
    k = _rope(seg(AB_K, AB_V), c, s1, s2)
    k32_ref[...] = k.T
    kbf_ref[...] = k.astype(BF16)
    v = seg(AB_V, AB_QI)
    v32_ref[...] = v.T
    vbf_ref[...] = v.astype(BF16)
    qi_ref[...] = _rope(seg(AB_QI, AB_KIW), c, s1, s2).astype(BF16)
    kz = seg(AB_KIW, AB_U)
    kr = _rope(kz, c, s1, s2)
    lane = lax.broadcasted_iota(jnp.int32, kz.shape, 1)
    kiw_ref[...] = jnp.where(lane < IDX_DIM, kr, kz).T
    kid_ref[...] = jnp.where(lane < IDX_DIM, kr, pltpu.roll(kr, IDX_DIM, 1)).astype(BF16)

    u = _gelu_tanh(seg(AB_U, AB_GV))
    gv = _gelu_tanh(seg(AB_GV, AB_COLS))
    gc = gv - jnp.mean(gv, axis=-1, keepdims=True)
    gvn = gc * lax.rsqrt(jnp.mean(gc * gc, axis=-1, keepdims=True) + EPS) * gmg_ref[...]
    if gv_ref:
        gv_ref[0][...] = gvn
    gvb = gvn.astype(BF16)
    lane = lax.broadcasted_iota(jnp.int32, (1, LANES), 1)
    mix = []
    for j in range(GM_GROUPS // 2):
        slab = gvb[:, j * LANES:(j + 1) * LANES]
        mix.append(jnp.where(lane < HEAD_DIM, _mm(wsp_ref[2 * j], slab), _mm(wsp_ref[2 * j + 1], slab)))
    ob_ref[...] = (u * (jnp.concatenate(mix, axis=1) + bsp_ref[...])).astype(BF16)


def _inproj_ab(x, g, shift, scale, w, rope_tabs, gm_g, wsp, bsp, tiles_per_seq, want_gv):
    m = x.shape[0]
    tm = min(ROW_TILE, m)
    n_tab_tiles = rope_tabs[0].shape[0] // tm
    row = lambda width: pl.BlockSpec((tm, width), lambda i: (i, 0))
    tab = pl.BlockSpec((tm, LANES), lambda i: (i % n_tab_tiles, 0))
    sds = lambda width, dt: jax.ShapeDtypeStruct((m, width), dt)
    col, col_sds = _position_minor_out(m, tm, tiles_per_seq)
    out_shape = [sds(512, BF16), col_sds(128), col_sds(128), col_sds(128), sds(128, BF16), sds(128, BF16),
                 sds(512, BF16), sds(128, BF16), sds(512, BF16)]
    out_specs = [row(512), col(128), col(128), col(128), row(128), row(128), row(512), row(128), row(512)]
    if want_gv:
        out_shape.append(sds(512, F32))
        out_specs.append(row(512))
    return pl.pallas_call(
        _inproj_ab_kernel,
        grid=(m // tm,),
        in_specs=[row(D_MODEL), _const_spec(g), _rowvec_spec(shift, tm, tiles_per_seq),
                  _rowvec_spec(scale, tm, tiles_per_seq), _const_spec(w), tab, tab, tab,
                  _const_spec(gm_g), _const_spec(wsp), _const_spec(bsp)],
        out_specs=out_specs,
        out_shape=out_shape,
        compiler_params=_cparams("parallel"),
        name="inproj_ab",
    )(x, g, shift, scale, w, *rope_tabs, gm_g, wsp, bsp)


def _inproj_cd_kernel(x_ref, g_ref, sh_ref, sc_ref, w_ref, rc_ref, rs1_ref, rs2_ref, fb_ref,
                      cq_ref, ck32_ref, cv32_ref, ckbf_ref, cvbf_ref,
                      dq_ref, dk32_ref, dv32_ref, dkbf_ref, dvbf_ref, lf_ref):
    h = (_rms(x_ref[...], g_ref[...]) * (1.0 + sc_ref[...]) + sh_ref[...]).astype(BF16)
    c, s1, s2 = rc_ref[...], rs1_ref[...], rs2_ref[...]

    def seg(lo, hi):
        return _mm(h, w_ref[:, lo:hi])

    cq_ref[...] = (_rope(seg(CD_CQ, CD_CK), c, s1, s2) * HEAD_DIM ** -0.5).astype(BF16)
    ck = _rope(seg(CD_CK, CD_CV), c, s1, s2)
    ck32_ref[...] = ck.T
    ckbf_ref[...] = ck.astype(BF16)
    cv = seg(CD_CV, CD_DQ)
    cv32_ref[...] = cv
    cvbf_ref[...] = cv.astype(BF16)
    dq_ref[...] = (seg(CD_DQ, CD_DK) * HEAD_DIM ** -0.5).astype(BF16)
    dk = seg(CD_DK, CD_DV)
    dk32_ref[...] = dk.T
    dkbf_ref[...] = dk.astype(BF16)
    dv = seg(CD_DV, CD_DF)
    dv32_ref[...] = dv.T
    dvbf_ref[...] = dv.astype(BF16)
    f = seg(CD_DF, CD_COLS) + fb_ref[...]
    lf_ref[...] = (jnp.minimum(f, 0.0) - jnp.log(1.0 + jnp.exp(-jnp.abs(f)))).T


def _inproj_cd(x, g, shift, scale, w, rope_tabs, fox_b, tiles_per_seq):
    m = x.shape[0]
    tm = min(ROW_TILE, m)
    n_tab_tiles = rope_tabs[0].shape[0] // tm
    row = lambda width: pl.BlockSpec((tm, width), lambda i: (i, 0))
    tab = pl.BlockSpec((tm, LANES), lambda i: (i % n_tab_tiles, 0))
    sds = lambda width, dt: jax.ShapeDtypeStruct((m, width), dt)
    col, col_sds = _position_minor_out(m, tm, tiles_per_seq)
    return pl.pallas_call(
        _inproj_cd_kernel,
        grid=(m // tm,),
        in_specs=[row(D_MODEL), _const_spec(g), _rowvec_spec(shift, tm, tiles_per_seq),
                  _rowvec_spec(scale, tm, tiles_per_seq), _const_spec(w), tab, tab, tab, _const_spec(fox_b)],
        out_specs=[row(512), col(512), row(512), row(512), row(512),
                   row(512), col(512), col(512), row(512), row(512), col(128)],
        out_shape=[sds(512, BF16), col_sds(512), sds(512, F32), sds(512, BF16), sds(512, BF16),
                   sds(512, BF16), col_sds(512), col_sds(512), sds(512, BF16), sds(512, BF16), col_sds(128)],
        compiler_params=_cparams("parallel"),
        name="inproj_cd",
    )(x, g, shift, scale, w, *rope_tabs, fox_b)


def _mixer_out_ffn_kernel(a_ref, b_ref, wa_ref, wb_ref, x_ref, gate1_ref, g1_ref,
                          g2_ref, sh_ref, sc_ref, wg_ref, wu_ref, wd_ref, gate2_ref, g3_ref, o_ref):
    op = _mm(a_ref[...], wa_ref[...]) + _mm(b_ref[...], wb_ref[...])
    x = x_ref[...] + gate1_ref[...] * _rms(op, g1_ref[...])
    h = (_rms(x, g2_ref[...]) * (1.0 + sc_ref[...]) + sh_ref[...]).astype(BF16)
    act = (_silu(_mm(h, wg_ref[...])) * _mm(h, wu_ref[...])).astype(BF16)
    o_ref[...] = x + gate2_ref[...] * _rms(_mm(act, wd_ref[...]), g3_ref[...])


def _mixer_out_ffn(a, b, wa, wb, x, gate1, g1, g2, shift, scale, wg, wu, wd, gate2, g3, tiles_per_seq_of):
    m = x.shape[0]
    tm = min(FFN_ROW_TILE, m)
    tps = tiles_per_seq_of(tm)
    row = lambda width: pl.BlockSpec((tm, width), lambda i: (i, 0))
    vec = lambda v: _rowvec_spec(v, tm, tps)
    once = lambda arr: _const_spec(arr, single_buffer=True)
    return pl.pallas_call(
        _mixer_out_ffn_kernel,
        grid=(m // tm,),
        in_specs=[row(a.shape[1]), row(b.shape[1]), once(wa), once(wb), row(D_MODEL), vec(gate1), _const_spec(g1),
                  _const_spec(g2), vec(shift), vec(scale), once(wg), once(wu), once(wd), vec(gate2), _const_spec(g3)],
        out_specs=row(D_MODEL),
        out_shape=jax.ShapeDtypeStruct((m, D_MODEL), F32),
        compiler_params=_cparams("parallel"),
        name="mixer_out_ffn",
    )(a, b, wa, wb, x, gate1, g1, g2, shift, scale, wg, wu, wd, gate2, g3)


def _dsa_prompt_kernel(qi_ref, wi_ref, kid_ref, q_ref, k_ref, v_ref, o_ref, sm_ref, bias_ref, *, n_sel, q_blk0):
    tq, s_len = sm_ref.shape
    i = q_blk0 + pl.program_id(1)
    m_lo, m_hi = _half_masks(BF16)
    kid = kid_ref[...]
    score = jnp.zeros((tq, s_len), F32)
    for j in range(IDX_HEADS // 2):
        slab = qi_ref[:, j * LANES:(j + 1) * LANES]
        s = _nt(jnp.concatenate([slab * m_lo, slab * m_hi], axis=0), kid)
        score = score + jnp.maximum(s[:tq], 0.0) * wi_ref[:, 2 * j:2 * j + 1]
        score = score + jnp.maximum(s[tq:], 0.0) * wi_ref[:, 2 * j + 1:2 * j + 2]
    score = score * (IDX_HEADS * IDX_DIM) ** -0.5
    row = i * tq + lax.broadcasted_iota(jnp.int32, (tq, s_len), 0)
    col = lax.broadcasted_iota(jnp.int32, (tq, s_len), 1)
    sm_ref[...] = jnp.where(col <= row, score, NEG_INF)
    _topk_bias(sm_ref, bias_ref, n_sel)

    k = k_ref[...]
    v = v_ref[...]
    n_slab = A_HEADS // 2
    outs = []
    for mask in (m_lo, m_hi):
        stk = jnp.concatenate([q_ref[:, j * LANES:(j + 1) * LANES] * mask for j in range(n_slab)], axis=0)
        lg = _nt(stk, k)
        per = []
        for j in range(n_slab):
            l = lg[j * tq:(j + 1) * tq] + bias_ref[...]
            p = jnp.exp(l - jnp.max(l, axis=1, keepdims=True))
            per.append(_mm(p.astype(BF16), v) / jnp.sum(p, axis=1, keepdims=True))
        outs.append(per)
    lane = lax.broadcasted_iota(jnp.int32, (1, LANES), 1)
    for j in range(n_slab):
        o_ref[:, j * LANES:(j + 1) * LANES] = jnp.where(lane < HEAD_DIM, outs[0][j], outs[1][j]).astype(BF16)


def _causal_segments(s_len, tq):
    n_seg = min(CAUSAL_SEGMENTS, s_len // tq)
    per = s_len // tq // n_seg
    return [(seg * per, per, (seg + 1) * per * tq) for seg in range(n_seg)]


def _dsa_prompt(qi, wi, kid, q, k, v, n_sel):
    b, s_len, _ = q.shape
    tq = min(Q_TILE, s_len)
    outs = []
    for q_blk0, n_blk, kext in _causal_segments(s_len, tq):
        qblk = lambda width: pl.BlockSpec((None, tq, width), lambda bb, i: (bb, q_blk0 + i, 0))
        keys = lambda width: pl.BlockSpec((None, kext, width), lambda bb, i: (bb, 0, 0))
        outs.append(pl.pallas_call(
            functools.partial(_dsa_prompt_kernel, n_sel=n_sel, q_blk0=q_blk0),
            grid=(b, n_blk),
            in_specs=[qblk(512), qblk(IDX_HEADS), keys(128), qblk(512), keys(128), keys(128)],
            out_specs=pl.BlockSpec((None, tq, 512), lambda bb, i: (bb, i, 0)),
            out_shape=jax.ShapeDtypeStruct((b, n_blk * tq, 512), BF16),
            scratch_shapes=[pltpu.VMEM((tq, kext), F32), pltpu.VMEM((tq, kext), F32)],
            compiler_params=_cparams("parallel", "arbitrary"),
            name="dsa_prompt",
        )(qi, wi, kid, q, k, v))
    return jnp.concatenate(outs, axis=1)


def _diff_lambda(lamv_ref, lam_init):
    lv = lamv_ref[...]
    d1 = jnp.sum(lv[0:1] * lv[1:2], axis=1, keepdims=True)
    d2 = jnp.sum(lv[2:3] * lv[3:4], axis=1, keepdims=True)
    return jnp.exp(d1) - jnp.exp(d2) + lam_init


def _causal_softmax_pv(lg, v, row0, tq, seg_rows):
    r, kext = lg.shape
    split = kext - seg_rows
    row = row0 + lax.broadcasted_iota(jnp.int32, (r, seg_rows), 0) % tq
    col = split + lax.broadcasted_iota(jnp.int32, (r, seg_rows), 1)
    hi = jnp.where(col <= row, lg[:, split:], NEG_INF)
    m = jnp.max(hi, axis=1, keepdims=True)
    if split:
        lo = lg[:, :split]
        m = jnp.maximum(m, jnp.max(lo, axis=1, keepdims=True))
    p = jnp.exp(hi - m)
    den = jnp.sum(p, axis=1, keepdims=True)
    pv = _mm(p.astype(BF16), v[split:])
    if split:
        p = jnp.exp(lo - m)
        den = den + jnp.sum(p, axis=1, keepdims=True)
        pv = pv + _mm(p.astype(BF16), v[:split])
    return pv / den


def _diff_prompt_kernel(lamv_ref, gsub_ref, q_ref, k_ref, v_ref, o_ref, *, lam_init, q_blk0, seg_rows):
    tq = q_ref.shape[0]
    i = q_blk0 + pl.program_id(1)
    lam = _diff_lambda(lamv_ref, lam_init)
    m_lo, m_hi = _half_masks(BF16)
    for h in range(C_HEADS):
        cols = slice(h * LANES, (h + 1) * LANES)
        slab = q_ref[:, cols]
        lg = _nt(jnp.concatenate([slab * m_lo, slab * m_hi], axis=0), k_ref[:, cols])
        pv = _causal_softmax_pv(lg, v_ref[:, cols], i * tq, tq, seg_rows)
        o = pv[:tq] - lam * pv[tq:]
        o_ref[:, cols] = (_rms(o, gsub_ref[...]) * (1.0 - lam_init)).astype(BF16)


def _diff_prompt(lamv, gsub, q, k, v, lam_init):
    b, s_len, _ = q.shape
    tq = min(Q_TILE, s_len)
    cst = lambda arr: pl.BlockSpec(arr.shape, lambda bb, i: (0, 0))
    outs = []
    for q_blk0, n_blk, kext in _causal_segments(s_len, tq):
        qblk = pl.BlockSpec((None, tq, 512), lambda bb, i: (bb, q_blk0 + i, 0))
        keys = pl.BlockSpec((None, kext, 512), lambda bb, i: (bb, 0, 0))
        outs.append(pl.pallas_call(
            functools.partial(_diff_prompt_kernel, lam_init=lam_init, q_blk0=q_blk0, seg_rows=n_blk * tq),
            grid=(b, n_blk),
            in_specs=[cst(lamv), cst(gsub), qblk, keys, keys],
            out_specs=pl.BlockSpec((None, tq, 512), lambda bb, i: (bb, i, 0)),
            out_shape=jax.ShapeDtypeStruct((b, n_blk * tq, 512), BF16),
            compiler_params=_cparams("parallel", "arbitrary"),
            name="diff_prompt",
        )(lamv, gsub, q, k, v))
    return jnp.concatenate(outs, axis=1)


def _cumsum_kernel(x_ref, o_ref):
    tri = _tri(incl_upper=True)
    carry = jnp.zeros((8, 1), F32)
    for c in range(x_ref.shape[1] // LANES):
        cols = slice(c * LANES, (c + 1) * LANES)
        out = _lane_prefix(x_ref[:, cols], tri) + carry
        o_ref[:, cols] = out
        carry = out[:, LANES - 1:LANES]


def _cumsum_lanes(x):
    b, h, s_len = x.shape
    blk = pl.BlockSpec((None, h, s_len), lambda bb: (bb, 0, 0))
    return pl.pallas_call(
        _cumsum_kernel,
        grid=(b,),
        in_specs=[blk],
        out_specs=blk,
        out_shape=jax.ShapeDtypeStruct(x.shape, F32),
        compiler_params=_cparams("parallel"),
        name="fox_cumsum",
    )(x)


def _fox_prompt_kernel(q_ref, k_ref, v_ref, cq_ref, ck_ref, o_ref, *, q_blk0, seg_rows):
    tq = q_ref.shape[0]
    i = q_blk0 + pl.program_id(1)
    m_lo, m_hi = _half_masks(BF16)
    lane = lax.broadcasted_iota(jnp.int32, (1, LANES), 1)
    for j in range(D_HEADS // 2):
        cols = slice(j * LANES, (j + 1) * LANES)
        slab = q_ref[:, cols]
        lg = _nt(jnp.concatenate([slab * m_lo, slab * m_hi], axis=0), k_ref[:, cols])
        v = v_ref[:, cols]
        pv = []
        for hh in range(2):
            h = 2 * j + hh
            l = lg[hh * tq:(hh + 1) * tq] + cq_ref[:, h:h + 1] - ck_ref[h:h + 1, :]
            pv.append(_causal_softmax_pv(l, v, i * tq, tq, seg_rows))
        o_ref[:, cols] = jnp.where(lane < HEAD_DIM, pv[0], pv[1]).astype(BF16)


def _fox_prompt(q, k, v, cum, cum_t):
    b, s_len, _ = q.shape
    tq = min(Q_TILE, s_len)
    outs = []
    for q_blk0, n_blk, kext in _causal_segments(s_len, tq):
        qblk = lambda width: pl.BlockSpec((None, tq, width), lambda bb, i: (bb, q_blk0 + i, 0))
        keys = pl.BlockSpec((None, kext, 512), lambda bb, i: (bb, 0, 0))
        outs.append(pl.pallas_call(
            functools.partial(_fox_prompt_kernel, q_blk0=q_blk0, seg_rows=n_blk * tq),
            grid=(b, n_blk),
            in_specs=[qblk(512), keys, keys, qblk(D_HEADS),
                      pl.BlockSpec((None, D_HEADS, kext), lambda bb, i: (bb, 0, 0))],
            out_specs=pl.BlockSpec((None, tq, 512), lambda bb, i: (bb, i, 0)),
            out_shape=jax.ShapeDtypeStruct((b, n_blk * tq, 512), BF16),
            compiler_params=_cparams("parallel", "arbitrary"),
            name="fox_prompt",
        )(q, k, v, cum, cum_t))
    return jnp.concatenate(outs, axis=1)


def _pages_per_step(n_pages, cap):
    return max(d for d in range(1, min(cap, n_pages) + 1) if n_pages % d == 0)


def _page_specs(block, n_pg, reverse_from=None):
    specs = []
    for pg in range(n_pg):
        if reverse_from is None:
            idx = lambda b, st, pt, pg=pg: (pt[b, st * n_pg + pg],) + (0,) * (len(block) - 1)
        else:
            idx = lambda b, st, pt, pg=pg: (pt[b, reverse_from - (st * n_pg + pg)],) + (0,) * (len(block) - 1)
        specs.append(pl.BlockSpec((None,) + tuple(block[1:]), idx))
    return specs


def _per_seq(shape):
    nd = len(shape)
    return pl.BlockSpec((None,) + tuple(shape[1:]), lambda b, st, pt: (b,) + (0,) * (nd - 1))


def _softmax_init(m_ref, l_ref, acc_ref):
    m_ref[...] = jnp.full(m_ref.shape, NEG_INF, F32)
    l_ref[...] = jnp.zeros(l_ref.shape, F32)
    acc_ref[...] = jnp.zeros(acc_ref.shape, F32)


def _attend(q_ref, keys_t_of, values_of, n_v, values_t, bias, m_ref, l_ref, acc_ref):
    n_k, r, _ = q_ref.shape
    pv = _nt if values_t else _mm
    lg = jnp.concatenate([_mm(q_ref[i], keys_t_of(i)) for i in range(n_k)], axis=0) + bias
    m_old = m_ref[...]
    m_new = jnp.maximum(m_old, jnp.max(lg, axis=1, keepdims=True))
    m_safe = jnp.where(m_new == NEG_INF, 0.0, m_new)
    alpha = jnp.exp(m_old - m_safe)
    p = jnp.exp(lg - m_safe)
    l_ref[...] = alpha * l_ref[...] + jnp.sum(p, axis=1, keepdims=True)
    m_ref[...] = m_new
    p = p.astype(BF16)
    rv = r * (n_k // n_v)
    for j in range(n_v):
        rows = slice(j * rv, (j + 1) * rv)
        acc_ref[rows, :] = alpha[rows] * acc_ref[rows, :] + pv(p[rows], values_of(j))


def _page_lanes(pages, *idx):
    return jnp.concatenate([(p[idx] if idx else p[...]).astype(BF16) for p in pages], axis=1)


def _rows8(x):
    return jnp.concatenate([jnp.broadcast_to(x[h:h + 1], (8, x.shape[1])) for h in range(x.shape[0])], axis=0)


def _head_sum(w, n_groups):
    out = w[0:8]
    for h in range(1, n_groups):
        out = out + w[8 * h:8 * h + 8]
    return out


def _new_key_mask(rows):
    t = lax.broadcasted_iota(jnp.int32, (rows, LANES), 0) % 8
    c = lax.broadcasted_iota(jnp.int32, (rows, LANES), 1)
    return c <= t


def _dsa_select_kernel(pt_ref, q_ref, w_ref, knew_ref, *rest, n_pg, n_sel):
    pages = rest[:n_pg]
    bias_ref, sm_ref = rest[n_pg], rest[n_pg + 1]
    st = pl.program_id(1)
    q = q_ref[...]
    wcol = w_ref[...]
    scale = (IDX_HEADS * IDX_DIM) ** -0.5

    def scores(keys_t):
        return _head_sum(jnp.maximum(_mm(q, keys_t), 0.0) * wcol, IDX_HEADS) * scale

    step_keys = n_pg * PAGE_SIZE
    col0 = pl.multiple_of(st * step_keys, step_keys)
    sm_ref[:, pl.ds(col0, step_keys)] = scores(_page_lanes(pages))

    @pl.when(st == pl.num_programs(1) - 1)
    def _():
        past = sm_ref.shape[1] - LANES
        sm_ref[:, past:] = jnp.where(_new_key_mask(8), scores(knew_ref[...]), NEG_INF)
        _topk_bias(sm_ref, bias_ref, n_sel)


def _dsa_select(page_table, q, wcol, knew, cache_idx, n_sel):
    bd, n_pages = page_table.shape
    n_pg = _pages_per_step(n_pages, SELECT_PAGES_PER_STEP)
    width = n_pages * PAGE_SIZE + LANES
    return pl.pallas_call(
        functools.partial(_dsa_select_kernel, n_pg=n_pg, n_sel=n_sel),
        grid_spec=pltpu.PrefetchScalarGridSpec(
            num_scalar_prefetch=1,
            grid=(bd, n_pages // n_pg),
            in_specs=[_per_seq(q.shape), _per_seq(wcol.shape), _per_seq(knew.shape)]
                     + _page_specs(cache_idx.shape, n_pg),
            out_specs=pl.BlockSpec((None, 8, width), lambda b, st, pt: (b, 0, 0)),
            scratch_shapes=[pltpu.VMEM((8, width), F32)],
        ),
        out_shape=jax.ShapeDtypeStruct((bd, 8, width), F32),
        compiler_params=_cparams("parallel", "arbitrary"),
        name="dsa_select",
    )(page_table, q, wcol, knew, *([cache_idx] * n_pg))


def _dsa_sample_kernel(pt_ref, q_ref, bias_ref, knew_ref, vnew_ref, *rest, n_pg):
    kpages, vpages = rest[:n_pg], rest[n_pg:2 * n_pg]
    o_ref, m_ref, l_ref, acc_ref = rest[2 * n_pg:]
    st = pl.program_id(1)

    @pl.when(st == 0)
    def _():
        _softmax_init(m_ref, l_ref, acc_ref)

    n_g, rg, _ = q_ref.shape
    n_rep = n_g * rg // 8

    def bias_rows(cols):
        bt = bias_ref[:, cols]
        return jnp.concatenate([bt] * n_rep, axis=0)

    step_keys = n_pg * PAGE_SIZE
    col0 = pl.multiple_of(st * step_keys, step_keys)
    _attend(q_ref, lambda g: _page_lanes(kpages, g), lambda g: _page_lanes(vpages, g), n_g, True,
            bias_rows(pl.ds(col0, step_keys)), m_ref, l_ref, acc_ref)

    @pl.when(st == pl.num_programs(1) - 1)
    def _():
        past = bias_ref.shape[1] - LANES
        _attend(q_ref, lambda g: knew_ref[g], lambda g: vnew_ref[g], n_g, True,
                bias_rows(slice(past, past + LANES)), m_ref, l_ref, acc_ref)
        o = acc_ref[...] / l_ref[...]
        for g in range(n_g):
            for j in range(rg // 8):
                lo = j * LANES + g * HEAD_DIM
                o_ref[:, lo:lo + HEAD_DIM] = o[g * rg + 8 * j:g * rg + 8 * j + 8]


def _dsa_sample(page_table, q, bias, knew, vnew, cache_k, cache_v):
    bd, n_pages = page_table.shape
    n_pg = _pages_per_step(n_pages, DSA_PAGES_PER_STEP)
    rows = q.shape[1] * q.shape[2]
    return pl.pallas_call(
        functools.partial(_dsa_sample_kernel, n_pg=n_pg),
        grid_spec=pltpu.PrefetchScalarGridSpec(
            num_scalar_prefetch=1,
            grid=(bd, n_pages // n_pg),
            in_specs=[_per_seq(q.shape), _per_seq(bias.shape), _per_seq(knew.shape), _per_seq(vnew.shape)]
                     + _page_specs(cache_k.shape, n_pg) + _page_specs(cache_v.shape, n_pg),
            out_specs=pl.BlockSpec((None, 8, 512), lambda b, st, pt: (b, 0, 0)),
            scratch_shapes=[pltpu.VMEM((rows, 1), F32), pltpu.VMEM((rows, 1), F32),
                            pltpu.VMEM((rows, HEAD_DIM), F32)],
        ),
        out_shape=jax.ShapeDtypeStruct((bd, 8, 512), F32),
        compiler_params=_cparams("parallel", "arbitrary"),
        name="dsa_sample",
    )(page_table, q, bias, knew, vnew, *([cache_k] * n_pg), *([cache_v] * n_pg))


def _diff_sample_kernel(pt_ref, lamv_ref, gsub_ref, q_ref, knew_ref, vnew_ref, *rest, n_pg, lam_init):
    kpages, vpages = rest[:n_pg], rest[n_pg:2 * n_pg]
    o_ref, m_ref, l_ref, acc_ref = rest[2 * n_pg:]
    st = pl.program_id(1)

    @pl.when(st == 0)
    def _():
        _softmax_init(m_ref, l_ref, acc_ref)

    n_k = q_ref.shape[0]

    def values(h):
        rows = pl.ds(h, PAGE_SIZE, stride=C_HEADS)
        return jnp.concatenate([p[rows, :].astype(BF16) for p in vpages], axis=0)

    _attend(q_ref, lambda i: _page_lanes(kpages, i // 2, i % 2), values, C_HEADS, False,
            0.0, m_ref, l_ref, acc_ref)

    @pl.when(st == pl.num_programs(1) - 1)
    def _():
        bias = jnp.where(_new_key_mask(8 * n_k), 0.0, NEG_INF)
        _attend(q_ref, lambda i: knew_ref[i], lambda h: vnew_ref[h], C_HEADS, False, bias, m_ref, l_ref, acc_ref)
        lam = _diff_lambda(lamv_ref, lam_init)
        o = acc_ref[...] / l_ref[...]
        for h in range(C_HEADS):
            oh = o[16 * h:16 * h + 8] - lam * o[16 * h + 8:16 * h + 16]
            o_ref[:, h * LANES:(h + 1) * LANES] = _rms(oh, gsub_ref[...]) * (1.0 - lam_init)


def _diff_sample(page_table, lamv, gsub, q, knew, vnew, cache_k, cache_v, lam_init):
    bd, n_pages = page_table.shape
    n_pg = _pages_per_step(n_pages, CD_PAGES_PER_STEP)
    rows = q.shape[1] * q.shape[2]
    cst = lambda arr: pl.BlockSpec(arr.shape, lambda b, st, pt: (0, 0))
    return pl.pallas_call(
        functools.partial(_diff_sample_kernel, n_pg=n_pg, lam_init=lam_init),
        grid_spec=pltpu.PrefetchScalarGridSpec(
            num_scalar_prefetch=1,
            grid=(bd, n_pages // n_pg),
            in_specs=[cst(lamv), cst(gsub), _per_seq(q.shape), _per_seq(knew.shape), _per_seq(vnew.shape)]
                     + _page_specs(cache_k.shape, n_pg) + _page_specs(cache_v.shape, n_pg),
            out_specs=pl.BlockSpec((None, 8, 512), lambda b, st, pt: (b, 0, 0)),
            scratch_shapes=[pltpu.VMEM((rows, 1), F32), pltpu.VMEM((rows, 1), F32),
                            pltpu.VMEM((rows, 2 * HEAD_DIM), F32)],
        ),
        out_shape=jax.ShapeDtypeStruct((bd, 8, 512), F32),
        compiler_params=_cparams("parallel", "arbitrary"),
        name="diff_sample",
    )(page_table, lamv, gsub, q, knew, vnew, *([cache_k] * n_pg), *([cache_v] * n_pg))


def _fox_sample_kernel(pt_ref, q_ref, lft_ref, knew_ref, vnew_ref, *rest, n_pg):
    kpages, vpages, fpages = rest[:n_pg], rest[n_pg:2 * n_pg], rest[2 * n_pg:3 * n_pg]
    o_ref, m_ref, l_ref, acc_ref, carry_ref = rest[3 * n_pg:]
    st = pl.program_id(1)

    @pl.when(st == 0)
    def _():
        _softmax_init(m_ref, l_ref, acc_ref)
        carry_ref[...] = jnp.zeros(carry_ref.shape, F32)

    n_h = q_ref.shape[0]
    cn64 = _rows8(_lane_prefix(lft_ref[...], _tri(incl_upper=True)))
    mine = lax.broadcasted_iota(jnp.int32, cn64.shape, 1) == lax.broadcasted_iota(jnp.int32, cn64.shape, 0) % 8
    cn_col = jnp.sum(jnp.where(mine, cn64, 0.0), axis=1, keepdims=True)
    tri_after = _tri(strict_lower=True)

    carry = carry_ref[...]
    sfx = []
    for pg in range(n_pg):
        x = fpages[pg][...]
        sfx.append(_lane_prefix(x, tri_after) + carry)
        carry = carry + jnp.sum(x, axis=1, keepdims=True)
    carry_ref[...] = carry
    sfx = jnp.concatenate(sfx, axis=1)
    _attend(q_ref, lambda h: _page_lanes(kpages, h), lambda h: _page_lanes(vpages, h), n_h, True,
            cn_col + _rows8(sfx), m_ref, l_ref, acc_ref)

    @pl.when(st == pl.num_programs(1) - 1)
    def _():
        bias = jnp.where(_new_key_mask(8 * n_h), cn_col - cn64, NEG_INF)
        _attend(q_ref, lambda h: knew_ref[h], lambda h: vnew_ref[h], n_h, True, bias, m_ref, l_ref, acc_ref)
        o = acc_ref[...] / l_ref[...]
        for h in range(n_h):
            o_ref[:, h * HEAD_DIM:(h + 1) * HEAD_DIM] = o[8 * h:8 * h + 8]


def _fox_sample(page_table, q, lf_t, knew, vnew, cache_k, cache_v, cache_lf_t):
    bd, n_pages = page_table.shape
    n_pg = _pages_per_step(n_pages, CD_PAGES_PER_STEP)
    rows = q.shape[1] * q.shape[2]
    last = n_pages - 1
    return pl.pallas_call(
        functools.partial(_fox_sample_kernel, n_pg=n_pg),
        grid_spec=pltpu.PrefetchScalarGridSpec(
            num_scalar_prefetch=1,
            grid=(bd, n_pages // n_pg),
            in_specs=[_per_seq(q.shape), _per_seq(lf_t.shape), _per_seq(knew.shape), _per_seq(vnew.shape)]
                     + _page_specs(cache_k.shape, n_pg, last) + _page_specs(cache_v.shape, n_pg, last)
                     + _page_specs(cache_lf_t.shape, n_pg, last),
            out_specs=pl.BlockSpec((None, 8, 512), lambda b, st, pt: (b, 0, 0)),
            scratch_shapes=[pltpu.VMEM((rows, 1), F32), pltpu.VMEM((rows, 1), F32),
                            pltpu.VMEM((rows, HEAD_DIM), F32), pltpu.VMEM((8, 1), F32)],
        ),
        out_shape=jax.ShapeDtypeStruct((bd, 8, 512), F32),
        compiler_params=_cparams("parallel", "arbitrary"),
        name="fox_sample",
    )(page_table, q, lf_t, knew, vnew, *([cache_k] * n_pg), *([cache_v] * n_pg), *([cache_lf_t] * n_pg))


def _rope_tables(pos):
    rot = HEAD_DIM // 4
    half = rot // 2
    inv = ROPE_THETA ** (-np.arange(half, dtype=np.float64) / half)
    ang = np.asarray(pos, np.float64)[:, None] * inv[None, :]
    lane = np.arange(LANES) % HEAD_DIM
    c = np.ones((len(pos), LANES), np.float64)
    s1 = np.zeros_like(c)
    s2 = np.zeros_like(c)
    first, second = lane < half, (lane >= half) & (lane < rot)
    c[:, first] = np.cos(ang)[:, lane[first]]
    c[:, second] = np.cos(ang)[:, lane[second] - half]
    s1[:, first] = -np.sin(ang)[:, lane[first]]
    s2[:, second] = np.sin(ang)[:, lane[second] - half]
    return tuple(jnp.asarray(a, F32) for a in (c, s1, s2))


def _pack_w_ab(w):
    q, k, v, qi, ki, wi, u, gv = jnp.split(w, np.cumsum(
        (512, 128, 128, 512, IDX_DIM, IDX_HEADS, GM_WIDTH, GM_WIDTH))[:-1].tolist(), axis=1)
    q = q.reshape(D_MODEL, A_HEADS, HEAD_DIM)[:, np.array(A_HEAD_PERM)].reshape(D_MODEL, 512)
    pad = jnp.zeros((D_MODEL, LANES - IDX_DIM - IDX_HEADS), w.dtype)
    return jnp.concatenate([q, k, v, qi, ki, wi, pad, u, gv], axis=1).astype(BF16)


def _pack_w_cd(w):
    pad = jnp.zeros((D_MODEL, CD_COLS - w.shape[1]), w.dtype)
    return jnp.concatenate([w, pad], axis=1).astype(BF16)


def _spatial_blockdiag(w_s, b_s, tm, rows_per_chunk):
    wt = (w_s * jnp.tril(jnp.ones((CHUNK, CHUNK), w_s.dtype)))[:, :rows_per_chunk, :rows_per_chunk]
    n_blk = tm // rows_per_chunk
    eye = jnp.eye(n_blk, dtype=w_s.dtype)
    wsp = jnp.einsum("ab,gts->gatbs", eye, wt).reshape(GM_GROUPS, tm, tm).astype(BF16)
    bias = jnp.repeat(b_s[:, :rows_per_chunk].T, HEAD_DIM, axis=1)
    return wsp, jnp.tile(bias, (n_blk, 1))


def _pad_rows(x, rows):
    return jnp.pad(x, ((0, 0), (0, rows - x.shape[1]), (0, 0)))


def _cache_out(xt, lead, tail):
    n_seq, _, length = xt.shape
    x = jnp.moveaxis(xt.reshape((n_seq,) + tail + (length,)), -1, 1)
    return x.reshape((1,) + lead + tail)


def _position_minor(cache):
    return jnp.moveaxis(cache, 1, -1)


def _group_minor(x, bd, t_new, n_groups, width):
    x = x.reshape(bd, t_new, n_groups, width).transpose(0, 2, 3, 1)
    return jnp.pad(x, ((0, 0), (0, 0), (0, 0), (0, PAGE_SIZE - t_new)))


def _group_major(x, bd, t_new, n_groups, width, pad_to=None):
    x = x.reshape(bd, t_new, n_groups, width).transpose(0, 2, 1, 3)
    if pad_to is not None:
        x = jnp.pad(x, ((0, 0), (0, 0), (0, pad_to - t_new), (0, 0)))
    return x


def kernel(x_prompt, x_sample, cache_a_k, cache_a_v, cache_a_idx, cache_c_k, cache_c_v, cache_d_k, cache_d_v,
           cache_d_logf, page_table, c_prompt, c_sample, w_mod, b_mod, g_norm, w_in_ab, w_out_ab, gm_norm_v,
           gm_spatial_w, gm_spatial_b, w_in_cd, w_out_cd, fox_bias, lam_q1, lam_k1, lam_q2, lam_k2, g_subln,
           w_gate, w_up, w_down):
    b, s_len, _ = x_prompt.shape
    bd, t_new, _ = x_sample.shape
    n_pages = page_table.shape[1]
    past_len = n_pages * PAGE_SIZE
    n_pool = cache_a_k.shape[1]
    assert t_new == 8 and s_len % CHUNK == 0 and (bd * t_new) % 8 == 0
    mp, ms = b * s_len, bd * t_new

    mod = _modulation(jnp.concatenate([c_prompt, c_sample], axis=0), w_mod, b_mod)
    mod = mod.reshape(mod.shape[0], b + bd, 6, D_MODEL)

    def mods(layer):
        p = [mod[layer, :b, i].reshape(b, 1, D_MODEL) for i in range(6)]
        s = [jnp.repeat(mod[layer, b:, i], t_new, axis=0) for i in range(6)]
        return p, s

    tps_p = lambda tm: s_len // tm
    tps_s = lambda tm: 1
    rope_p = _rope_tables(np.arange(s_len))
    rope_s = _rope_tables(np.tile(past_len + np.arange(t_new), bd))
    gvec = lambda layer, i: g_norm[layer, i].reshape(1, D_MODEL)

    xp = x_prompt.reshape(mp, D_MODEL)
    xs = x_sample.reshape(ms, D_MODEL)
    pt = page_table.astype(jnp.int32)

    li = 0
    mod_p, mod_s = mods(0)
    w_ab = _pack_w_ab(w_in_ab[li])
    gm_g = gm_norm_v[li].reshape(1, GM_WIDTH)
    tm_p, tm_s = min(ROW_TILE, mp), min(ROW_TILE, ms)
    wsp_p, bsp_p = _spatial_blockdiag(gm_spatial_w[li], gm_spatial_b[li], tm_p, CHUNK)
    wsp_s, bsp_s = _spatial_blockdiag(gm_spatial_w[li], gm_spatial_b[li], tm_s, t_new)
    wo = w_out_ab[li]
    wo_a = wo[:512].reshape(A_HEADS, HEAD_DIM, D_MODEL)[np.array(A_HEAD_PERM)].reshape(512, D_MODEL).astype(BF16)
    wo_b = wo[512:].astype(BF16)
    n_sel_p = min(TOPK_MAX, s_len // 4)
    n_sel_s = min(TOPK_MAX, (past_len + t_new) // 4)

    (q_p, k32_p, v32_p, kiw_p, kbf_p, vbf_p, qi_p, kid_p, ob_p) = _inproj_ab(
        xp, gvec(0, 0), mod_p[0], mod_p[1], w_ab, rope_p, gm_g, wsp_p, bsp_p, tps_p(tm_p), False)
    (q_s, k32_s, v32_s, kiw_s, kbf_s, vbf_s, qi_s, kid_s, ob_s, gv_s) = _inproj_ab(
        xs, gvec(0, 0), mod_s[0], mod_s[1], w_ab, rope_s, gm_g, wsp_s, bsp_s, 1, True)

    r3 = lambda a: a.reshape(b, s_len, a.shape[-1])
    wi_rows = lambda kiw: kiw[:, IDX_DIM:IDX_DIM + IDX_HEADS].transpose(0, 2, 1).reshape(-1, IDX_HEADS)
    oa_p = _dsa_prompt(r3(qi_p), r3(wi_rows(kiw_p)), r3(kid_p), r3(q_p), r3(kbf_p), r3(vbf_p), n_sel_p)

    qi_stk = qi_s.reshape(bd, t_new, IDX_HEADS, IDX_DIM).transpose(0, 2, 1, 3).reshape(bd, 64, IDX_DIM)
    wi_col = wi_rows(kiw_s).reshape(bd, t_new, IDX_HEADS).transpose(0, 2, 1).reshape(bd, 64, 1)
    ki_new = _group_minor(kid_s[:, :IDX_DIM], bd, t_new, 1, IDX_DIM)[:, 0]
    sel_bias = _dsa_select(pt, qi_stk, wi_col, ki_new, _position_minor(cache_a_idx[li]), n_sel_s)

    q_stk = q_s.reshape(bd, t_new, 4, 2, HEAD_DIM).transpose(0, 3, 2, 1, 4)
    q_stk = q_stk.reshape(bd, A_KV_HEADS, 4 * t_new, HEAD_DIM)
    o_s = _dsa_sample(pt, q_stk, sel_bias,
                      _group_minor(kbf_s, bd, t_new, A_KV_HEADS, HEAD_DIM),
                      _group_minor(vbf_s, bd, t_new, A_KV_HEADS, HEAD_DIM),
                      _position_minor(cache_a_k[li]), _position_minor(cache_a_v[li]))
    oa_s = o_s.reshape(ms, 512).astype(BF16)

    def finish_layer(layer, a, b_, wa, wb, x, md, tps):
        wg, wu, wd = w_gate[layer].astype(BF16), w_up[layer].astype(BF16), w_down[layer].astype(BF16)
        return _mixer_out_ffn(a, b_, wa, wb, x, md[2], gvec(layer, 1), gvec(layer, 2), md[3], md[4],
                              wg, wu, wd, md[5], gvec(layer, 3), tps)

    xp = finish_layer(0, oa_p.reshape(mp, 512), ob_p, wo_a, wo_b, xp, mod_p, tps_p)
    xs = finish_layer(0, oa_s, ob_s, wo_a, wo_b, xs, mod_s, tps_s)

    gp, gs = (b, s_len), (bd, t_new)
    a_k_p = _cache_out(k32_p, gp, (A_KV_HEADS, HEAD_DIM))
    a_v_p = _cache_out(v32_p, gp, (A_KV_HEADS, HEAD_DIM))
    a_idx_p = _cache_out(kiw_p[:, :IDX_DIM], gp, (IDX_DIM,))
    a_k_s = _cache_out(k32_s, gs, (A_KV_HEADS, HEAD_DIM))
    a_v_s = _cache_out(v32_s, gs, (A_KV_HEADS, HEAD_DIM))
    a_idx_s = _cache_out(kiw_s[:, :IDX_DIM], gs, (IDX_DIM,))
    gm_v_s = gv_s.reshape(1, bd, t_new, GM_WIDTH)

    layer = 1
    mod_p, mod_s = mods(layer)
    lam_init = 0.8 - 0.6 * math.exp(-0.3 * layer)
    w_cd = _pack_w_cd(w_in_cd[li])
    fox_b = jnp.pad(fox_bias[li], (0, LANES - D_HEADS)).reshape(1, LANES)
    lamv = jnp.stack([lam_q1[li], lam_k1[li], lam_q2[li], lam_k2[li]])
    gsub = g_subln[li].reshape(1, 2 * HEAD_DIM)
    wo = w_out_cd[li]
    wo_c, wo_d = wo[:512].astype(BF16), wo[512:].astype(BF16)

    (cq_p, ck32_p, cv32_p, ckbf_p, cvbf_p, dq_p, dk32_p, dv32_p, dkbf_p, dvbf_p, lf_p) = _inproj_cd(
        xp, gvec(1, 0), mod_p[0], mod_p[1], w_cd, rope_p, fox_b, tps_p(tm_p))
    (cq_s, ck32_s, cv32_s, ckbf_s, cvbf_s, dq_s, dk32_s, dv32_s, dkbf_s, dvbf_s, lf_s) = _inproj_cd(
        xs, gvec(1, 0), mod_s[0], mod_s[1], w_cd, rope_s, fox_b, 1)

    oc_p = _diff_prompt(lamv, gsub, r3(cq_p), r3(ckbf_p), r3(cvbf_p), lam_init)
    cum_t = _cumsum_lanes(lf_p[:, :D_HEADS])
    od_p = _fox_prompt(r3(dq_p), r3(dkbf_p), r3(dvbf_p), cum_t.transpose(0, 2, 1), cum_t)

    oc_s = _diff_sample(pt, lamv, gsub, _group_major(cq_s, bd, t_new, 2 * C_HEADS, HEAD_DIM),
                        _group_minor(ckbf_s, bd, t_new, 2 * C_HEADS, HEAD_DIM),
                        _group_major(cvbf_s, bd, t_new, C_HEADS, 2 * HEAD_DIM, PAGE_SIZE),
                        _position_minor(cache_c_k[li]),
                        cache_c_v[li].reshape(-1, PAGE_SIZE * C_HEADS, 2 * HEAD_DIM), lam_init)

    d_logf_s = _cache_out(lf_s[:, :D_HEADS], gs, (D_HEADS,))
    logf_s = d_logf_s[0]
    lf_t = jnp.pad(logf_s.transpose(0, 2, 1), ((0, 0), (0, 0), (0, LANES - t_new)))
    od_s = _fox_sample(pt, _group_major(dq_s, bd, t_new, D_HEADS, HEAD_DIM), lf_t,
                       _group_minor(dkbf_s, bd, t_new, D_HEADS, HEAD_DIM),
                       _group_minor(dvbf_s, bd, t_new, D_HEADS, HEAD_DIM),
                       _position_minor(cache_d_k[li]), _position_minor(cache_d_v[li]),
                       _position_minor(cache_d_logf[li]))

    xp = finish_layer(1, oc_p.reshape(mp, 512), od_p.reshape(mp, 512), wo_c, wo_d, xp, mod_p, tps_p)
    xs = finish_layer(1, oc_s.reshape(ms, 512).astype(BF16), od_s.reshape(ms, 512).astype(BF16), wo_c, wo_d, xs,
                      mod_s, tps_s)

    return (xp.reshape(b, s_len, D_MODEL), xs.reshape(bd, t_new, D_MODEL),
            a_k_p, a_v_p, a_idx_p,
            _cache_out(ck32_p, gp, (C_HEADS, 2, HEAD_DIM)), cv32_p.reshape((1,) + gp + (C_HEADS, 2 * HEAD_DIM)),
            _cache_out(dk32_p, gp, (D_HEADS, HEAD_DIM)), _cache_out(dv32_p, gp, (D_HEADS, HEAD_DIM)),
            _cache_out(lf_p[:, :D_HEADS], gp, (D_HEADS,)),
            a_k_s, a_v_s, a_idx_s,
            _cache_out(ck32_s, gs, (C_HEADS, 2, HEAD_DIM)), cv32_s.reshape((1,) + gs + (C_HEADS, 2 * HEAD_DIM)),
            _cache_out(dk32_s, gs, (D_HEADS, HEAD_DIM)), _cache_out(dv32_s, gs, (D_HEADS, HEAD_DIM)),
            d_logf_s, gm_v_s)
```

```python
import functools
import math

import numpy as np
import jax
import jax.numpy as jnp
from jax import lax
from jax.experimental import pallas as pl
from jax.experimental.pallas import tpu as pltpu

F32 = jnp.float32
BF16 = jnp.bfloat16
NEG_INF = float("-inf")

D_MODEL = 1024
HEAD_DIM = 64
ROPE_THETA = 500000.0
A_HEADS = 8
A_KV_HEADS = 2
IDX_HEADS = 8
IDX_DIM = 64
TOPK_MAX = 256
GM_GROUPS = 8
GM_WIDTH = GM_GROUPS * HEAD_DIM
CHUNK = 128
C_HEADS = 4
D_HEADS = 8
PAGE_SIZE = 128
EPS = 1e-6
D_FF = -(-(8 * D_MODEL) // (3 * 256)) * 256
LOG2E = math.log2(math.e)
Q_SCALE = HEAD_DIM ** -0.5 * LOG2E

LANES = 128
VMEM_LIMIT_BYTES = 56 * 1024 * 1024

ROW_TILE = 256
FFN_ROW_TILE = 512
Q_TILE = 256
CAUSAL_SEGMENTS = 8
SELECT_PAGES_PER_STEP = 64
DSA_PAGES_PER_STEP = 32
CD_PAGES_PER_STEP = 16
MOD_COL_TILE = 1536
BISECT_STEPS = 20

AB_Q, AB_K, AB_V, AB_QI, AB_KIW, AB_U, AB_GV, AB_COLS = 0, 512, 640, 768, 1280, 1408, 1920, 2432
CD_CQ, CD_CK, CD_CV, CD_DQ, CD_DK, CD_DV, CD_DF, CD_COLS = 0, 512, 1024, 1536, 2048, 2560, 3072, 3200
A_HEAD_PERM = (0, 4, 1, 5, 2, 6, 3, 7)


def _cparams(*sem):
    return pltpu.CompilerParams(dimension_semantics=sem, vmem_limit_bytes=VMEM_LIMIT_BYTES)


def _nt(a, b):
    return lax.dot_general(a, b, (((1,), (1,)), ((), ())), preferred_element_type=F32)


def _mm(a, b):
    return jnp.dot(a, b, preferred_element_type=F32)


def _rms(x, g):
    return x * lax.rsqrt(jnp.mean(x * x, axis=-1, keepdims=True) + EPS) * g


def _silu(x):
    return x / (1.0 + jnp.exp(-x))


def _gelu_tanh(x):
    return 0.5 * x * (1.0 + jnp.tanh(math.sqrt(2.0 / math.pi) * (x + 0.044715 * (x * x * x))))


def _half_masks(dtype):
    lane = lax.broadcasted_iota(jnp.int32, (1, LANES), 1)
    lo = (lane < HEAD_DIM).astype(dtype)
    return lo, (1 - lo).astype(dtype)


def _rope(z, c, s1, s2):
    outs = []
    for j in range(z.shape[1] // LANES):
        x = z[:, j * LANES:(j + 1) * LANES]
        outs.append(x * c + pltpu.roll(x, LANES - 8, 1) * s1 + pltpu.roll(x, 8, 1) * s2)
    return outs[0] if len(outs) == 1 else jnp.concatenate(outs, axis=1)


def _split3(x):
    hi = x.astype(BF16)
    r1 = x - hi.astype(F32)
    mid = r1.astype(BF16)
    lo = (r1 - mid.astype(F32)).astype(BF16)
    return hi, mid, lo


def _lane_prefix(x, tri):
    hi, mid, lo = _split3(x)
    r = _mm(jnp.concatenate([hi, mid, lo], axis=0), tri)
    return r[0:8] + r[8:16] + r[16:24]


def _tri(incl_upper=None, strict_lower=None):
    r = lax.broadcasted_iota(jnp.int32, (LANES, LANES), 0)
    c = lax.broadcasted_iota(jnp.int32, (LANES, LANES), 1)
    if incl_upper:
        return (r <= c).astype(BF16)
    return (r > c).astype(BF16)


def _count_ge(sm, x):
    return jnp.sum(jnp.where(sm >= x, 1.0, 0.0), axis=1, keepdims=True)


def _topk_bias(sm_ref, bias_ref, n_sel):
    kf = float(n_sel)
    sm = sm_ref[...]
    allowed = sm > NEG_INF
    mx = jnp.max(sm, axis=1, keepdims=True)
    mn = jnp.min(jnp.where(allowed, sm, jnp.inf), axis=1, keepdims=True)
    n_allowed = jnp.sum(jnp.where(allowed, 1.0, 0.0), axis=1, keepdims=True)
    few = n_allowed <= kf
    top_tie = _count_ge(sm, mx) >= kf

    def bisect(_, c):
        lo, hi = c
        mid = lo + (hi - lo) * 0.5
        ge = _count_ge(sm_ref[...], mid) >= kf
        return jnp.where(ge, mid, lo), jnp.where(ge, hi, mid)

    _, hi = lax.fori_loop(0, BISECT_STEPS, bisect, (mn, mx))

    done0 = jnp.where(few, 1.0, jnp.where(top_tie, 1.0, 0.0))
    thr0 = jnp.where(few, NEG_INF, mx)

    def not_done(c):
        return c[3] > 0.0

    def step(c):
        hi, thr, done, _ = c
        s = sm_ref[...]
        m = jnp.max(jnp.where(s < hi, s, NEG_INF), axis=1, keepdims=True)
        found = _count_ge(s, m) >= kf
        is_done = done > 0.5
        thr = jnp.where(is_done, thr, jnp.where(found, m, thr))
        hi = jnp.where(is_done, hi, jnp.where(found, hi, m))
        done = jnp.where(found, 1.0, done)
        return hi, thr, done, jnp.sum(1.0 - done)

    _, thr, _, _ = lax.while_loop(not_done, step, (hi, thr0, done0, jnp.sum(1.0 - done0)))

    is_gt = sm > thr
    is_eq = jnp.where(sm == thr, jnp.where(allowed, 1.0, 0.0), 0.0)
    need = kf - jnp.sum(jnp.where(is_gt, 1.0, 0.0), axis=1, keepdims=True)
    n_eq = jnp.sum(is_eq, axis=1, keepdims=True)
    bias_ref[...] = jnp.where(sm >= thr, jnp.where(allowed, 0.0, NEG_INF), NEG_INF)

    @pl.when(jnp.max(n_eq - need) > 0.0)
    def _():
        tri = _tri(incl_upper=True)

        def blk(j, run):
            cols = pl.ds(pl.multiple_of(j * LANES, LANES), LANES)
            s = sm_ref[:, cols]
            e = jnp.where(s == thr, jnp.where(s > NEG_INF, 1.0, 0.0), 0.0)
            rank = run + _mm(e.astype(BF16), tri)
            keep = jnp.where(e > 0.5, jnp.where(rank <= need, 1.0, 0.0), 0.0)
            bias_ref[:, cols] = jnp.where(s > thr, 0.0, jnp.where(keep > 0.5, 0.0, NEG_INF))
            return run + jnp.sum(e, axis=1, keepdims=True)

        lax.fori_loop(0, sm_ref.shape[1] // LANES, blk, jnp.zeros_like(need))


def _mod_kernel(c_ref, w_ref, b_ref, o_ref):
    a = _silu(c_ref[...]).astype(BF16)
    o_ref[...] = _mm(a, w_ref[...].astype(BF16)) + b_ref[...]


def _modulation(c_all, w_mod, b_mod):
    n_layers, _, n_out = w_mod.shape
    rows = c_all.shape[0]
    tn = MOD_COL_TILE
    return pl.pallas_call(
        _mod_kernel,
        grid=(n_layers, n_out // tn),
        in_specs=[
            pl.BlockSpec((rows, D_MODEL), lambda l, n: (0, 0)),
            pl.BlockSpec((None, D_MODEL, tn), lambda l, n: (l, 0, n)),
            pl.BlockSpec((None, 1, tn), lambda l, n: (l, 0, n)),
        ],
        out_specs=pl.BlockSpec((None, rows, tn), lambda l, n: (l, 0, n)),
        out_shape=jax.ShapeDtypeStruct((n_layers, rows, n_out), F32),
        compiler_params=_cparams("arbitrary", "arbitrary"),
        name="modulation",
    )(c_all, w_mod, b_mod.reshape(n_layers, 1, n_out))


def _rowvec_spec(vec, tm, tiles_per_seq):
    if vec.ndim == 3:
        return pl.BlockSpec((None, 1, vec.shape[-1]), lambda i: (i // tiles_per_seq, 0, 0))
    return pl.BlockSpec((tm, vec.shape[-1]), lambda i: (i, 0))


def _const_spec(arr, single_buffer=False):
    nd = arr.ndim
    if single_buffer:
        return pl.BlockSpec(arr.shape, lambda i: (0,) * nd, pipeline_mode=pl.Buffered(1))
    return pl.BlockSpec(arr.shape, lambda i: (0,) * nd)


def _position_minor_out(m, tm, tiles_per_seq):
    seq_len = tm * tiles_per_seq
    spec = lambda width: pl.BlockSpec((None, width, tm), lambda i: (i // tiles_per_seq, 0, i % tiles_per_seq))
    shape = lambda width: jax.ShapeDtypeStruct((m // seq_len, width, seq_len), F32)
    return spec, shape


def _inproj_ab_kernel(x_ref, g_ref, sh_ref, sc_ref, w_ref, rc_ref, rs1_ref, rs2_ref, gmg_ref, wsp_ref, bsp_ref,
                      q_ref, k32_ref, v32_ref, kiw_ref, kbf_ref, vbf_ref, qi_ref, kid_ref, ob_ref, *gv_ref):
    h = (_rms(x_ref[...], g_ref[...]) * (1.0 + sc_ref[...]) + sh_ref[...]).astype(BF16)
    c, s1, s2 = rc_ref[...], rs1_ref[...], rs2_ref[...]

    def seg(lo, hi):
        return _mm(h, w_ref[:, lo:hi])

    q_ref[...] = (_rope(seg(AB_Q, AB_K), c, s1, s2) * Q_SCALE).astype(BF16)
    k = _rope(seg(AB_K, AB_V), c, s1, s2)
    k32_ref[...] = k.T
    kbf_ref[...] = k.astype(BF16)
    v = seg(AB_V, AB_QI)
    v32_ref[...] = v.T
    vbf_ref[...] = v.astype(BF16)
    qi_ref[...] = _rope(seg(AB_QI, AB_KIW), c, s1, s2).astype(BF16)
    kz = seg(AB_KIW, AB_U)
    kr = _rope(kz, c, s1, s2)
    lane = lax.broadcasted_iota(jnp.int32, kz.shape, 1)
    kiw_ref[...] = jnp.where(lane < IDX_DIM, kr, kz).T
    kid_ref[...] = jnp.where(lane < IDX_DIM, kr, pltpu.roll(kr, IDX_DIM, 1)).astype(BF16)

    u = _gelu_tanh(seg(AB_U, AB_GV))
    gv = _gelu_tanh(seg(AB_GV, AB_COLS))
    gc = gv - jnp.mean(gv, axis=-1, keepdims=True)
    gvn = gc * lax.rsqrt(jnp.mean(gc * gc, axis=-1, keepdims=True) + EPS) * gmg_ref[...]
    if gv_ref:
        gv_ref[0][...] = gvn
    gvb = gvn.astype(BF16)
    lane = lax.broadcasted_iota(jnp.int32, (1, LANES), 1)
    mix = []
    for j in range(GM_GROUPS // 2):
        slab = gvb[:, j * LANES:(j + 1) * LANES]
        mix.append(jnp.where(lane < HEAD_DIM, _mm(wsp_ref[2 * j], slab), _mm(wsp_ref[2 * j + 1], slab)))
    ob_ref[...] = (u * (jnp.concatenate(mix, axis=1) + bsp_ref[...])).astype(BF16)


def _inproj_ab(x, g, shift, scale, w, rope_tabs, gm_g, wsp, bsp, tiles_per_seq, want_gv):
    m = x.shape[0]
    tm = min(ROW_TILE, m)
    n_tab_tiles = rope_tabs[0].shape[0] // tm
    row = lambda width: pl.BlockSpec((tm, width), lambda i: (i, 0))
    tab = pl.BlockSpec((tm, LANES), lambda i: (i % n_tab_tiles, 0))
    sds = lambda width, dt: jax.ShapeDtypeStruct((m, width), dt)
    col, col_sds = _position_minor_out(m, tm, tiles_per_seq)
    out_shape = [sds(512, BF16), col_sds(128), col_sds(128), col_sds(128), sds(128, BF16), sds(128, BF16),
                 sds(512, BF16), sds(128, BF16), sds(512, BF16)]
    out_specs = [row(512), col(128), col(128), col(128), row(128), row(128), row(512), row(128), row(512)]
    if want_gv:
        out_shape.append(sds(512, F32))
        out_specs.append(row(512))
    return pl.pallas_call(
        _inproj_ab_kernel,
        grid=(m // tm,),
        in_specs=[row(D_MODEL), _const_spec(g), _rowvec_spec(shift, tm, tiles_per_seq),
                  _rowvec_spec(scale, tm, tiles_per_seq), _const_spec(w), tab, tab, tab,
                  _const_spec(gm_g), _const_spec(wsp), _const_spec(bsp)],
        out_specs=out_specs,
        out_shape=out_shape,
        compiler_params=_cparams("parallel"),
        name="inproj_ab",
    )(x, g, shift, scale, w, *rope_tabs, gm_g, wsp, bsp)


def _inproj_cd_kernel(x_ref, g_ref, sh_ref, sc_ref, w_ref, rc_ref, rs1_ref, rs2_ref, fb_ref,
                      cq_ref, ck32_ref, cv32_ref, ckbf_ref, cvbf_ref,
                      dq_ref, dk32_ref, dv32_ref, dkbf_ref, dvbf_ref, lf_ref):
    h = (_rms(x_ref[...], g_ref[...]) * (1.0 + sc_ref[...]) + sh_ref[...]).astype(BF16)
    c, s1, s2 = rc_ref[...], rs1_ref[...], rs2_ref[...]

    def seg(lo, hi):
        return _mm(h, w_ref[:, lo:hi])

    cq_ref[...] = (_rope(seg(CD_CQ, CD_CK), c, s1, s2) * Q_SCALE).astype(BF16)
    ck = _rope(seg(CD_CK, CD_CV), c, s1, s2)
    ck32_ref[...] = ck.T
    ckbf_ref[...] = ck.astype(BF16)
    cv = seg(CD_CV, CD_DQ)
    cv32_ref[...] = cv
    cvbf_ref[...] = cv.astype(BF16)
    dq_ref[...] = (seg(CD_DQ, CD_DK) * Q_SCALE).astype(BF16)
    dk = seg(CD_DK, CD_DV)
    dk32_ref[...] = dk.T
    dkbf_ref[...] = dk.astype(BF16)
    dv = seg(CD_DV, CD_DF)
    dv32_ref[...] = dv.T
    dvbf_ref[...] = dv.astype(BF16)
    f = seg(CD_DF, CD_COLS) + fb_ref[...]
    lf_ref[...] = (jnp.minimum(f, 0.0) - jnp.log(1.0 + jnp.exp(-jnp.abs(f)))).T


def _inproj_cd(x, g, shift, scale, w, rope_tabs, fox_b, tiles_per_seq):
    m = x.shape[0]
    tm = min(ROW_TILE, m)
    n_tab_tiles = rope_tabs[0].shape[0] // tm
    row = lambda width: pl.BlockSpec((tm, width), lambda i: (i, 0))
    tab = pl.BlockSpec((tm, LANES), lambda i: (i % n_tab_tiles, 0))
    sds = lambda width, dt: jax.ShapeDtypeStruct((m, width), dt)
    col, col_sds = _position_minor_out(m, tm, tiles_per_seq)
    return pl.pallas_call(
        _inproj_cd_kernel,
        grid=(m // tm,),
        in_specs=[row(D_MODEL), _const_spec(g), _rowvec_spec(shift, tm, tiles_per_seq),
                  _rowvec_spec(scale, tm, tiles_per_seq), _const_spec(w), tab, tab, tab, _const_spec(fox_b)],
        out_specs=[row(512), col(512), row(512), row(512), row(512),
                   row(512), col(512), col(512), row(512), row(512), col(128)],
        out_shape=[sds(512, BF16), col_sds(512), sds(512, F32), sds(512, BF16), sds(512, BF16),
                   sds(512, BF16), col_sds(512), col_sds(512), sds(512, BF16), sds(512, BF16), col_sds(128)],
        compiler_params=_cparams("parallel"),
        name="inproj_cd",
    )(x, g, shift, scale, w, *rope_tabs, fox_b)


def _mixer_out_ffn_kernel(a_ref, b_ref, wa_ref, wb_ref, x_ref, gate1_ref, g1_ref,
                          g2_ref, sh_ref, sc_ref, wg_ref, wu_ref, wd_ref, gate2_ref, g3_ref, o_ref):
    op = _mm(a_ref[...], wa_ref[...]) + _mm(b_ref[...], wb_ref[...])
    x = x_ref[...] + gate1_ref[...] * _rms(op, g1_ref[...])
    h = (_rms(x, g2_ref[...]) * (1.0 + sc_ref[...]) + sh_ref[...]).astype(BF16)
    act = (_silu(_mm(h, wg_ref[...])) * _mm(h, wu_ref[...])).astype(BF16)
    o_ref[...] = x + gate2_ref[...] * _rms(_mm(act, wd_ref[...]), g3_ref[...])


def _mixer_out_ffn(a, b, wa, wb, x, gate1, g1, g2, shift, scale, wg, wu, wd, gate2, g3, tiles_per_seq_of):
    m = x.shape[0]
    tm = min(FFN_ROW_TILE, m)
    tps = tiles_per_seq_of(tm)
    row = lambda width: pl.BlockSpec((tm, width), lambda i: (i, 0))
    vec = lambda v: _rowvec_spec(v, tm, tps)
    once = lambda arr: _const_spec(arr, single_buffer=True)
    return pl.pallas_call(
        _mixer_out_ffn_kernel,
        grid=(m // tm,),
        in_specs=[row(a.shape[1]), row(b.shape[1]), once(wa), once(wb), row(D_MODEL), vec(gate1), _const_spec(g1),
                  _const_spec(g2), vec(shift), vec(scale), once(wg), once(wu), once(wd), vec(gate2), _const_spec(g3)],
        out_specs=row(D_MODEL),
        out_shape=jax.ShapeDtypeStruct((m, D_MODEL), F32),
        compiler_params=_cparams("parallel"),
        name="mixer_out_ffn",
    )(a, b, wa, wb, x, gate1, g1, g2, shift, scale, wg, wu, wd, gate2, g3)


def _dsa_prompt_kernel(qi_ref, wi_ref, kid_ref, q_ref, k_ref, v_ref, o_ref, sm_ref, bias_ref, *, n_sel, q_blk0):
    tq, s_len = sm_ref.shape
    i = q_blk0 + pl.program_id(1)
    m_lo, m_hi = _half_masks(BF16)
    kid = kid_ref[...]
    score = jnp.zeros((tq, s_len), F32)
    for j in range(IDX_HEADS // 2):
        slab = qi_ref[:, j * LANES:(j + 1) * LANES]
        s = _nt(jnp.concatenate([slab * m_lo, slab * m_hi], axis=0), kid)
        score = score + jnp.maximum(s[:tq], 0.0) * wi_ref[:, 2 * j:2 * j + 1]
        score = score + jnp.maximum(s[tq:], 0.0) * wi_ref[:, 2 * j + 1:2 * j + 2]
    score = score * (IDX_HEADS * IDX_DIM) ** -0.5
    row = i * tq + lax.broadcasted_iota(jnp.int32, (tq, s_len), 0)
    col = lax.broadcasted_iota(jnp.int32, (tq, s_len), 1)
    sm_ref[...] = jnp.where(col <= row, score, NEG_INF)
    _topk_bias(sm_ref, bias_ref, n_sel)

    k = k_ref[...]
    v = v_ref[...]
    n_slab = A_HEADS // 2
    outs = []
    for mask in (m_lo, m_hi):
        stk = jnp.concatenate([q_ref[:, j * LANES:(j + 1) * LANES] * mask for j in range(n_slab)], axis=0)
        lg = _nt(stk, k)
        per = []
        for j in range(n_slab):
            l = lg[j * tq:(j + 1) * tq] + bias_ref[...]
            p = jnp.exp2(l - jnp.max(l, axis=1, keepdims=True))
            per.append(_mm(p.astype(BF16), v) / jnp.sum(p, axis=1, keepdims=True))
        outs.append(per)
    lane = lax.broadcasted_iota(jnp.int32, (1, LANES), 1)
    for j in range(n_slab):
        o_ref[:, j * LANES:(j + 1) * LANES] = jnp.where(lane < HEAD_DIM, outs[0][j], outs[1][j]).astype(BF16)


def _causal_segments(s_len, tq):
    n_seg = min(CAUSAL_SEGMENTS, s_len // tq)
    per = s_len // tq // n_seg
    return [(seg * per, per, (seg + 1) * per * tq) for seg in range(n_seg)]


def _dsa_prompt(qi, wi, kid, q, k, v, n_sel):
    b, s_len, _ = q.shape
    tq = min(Q_TILE, s_len)
    outs = []
    for q_blk0, n_blk, kext in _causal_segments(s_len, tq):
        qblk = lambda width: pl.BlockSpec((None, tq, width), lambda bb, i: (bb, q_blk0 + i, 0))
        keys = lambda width: pl.BlockSpec((None, kext, width), lambda bb, i: (bb, 0, 0))
        outs.append(pl.pallas_call(
            functools.partial(_dsa_prompt_kernel, n_sel=n_sel, q_blk0=q_blk0),
            grid=(b, n_blk),
            in_specs=[qblk(512), qblk(IDX_HEADS), keys(128), qblk(512), keys(128), keys(128)],
            out_specs=pl.BlockSpec((None, tq, 512), lambda bb, i: (bb, i, 0)),
            out_shape=jax.ShapeDtypeStruct((b, n_blk * tq, 512), BF16),
            scratch_shapes=[pltpu.VMEM((tq, kext), F32), pltpu.VMEM((tq, kext), F32)],
            compiler_params=_cparams("parallel", "arbitrary"),
            name="dsa_prompt",
        )(qi, wi, kid, q, k, v))
    return jnp.concatenate(outs, axis=1)


def _diff_lambda(lamv_ref, lam_init):
    lv = lamv_ref[...]
    d1 = jnp.sum(lv[0:1] * lv[1:2], axis=1, keepdims=True)
    d2 = jnp.sum(lv[2:3] * lv[3:4], axis=1, keepdims=True)
    return jnp.exp(d1) - jnp.exp(d2) + lam_init


def _causal_softmax_pv(lg, v, row0, tq, seg_rows):
    r, kext = lg.shape
    split = kext - seg_rows
    row = row0 + lax.broadcasted_iota(jnp.int32, (r, seg_rows), 0) % tq
    col = split + lax.broadcasted_iota(jnp.int32, (r, seg_rows), 1)
    hi = jnp.where(col <= row, lg[:, split:], NEG_INF)
    m = jnp.max(hi, axis=1, keepdims=True)
    if split:
        lo = lg[:, :split]
        m = jnp.maximum(m, jnp.max(lo, axis=1, keepdims=True))
    p = jnp.exp2(hi - m)
    den = jnp.sum(p, axis=1, keepdims=True)
    pv = _mm(p.astype(BF16), v[split:])
    if split:
        p = jnp.exp2(lo - m)
        den = den + jnp.sum(p, axis=1, keepdims=True)
        pv = pv + _mm(p.astype(BF16), v[:split])
    return pv / den


def _diff_prompt_kernel(lamv_ref, gsub_ref, q_ref, k_ref, v_ref, o_ref, *, lam_init, q_blk0, seg_rows):
    tq = q_ref.shape[0]
    i = q_blk0 + pl.program_id(1)
    lam = _diff_lambda(lamv_ref, lam_init)
    m_lo, m_hi = _half_masks(BF16)
    for h in range(C_HEADS):
        cols = slice(h * LANES, (h + 1) * LANES)
        slab = q_ref[:, cols]
        lg = _nt(jnp.concatenate([slab * m_lo, slab * m_hi], axis=0), k_ref[:, cols])
        pv = _causal_softmax_pv(lg, v_ref[:, cols], i * tq, tq, seg_rows)
        o = pv[:tq] - lam * pv[tq:]
        o_ref[:, cols] = (_rms(o, gsub_ref[...]) * (1.0 - lam_init)).astype(BF16)


def _diff_prompt(lamv, gsub, q, k, v, lam_init):
    b, s_len, _ = q.shape
    tq = min(Q_TILE, s_len)
    cst = lambda arr: pl.BlockSpec(arr.shape, lambda bb, i: (0, 0))
    outs = []
    for q_blk0, n_blk, kext in _causal_segments(s_len, tq):
        qblk = pl.BlockSpec((None, tq, 512), lambda bb, i: (bb, q_blk0 + i, 0))
        keys = pl.BlockSpec((None, kext, 512), lambda bb, i: (bb, 0, 0))
        outs.append(pl.pallas_call(
            functools.partial(_diff_prompt_kernel, lam_init=lam_init, q_blk0=q_blk0, seg_rows=n_blk * tq),
            grid=(b, n_blk),
            in_specs=[cst(lamv), cst(gsub), qblk, keys, keys],
            out_specs=pl.BlockSpec((None, tq, 512), lambda bb, i: (bb, i, 0)),
            out_shape=jax.ShapeDtypeStruct((b, n_blk * tq, 512), BF16),
            compiler_params=_cparams("parallel", "arbitrary"),
            name="diff_prompt",
        )(lamv, gsub, q, k, v))
    return jnp.concatenate(outs, axis=1)


def _cumsum_kernel(x_ref, o_ref):
    tri = _tri(incl_upper=True)
    carry = jnp.zeros((8, 1), F32)
    for c in range(x_ref.shape[1] // LANES):
        cols = slice(c * LANES, (c + 1) * LANES)
        out = _lane_prefix(x_ref[:, cols], tri) + carry
        o_ref[:, cols] = out * LOG2E
        carry = out[:, LANES - 1:LANES]


def _cumsum_lanes(x):
    b, h, s_len = x.shape
    blk = pl.BlockSpec((None, h, s_len), lambda bb: (bb, 0, 0))
    return pl.pallas_call(
        _cumsum_kernel,
        grid=(b,),
        in_specs=[blk],
        out_specs=blk,
        out_shape=jax.ShapeDtypeStruct(x.shape, F32),
        compiler_params=_cparams("parallel"),
        name="fox_cumsum",
    )(x)


def _fox_prompt_kernel(q_ref, k_ref, v_ref, cq_ref, ck_ref, o_ref, *, q_blk0, seg_rows):
    tq = q_ref.shape[0]
    i = q_blk0 + pl.program_id(1)
    m_lo, m_hi = _half_masks(BF16)
    lane = lax.broadcasted_iota(jnp.int32, (1, LANES), 1)
    for j in range(D_HEADS // 2):
        cols = slice(j * LANES, (j + 1) * LANES)
        slab = q_ref[:, cols]
        lg = _nt(jnp.concatenate([slab * m_lo, slab * m_hi], axis=0), k_ref[:, cols])
        v = v_ref[:, cols]
        pv = []
        for hh in range(2):
            h = 2 * j + hh
            l = lg[hh * tq:(hh + 1) * tq] + cq_ref[:, h:h + 1] - ck_ref[h:h + 1, :]
            pv.append(_causal_softmax_pv(l, v, i * tq, tq, seg_rows))
        o_ref[:, cols] = jnp.where(lane < HEAD_DIM, pv[0], pv[1]).astype(BF16)


def _fox_prompt(q, k, v, cum, cum_t):
    b, s_len, _ = q.shape
    tq = min(Q_TILE, s_len)
    outs = []
    for q_blk0, n_blk, kext in _causal_segments(s_len, tq):
        qblk = lambda width: pl.BlockSpec((None, tq, width), lambda bb, i: (bb, q_blk0 + i, 0))
        keys = pl.BlockSpec((None, kext, 512), lambda bb, i: (bb, 0, 0))
        outs.append(pl.pallas_call(
            functools.partial(_fox_prompt_kernel, q_blk0=q_blk0, seg_rows=n_blk * tq),
            grid=(b, n_blk),
            in_specs=[qblk(512), keys, keys, qblk(D_HEADS),
                      pl.BlockSpec((None, D_HEADS, kext), lambda bb, i: (bb, 0, 0))],
            out_specs=pl.BlockSpec((None, tq, 512), lambda bb, i: (bb, i, 0)),
            out_shape=jax.ShapeDtypeStruct((b, n_blk * tq, 512), BF16),
            compiler_params=_cparams("parallel", "arbitrary"),
            name="fox_prompt",
        )(q, k, v, cum, cum_t))
    return jnp.concatenate(outs, axis=1)


def _pages_per_step(n_pages, cap):
    return max(d for d in range(1, min(cap, n_pages) + 1) if n_pages % d == 0)


def _page_specs(block, n_pg, reverse_from=None):
    specs = []
    for pg in range(n_pg):
        if reverse_from is None:
            idx = lambda b, st, pt, pg=pg: (pt[b, st * n_pg + pg],) + (0,) * (len(block) - 1)
        else:
            idx = lambda b, st, pt, pg=pg: (pt[b, reverse_from - (st * n_pg + pg)],) + (0,) * (len(block) - 1)
        specs.append(pl.BlockSpec((None,) + tuple(block[1:]), idx))
    return specs


def _per_seq(shape):
    nd = len(shape)
    return pl.BlockSpec((None,) + tuple(shape[1:]), lambda b, st, pt: (b,) + (0,) * (nd - 1))


def _softmax_init(m_ref, l_ref, acc_ref):
    m_ref[...] = jnp.full(m_ref.shape, NEG_INF, F32)
    l_ref[...] = jnp.zeros(l_ref.shape, F32)
    acc_ref[...] = jnp.zeros(acc_ref.shape, F32)


def _attend(q_ref, keys_t_of, values_of, n_v, values_t, bias, m_ref, l_ref, acc_ref):
    n_k, r, _ = q_ref.shape
    pv = _nt if values_t else _mm
    lg = jnp.concatenate([_mm(q_ref[i], keys_t_of(i)) for i in range(n_k)], axis=0) + bias
    m_old = m_ref[...]
    m_new = jnp.maximum(m_old, jnp.max(lg, axis=1, keepdims=True))
    m_safe = jnp.where(m_new == NEG_INF, 0.0, m_new)
    alpha = jnp.exp2(m_old - m_safe)
    p = jnp.exp2(lg - m_safe)
    l_ref[...] = alpha * l_ref[...] + jnp.sum(p, axis=1, keepdims=True)
    m_ref[...] = m_new
    p = p.astype(BF16)
    rv = r * (n_k // n_v)
    for j in range(n_v):
        rows = slice(j * rv, (j + 1) * rv)
        acc_ref[rows, :] = alpha[rows] * acc_ref[rows, :] + pv(p[rows], values_of(j))


def _page_lanes(pages, *idx):
    return jnp.concatenate([(p[idx] if idx else p[...]).astype(BF16) for p in pages], axis=1)


def _rows8(x):
    return jnp.concatenate([jnp.broadcast_to(x[h:h + 1], (8, x.shape[1])) for h in range(x.shape[0])], axis=0)


def _head_sum(w, n_groups):
    out = w[0:8]
    for h in range(1, n_groups):
        out = out + w[8 * h:8 * h + 8]
    return out


def _new_key_mask(rows):
    t = lax.broadcasted_iota(jnp.int32, (rows, LANES), 0) % 8
    c = lax.broadcasted_iota(jnp.int32, (rows, LANES), 1)
    return c <= t


def _dsa_select_kernel(pt_ref, q_ref, w_ref, knew_ref, *rest, n_pg, n_sel):
    pages = rest[:n_pg]
    bias_ref, sm_ref = rest[n_pg], rest[n_pg + 1]
    st = pl.program_id(1)
    q = q_ref[...]
    wcol = w_ref[...]
    scale = (IDX_HEADS * IDX_DIM) ** -0.5

    def scores(keys_t):
        return _head_sum(jnp.maximum(_mm(q, keys_t), 0.0) * wcol, IDX_HEADS) * scale

    step_keys = n_pg * PAGE_SIZE
    col0 = pl.multiple_of(st * step_keys, step_keys)
    sm_ref[:, pl.ds(col0, step_keys)] = scores(_page_lanes(pages))

    @pl.when(st == pl.num_programs(1) - 1)
    def _():
        past = sm_ref.shape[1] - LANES
        sm_ref[:, past:] = jnp.where(_new_key_mask(8), scores(knew_ref[...]), NEG_INF)
        _topk_bias(sm_ref, bias_ref, n_sel)


def _dsa_select(page_table, q, wcol, knew, cache_idx, n_sel):
    bd, n_pages = page_table.shape
    n_pg = _pages_per_step(n_pages, SELECT_PAGES_PER_STEP)
    width = n_pages * PAGE_SIZE + LANES
    return pl.pallas_call(
        functools.partial(_dsa_select_kernel, n_pg=n_pg, n_sel=n_sel),
        grid_spec=pltpu.PrefetchScalarGridSpec(
            num_scalar_prefetch=1,
            grid=(bd, n_pages // n_pg),
            in_specs=[_per_seq(q.shape), _per_seq(wcol.shape), _per_seq(knew.shape)]
                     + _page_specs(cache_idx.shape, n_pg),
            out_specs=pl.BlockSpec((None, 8, width), lambda b, st, pt: (b, 0, 0)),
            scratch_shapes=[pltpu.VMEM((8, width), F32)],
        ),
        out_shape=jax.ShapeDtypeStruct((bd, 8, width), F32),
        compiler_params=_cparams("parallel", "arbitrary"),
        name="dsa_select",
    )(page_table, q, wcol, knew, *([cache_idx] * n_pg))


def _dsa_sample_kernel(pt_ref, q_ref, bias_ref, knew_ref, vnew_ref, *rest, n_pg):
    kpages, vpages = rest[:n_pg], rest[n_pg:2 * n_pg]
    o_ref, m_ref, l_ref, acc_ref = rest[2 * n_pg:]
    st = pl.program_id(1)

    @pl.when(st == 0)
    def _():
        _softmax_init(m_ref, l_ref, acc_ref)

    n_g, rg, _ = q_ref.shape
    n_rep = n_g * rg // 8

    def bias_rows(cols):
        bt = bias_ref[:, cols]
        return jnp.concatenate([bt] * n_rep, axis=0)

    step_keys = n_pg * PAGE_SIZE
    col0 = pl.multiple_of(st * step_keys, step_keys)
    _attend(q_ref, lambda g: _page_lanes(kpages, g), lambda g: _page_lanes(vpages, g), n_g, True,
            bias_rows(pl.ds(col0, step_keys)), m_ref, l_ref, acc_ref)

    @pl.when(st == pl.num_programs(1) - 1)
    def _():
        past = bias_ref.shape[1] - LANES
        _attend(q_ref, lambda g: knew_ref[g], lambda g: vnew_ref[g], n_g, True,
                bias_rows(slice(past, past + LANES)), m_ref, l_ref, acc_ref)
        o = acc_ref[...] / l_ref[...]
        for g in range(n_g):
            for j in range(rg // 8):
                lo = j * LANES + g * HEAD_DIM
                o_ref[:, lo:lo + HEAD_DIM] = o[g * rg + 8 * j:g * rg + 8 * j + 8]


def _dsa_sample(page_table, q, bias, knew, vnew, cache_k, cache_v):
    bd, n_pages = page_table.shape
    n_pg = _pages_per_step(n_pages, DSA_PAGES_PER_STEP)
    rows = q.shape[1] * q.shape[2]
    return pl.pallas_call(
        functools.partial(_dsa_sample_kernel, n_pg=n_pg),
        grid_spec=pltpu.PrefetchScalarGridSpec(
            num_scalar_prefetch=1,
            grid=(bd, n_pages // n_pg),
            in_specs=[_per_seq(q.shape), _per_seq(bias.shape), _per_seq(knew.shape), _per_seq(vnew.shape)]
                     + _page_specs(cache_k.shape, n_pg) + _page_specs(cache_v.shape, n_pg),
            out_specs=pl.BlockSpec((None, 8, 512), lambda b, st, pt: (b, 0, 0)),
            scratch_shapes=[pltpu.VMEM((rows, 1), F32), pltpu.VMEM((rows, 1), F32),
                            pltpu.VMEM((rows, HEAD_DIM), F32)],
        ),
        out_shape=jax.ShapeDtypeStruct((bd, 8, 512), F32),
        compiler_params=_cparams("parallel", "arbitrary"),
        name="dsa_sample",
    )(page_table, q, bias, knew, vnew, *([cache_k] * n_pg), *([cache_v] * n_pg))


def _diff_sample_kernel(pt_ref, lamv_ref, gsub_ref, q_ref, knew_ref, vnew_ref, *rest, n_pg, lam_init):
    kpages, vpages = rest[:n_pg], rest[n_pg:2 * n_pg]
    o_ref, m_ref, l_ref, acc_ref = rest[2 * n_pg:]
    st = pl.program_id(1)

    @pl.when(st == 0)
    def _():
        _softmax_init(m_ref, l_ref, acc_ref)

    n_k = q_ref.shape[0]

    def values(h):
        rows = pl.ds(h, PAGE_SIZE, stride=C_HEADS)
        return jnp.concatenate([p[rows, :].astype(BF16) for p in vpages], axis=0)

    _attend(q_ref, lambda i: _page_lanes(kpages, i // 2, i % 2), values, C_HEADS, False,
            0.0, m_ref, l_ref, acc_ref)

    @pl.when(st == pl.num_programs(1) - 1)
    def _():
        bias = jnp.where(_new_key_mask(8 * n_k), 0.0, NEG_INF)
        _attend(q_ref, lambda i: knew_ref[i], lambda h: vnew_ref[h], C_HEADS, False, bias, m_ref, l_ref, acc_ref)
        lam = _diff_lambda(lamv_ref, lam_init)
        o = acc_ref[...] / l_ref[...]
        for h in range(C_HEADS):
            oh = o[16 * h:16 * h + 8] - lam * o[16 * h + 8:16 * h + 16]
            o_ref[:, h * LANES:(h + 1) * LANES] = _rms(oh, gsub_ref[...]) * (1.0 - lam_init)


def _diff_sample(page_table, lamv, gsub, q, knew, vnew, cache_k, cache_v, lam_init):
    bd, n_pages = page_table.shape
    n_pg = _pages_per_step(n_pages, CD_PAGES_PER_STEP)
    rows = q.shape[1] * q.shape[2]
    cst = lambda arr: pl.BlockSpec(arr.shape, lambda b, st, pt: (0, 0))
    return pl.pallas_call(
        functools.partial(_diff_sample_kernel, n_pg=n_pg, lam_init=lam_init),
        grid_spec=pltpu.PrefetchScalarGridSpec(
            num_scalar_prefetch=1,
            grid=(bd, n_pages // n_pg),
            in_specs=[cst(lamv), cst(gsub), _per_seq(q.shape), _per_seq(knew.shape), _per_seq(vnew.shape)]
                     + _page_specs(cache_k.shape, n_pg) + _page_specs(cache_v.shape, n_pg),
            out_specs=pl.BlockSpec((None, 8, 512), lambda b, st, pt: (b, 0, 0)),
            scratch_shapes=[pltpu.VMEM((rows, 1), F32), pltpu.VMEM((rows, 1), F32),
                            pltpu.VMEM((rows, 2 * HEAD_DIM), F32)],
        ),
        out_shape=jax.ShapeDtypeStruct((bd, 8, 512), F32),
        compiler_params=_cparams("parallel", "arbitrary"),
        name="diff_sample",
    )(page_table, lamv, gsub, q, knew, vnew, *([cache_k] * n_pg), *([cache_v] * n_pg))


def _fox_sample_kernel(pt_ref, q_ref, lft_ref, knew_ref, vnew_ref, *rest, n_pg):
    kpages, vpages, fpages = rest[:n_pg], rest[n_pg:2 * n_pg], rest[2 * n_pg:3 * n_pg]
    o_ref, m_ref, l_ref, acc_ref, carry_ref = rest[3 * n_pg:]
    st = pl.program_id(1)

    @pl.when(st == 0)
    def _():
        _softmax_init(m_ref, l_ref, acc_ref)
        carry_ref[...] = jnp.zeros(carry_ref.shape, F32)

    n_h = q_ref.shape[0]
    cn64 = _rows8(_lane_prefix(lft_ref[...], _tri(incl_upper=True)) * LOG2E)
    mine = lax.broadcasted_iota(jnp.int32, cn64.shape, 1) == lax.broadcasted_iota(jnp.int32, cn64.shape, 0) % 8
    cn_col = jnp.sum(jnp.where(mine, cn64, 0.0), axis=1, keepdims=True)
    tri_after = _tri(strict_lower=True)

    carry = carry_ref[...]
    sfx = []
    for pg in range(n_pg):
        x = fpages[pg][...]
        sfx.append(_lane_prefix(x, tri_after) + carry)
        carry = carry + jnp.sum(x, axis=1, keepdims=True)
    carry_ref[...] = carry
    sfx = jnp.concatenate(sfx, axis=1)
    _attend(q_ref, lambda h: _page_lanes(kpages, h), lambda h: _page_lanes(vpages, h), n_h, True,
            cn_col + _rows8(sfx * LOG2E), m_ref, l_ref, acc_ref)

    @pl.when(st == pl.num_programs(1) - 1)
    def _():
        bias = jnp.where(_new_key_mask(8 * n_h), cn_col - cn64, NEG_INF)
        _attend(q_ref, lambda h: knew_ref[h], lambda h: vnew_ref[h], n_h, True, bias, m_ref, l_ref, acc_ref)
        o = acc_ref[...] / l_ref[...]
        for h in range(n_h):
            o_ref[:, h * HEAD_DIM:(h + 1) * HEAD_DIM] = o[8 * h:8 * h + 8]


def _fox_sample(page_table, q, lf_t, knew, vnew, cache_k, cache_v, cache_lf_t):
    bd, n_pages = page_table.shape
    n_pg = _pages_per_step(n_pages, CD_PAGES_PER_STEP)
    rows = q.shape[1] * q.shape[2]
    last = n_pages - 1
    return pl.pallas_call(
        functools.partial(_fox_sample_kernel, n_pg=n_pg),
        grid_spec=pltpu.PrefetchScalarGridSpec(
            num_scalar_prefetch=1,
            grid=(bd, n_pages // n_pg),
            in_specs=[_per_seq(q.shape), _per_seq(lf_t.shape), _per_seq(knew.shape), _per_seq(vnew.shape)]
                     + _page_specs(cache_k.shape, n_pg, last) + _page_specs(cache_v.shape, n_pg, last)
                     + _page_specs(cache_lf_t.shape, n_pg, last),
            out_specs=pl.BlockSpec((None, 8, 512), lambda b, st, pt: (b, 0, 0)),
            scratch_shapes=[pltpu.VMEM((rows, 1), F32), pltpu.VMEM((rows, 1), F32),
                            pltpu.VMEM((rows, HEAD_DIM), F32), pltpu.VMEM((8, 1), F32)],
        ),
        out_shape=jax.ShapeDtypeStruct((bd, 8, 512), F32),
        compiler_params=_cparams("parallel", "arbitrary"),
        name="fox_sample",
    )(page_table, q, lf_t, knew, vnew, *([cache_k] * n_pg), *([cache_v] * n_pg), *([cache_lf_t] * n_pg))


def _rope_tables(pos):
    rot = HEAD_DIM // 4
    half = rot // 2
    inv = ROPE_THETA ** (-np.arange(half, dtype=np.float64) / half)
    ang = np.asarray(pos, np.float64)[:, None] * inv[None, :]
    lane = np.arange(LANES) % HEAD_DIM
    c = np.ones((len(pos), LANES), np.float64)
    s1 = np.zeros_like(c)
    s2 = np.zeros_like(c)
    first, second = lane < half, (lane >= half) & (lane < rot)
    c[:, first] = np.cos(ang)[:, lane[first]]
    c[:, second] = np.cos(ang)[:, lane[second] - half]
    s1[:, first] = -np.sin(ang)[:, lane[first]]
    s2[:, second] = np.sin(ang)[:, lane[second] - half]
    return tuple(jnp.asarray(a, F32) for a in (c, s1, s2))


def _pack_w_ab(w):
    q, k, v, qi, ki, wi, u, gv = jnp.split(w, np.cumsum(
        (512, 128, 128, 512, IDX_DIM, IDX_HEADS, GM_WIDTH, GM_WIDTH))[:-1].tolist(), axis=1)
    q = q.reshape(D_MODEL, A_HEADS, HEAD_DIM)[:, np.array(A_HEAD_PERM)].reshape(D_MODEL, 512)
    pad = jnp.zeros((D_MODEL, LANES - IDX_DIM - IDX_HEADS), w.dtype)
    return jnp.concatenate([q, k, v, qi, ki, wi, pad, u, gv], axis=1).astype(BF16)


def _pack_w_cd(w):
    pad = jnp.zeros((D_MODEL, CD_COLS - w.shape[1]), w.dtype)
    return jnp.concatenate([w, pad], axis=1).astype(BF16)


def _spatial_blockdiag(w_s, b_s, tm, rows_per_chunk):
    wt = (w_s * jnp.tril(jnp.ones((CHUNK, CHUNK), w_s.dtype)))[:, :rows_per_chunk, :rows_per_chunk]
    n_blk = tm // rows_per_chunk
    eye = jnp.eye(n_blk, dtype=w_s.dtype)
    wsp = jnp.einsum("ab,gts->gatbs", eye, wt).reshape(GM_GROUPS, tm, tm).astype(BF16)
    bias = jnp.repeat(b_s[:, :rows_per_chunk].T, HEAD_DIM, axis=1)
    return wsp, jnp.tile(bias, (n_blk, 1))


def _pad_rows(x, rows):
    return jnp.pad(x, ((0, 0), (0, rows - x.shape[1]), (0, 0)))


def _cache_out(xt, lead, tail):
    n_seq, _, length = xt.shape
    x = jnp.moveaxis(xt.reshape((n_seq,) + tail + (length,)), -1, 1)
    return x.reshape((1,) + lead + tail)


def _position_minor(cache):
    return jnp.moveaxis(cache, 1, -1)


def _group_minor(x, bd, t_new, n_groups, width):
    x = x.reshape(bd, t_new, n_groups, width).transpose(0, 2, 3, 1)
    return jnp.pad(x, ((0, 0), (0, 0), (0, 0), (0, PAGE_SIZE - t_new)))


def _group_major(x, bd, t_new, n_groups, width, pad_to=None):
    x = x.reshape(bd, t_new, n_groups, width).transpose(0, 2, 1, 3)
    if pad_to is not None:
        x = jnp.pad(x, ((0, 0), (0, 0), (0, pad_to - t_new), (0, 0)))
    return x


def kernel(x_prompt, x_sample, cache_a_k, cache_a_v, cache_a_idx, cache_c_k, cache_c_v, cache_d_k, cache_d_v,
           cache_d_logf, page_table, c_prompt, c_sample, w_mod, b_mod, g_norm, w_in_ab, w_out_ab, gm_norm_v,
           gm_spatial_w, gm_spatial_b, w_in_cd, w_out_cd, fox_bias, lam_q1, lam_k1, lam_q2, lam_k2, g_subln,
           w_gate, w_up, w_down):
    b, s_len, _ = x_prompt.shape
    bd, t_new, _ = x_sample.shape
    n_pages = page_table.shape[1]
    past_len = n_pages * PAGE_SIZE
    n_pool = cache_a_k.shape[1]
    assert t_new == 8 and s_len % CHUNK == 0 and (bd * t_new) % 8 == 0
    mp, ms = b * s_len, bd * t_new

    mod = _modulation(jnp.concatenate([c_prompt, c_sample], axis=0), w_mod, b_mod)
    mod = mod.reshape(mod.shape[0], b + bd, 6, D_MODEL)

    def mods(layer):
        p = [mod[layer, :b, i].reshape(b, 1, D_MODEL) for i in range(6)]
        s = [jnp.repeat(mod[layer, b:, i], t_new, axis=0) for i in range(6)]
        return p, s

    tps_p = lambda tm: s_len // tm
    tps_s = lambda tm: 1
    rope_p = _rope_tables(np.arange(s_len))
    rope_s = _rope_tables(np.tile(past_len + np.arange(t_new), bd))
    gvec = lambda layer, i: g_norm[layer, i].reshape(1, D_MODEL)

    xp = x_prompt.reshape(mp, D_MODEL)
    xs = x_sample.reshape(ms, D_MODEL)
    pt = page_table.astype(jnp.int32)

    li = 0
    mod_p, mod_s = mods(0)
    w_ab = _pack_w_ab(w_in_ab[li])
    gm_g = gm_norm_v[li].reshape(1, GM_WIDTH)
    tm_p, tm_s = min(ROW_TILE, mp), min(ROW_TILE, ms)
    wsp_p, bsp_p = _spatial_blockdiag(gm_spatial_w[li], gm_spatial_b[li], tm_p, CHUNK)
    wsp_s, bsp_s = _spatial_blockdiag(gm_spatial_w[li], gm_spatial_b[li], tm_s, t_new)
    wo = w_out_ab[li]
    wo_a = wo[:512].reshape(A_HEADS, HEAD_DIM, D_MODEL)[np.array(A_HEAD_PERM)].reshape(512, D_MODEL).astype(BF16)
    wo_b = wo[512:].astype(BF16)
    n_sel_p = min(TOPK_MAX, s_len // 4)
    n_sel_s = min(TOPK_MAX, (past_len + t_new) // 4)

    (q_p, k32_p, v32_p, kiw_p, kbf_p, vbf_p, qi_p, kid_p, ob_p) = _inproj_ab(
        xp, gvec(0, 0), mod_p[0], mod_p[1], w_ab, rope_p, gm_g, wsp_p, bsp_p, tps_p(tm_p), False)
    (q_s, k32_s, v32_s, kiw_s, kbf_s, vbf_s, qi_s, kid_s, ob_s, gv_s) = _inproj_ab(
        xs, gvec(0, 0), mod_s[0], mod_s[1], w_ab, rope_s, gm_g, wsp_s, bsp_s, 1, True)

    r3 = lambda a: a.reshape(b, s_len, a.shape[-1])
    wi_rows = lambda kiw: kiw[:, IDX_DIM:IDX_DIM + IDX_HEADS].transpose(0, 2, 1).reshape(-1, IDX_HEADS)
    oa_p = _dsa_prompt(r3(qi_p), r3(wi_rows(kiw_p)), r3(kid_p), r3(q_p), r3(kbf_p), r3(vbf_p), n_sel_p)

    qi_stk = qi_s.reshape(bd, t_new, IDX_HEADS, IDX_DIM).transpose(0, 2, 1, 3).reshape(bd, 64, IDX_DIM)
    wi_col = wi_rows(kiw_s).reshape(bd, t_new, IDX_HEADS).transpose(0, 2, 1).reshape(bd, 64, 1)
    ki_new = _group_minor(kid_s[:, :IDX_DIM], bd, t_new, 1, IDX_DIM)[:, 0]
    sel_bias = _dsa_select(pt, qi_stk, wi_col, ki_new, _position_minor(cache_a_idx[li]), n_sel_s)

    q_stk = q_s.reshape(bd, t_new, 4, 2, HEAD_DIM).transpose(0, 3, 2, 1, 4)
    q_stk = q_stk.reshape(bd, A_KV_HEADS, 4 * t_new, HEAD_DIM)
    o_s = _dsa_sample(pt, q_stk, sel_bias,
                      _group_minor(kbf_s, bd, t_new, A_KV_HEADS, HEAD_DIM),
                      _group_minor(vbf_s, bd, t_new, A_KV_HEADS, HEAD_DIM),
                      _position_minor(cache_a_k[li]), _position_minor(cache_a_v[li]))
    oa_s = o_s.reshape(ms, 512).astype(BF16)

    def finish_layer(layer, a, b_, wa, wb, x, md, tps):
        wg, wu, wd = w_gate[layer].astype(BF16), w_up[layer].astype(BF16), w_down[layer].astype(BF16)
        return _mixer_out_ffn(a, b_, wa, wb, x, md[2], gvec(layer, 1), gvec(layer, 2), md[3], md[4],
                              wg, wu, wd, md[5], gvec(layer, 3), tps)

    xp = finish_layer(0, oa_p.reshape(mp, 512), ob_p, wo_a, wo_b, xp, mod_p, tps_p)
    xs = finish_layer(0, oa_s, ob_s, wo_a, wo_b, xs, mod_s, tps_s)

    gp, gs = (b, s_len), (bd, t_new)
    a_k_p = _cache_out(k32_p, gp, (A_KV_HEADS, HEAD_DIM))
    a_v_p = _cache_out(v32_p, gp, (A_KV_HEADS, HEAD_DIM))
    a_idx_p = _cache_out(kiw_p[:, :IDX_DIM], gp, (IDX_DIM,))
    a_k_s = _cache_out(k32_s, gs, (A_KV_HEADS, HEAD_DIM))
    a_v_s = _cache_out(v32_s, gs, (A_KV_HEADS, HEAD_DIM))
    a_idx_s = _cache_out(kiw_s[:, :IDX_DIM], gs, (IDX_DIM,))
    gm_v_s = gv_s.reshape(1, bd, t_new, GM_WIDTH)

    layer = 1
    mod_p, mod_s = mods(layer)
    lam_init = 0.8 - 0.6 * math.exp(-0.3 * layer)
    w_cd = _pack_w_cd(w_in_cd[li])
    fox_b = jnp.pad(fox_bias[li], (0, LANES - D_HEADS)).reshape(1, LANES)
    lamv = jnp.stack([lam_q1[li], lam_k1[li], lam_q2[li], lam_k2[li]])
    gsub = g_subln[li].reshape(1, 2 * HEAD_DIM)
    wo = w_out_cd[li]
    wo_c, wo_d = wo[:512].astype(BF16), wo[512:].astype(BF16)

    (cq_p, ck32_p, cv32_p, ckbf_p, cvbf_p, dq_p, dk32_p, dv32_p, dkbf_p, dvbf_p, lf_p) = _inproj_cd(
        xp, gvec(1, 0), mod_p[0], mod_p[1], w_cd, rope_p, fox_b, tps_p(tm_p))
    (cq_s, ck32_s, cv32_s, ckbf_s, cvbf_s, dq_s, dk32_s, dv32_s, dkbf_s, dvbf_s, lf_s) = _inproj_cd(
        xs, gvec(1, 0), mod_s[0], mod_s[1], w_cd, rope_s, fox_b, 1)

    oc_p = _diff_prompt(lamv, gsub, r3(cq_p), r3(ckbf_p), r3(cvbf_p), lam_init)
    cum_t = _cumsum_lanes(lf_p[:, :D_HEADS])
    od_p = _fox_prompt(r3(dq_p), r3(dkbf_p), r3(dvbf_p), cum_t.transpose(0, 2, 1), cum_t)

    oc_s = _diff_sample(pt, lamv, gsub, _group_major(cq_s, bd, t_new, 2 * C_HEADS, HEAD_DIM),
                        _group_minor(ckbf_s, bd, t_new, 2 * C_HEADS, HEAD_DIM),
                        _group_major(cvbf_s, bd, t_new, C_HEADS, 2 * HEAD_DIM, PAGE_SIZE),
                        _position_minor(cache_c_k[li]),
                        cache_c_v[li].reshape(-1, PAGE_SIZE * C_HEADS, 2 * HEAD_DIM), lam_init)

    d_logf_s = _cache_out(lf_s[:, :D_HEADS], gs, (D_HEADS,))
    logf_s = d_logf_s[0]
    lf_t = jnp.pad(logf_s.transpose(0, 2, 1), ((0, 0), (0, 0), (0, LANES - t_new)))
    od_s = _fox_sample(pt, _group_major(dq_s, bd, t_new, D_HEADS, HEAD_DIM), lf_t,
                       _group_minor(dkbf_s, bd, t_new, D_HEADS, HEAD_DIM),
                       _group_minor(dvbf_s, bd, t_new, D_HEADS, HEAD_DIM),
                       _position_minor(cache_d_k[li]), _position_minor(cache_d_v[li]),
                       _position_minor(cache_d_logf[li]))

    xp = finish_layer(1, oc_p.reshape(mp, 512), od_p.reshape(mp, 512), wo_c, wo_d, xp, mod_p, tps_p)
    xs = finish_layer(1, oc_s.reshape(ms, 512).astype(BF16), od_s.reshape(ms, 512).astype(BF16), wo_c, wo_d, xs,
                      mod_s, tps_s)

    return (xp.reshape(b, s_len, D_MODEL), xs.reshape(bd, t_new, D_MODEL),
            a_k_p, a_v_p, a_idx_p,
            _cache_out(ck32_p, gp, (C_HEADS, 2, HEAD_DIM)), cv32_p.reshape((1,) + gp + (C_HEADS, 2 * HEAD_DIM)),
            _cache_out(dk32_p, gp, (D_HEADS, HEAD_DIM)), _cache_out(dv32_p, gp, (D_HEADS, HEAD_DIM)),
            _cache_out(lf_p[:, :D_HEADS], gp, (D_HEADS,)),
            a_k_s, a_v_s, a_idx_s,
            _cache_out(ck32_s, gs, (C_HEADS, 2, HEAD_DIM)), cv32_s.reshape((1,) + gs + (C_HEADS, 2 * HEAD_DIM)),
            _cache_out(dk32_s, gs, (D_HEADS, HEAD_DIM)), _cache_out(dv32_s, gs, (D_HEADS, HEAD_DIM)),
            d_logf_s, gm_v_s)
```

```python
import functools
import math

import numpy as np
import jax
import jax.numpy as jnp
from jax import lax
from jax.experimental import pallas as pl
from jax.experimental.pallas import tpu as pltpu

F32 = jnp.float32
BF16 = jnp.bfloat16
NEG_INF = float("-inf")

D_MODEL = 1024
HEAD_DIM = 64
ROPE_THETA = 500000.0
A_HEADS = 8
A_KV_HEADS = 2
IDX_HEADS = 8
IDX_DIM = 64
TOPK_MAX = 256
GM_GROUPS = 8
GM_WIDTH = GM_GROUPS * HEAD_DIM
CHUNK = 128
C_HEADS = 4
D_HEADS = 8
PAGE_SIZE = 128
EPS = 1e-6
D_FF = -(-(8 * D_MODEL) // (3 * 256)) * 256
LOG2E = math.log2(math.e)
Q_SCALE = HEAD_DIM ** -0.5 * LOG2E

LANES = 128
VMEM_LIMIT_BYTES = 56 * 1024 * 1024

ROW_TILE = 256
FFN_ROW_TILE = 512
Q_TILE = 256
CAUSAL_SEGMENTS = 8
SELECT_PAGES_PER_STEP = 64
DSA_PAGES_PER_STEP = 32
CD_PAGES_PER_STEP = 16
MOD_COL_TILE = 1536
BISECT_STEPS = 20

AB_Q, AB_K, AB_V, AB_QI, AB_KIW, AB_U, AB_GV, AB_COLS = 0, 512, 640, 768, 1280, 1408, 1920, 2432
CD_CQ, CD_CK, CD_CV, CD_DQ, CD_DK, CD_DV, CD_DF, CD_COLS = 0, 512, 1024, 1536, 2048, 2560, 3072, 3200
A_HEAD_PERM = (0, 4, 1, 5, 2, 6, 3, 7)


def _cparams(*sem):
    return pltpu.CompilerParams(dimension_semantics=sem, vmem_limit_bytes=VMEM_LIMIT_BYTES)


def _nt(a, b):
    return lax.dot_general(a, b, (((1,), (1,)), ((), ())), preferred_element_type=F32)


def _mm(a, b):
    return jnp.dot(a, b, preferred_element_type=F32)


def _rms(x, g):
    return x * lax.rsqrt(jnp.mean(x * x, axis=-1, keepdims=True) + EPS) * g


def _silu(x):
    return x / (1.0 + jnp.exp(-x))


def _gelu_tanh(x):
    return 0.5 * x * (1.0 + jnp.tanh(math.sqrt(2.0 / math.pi) * (x + 0.044715 * (x * x * x))))


def _half_masks(dtype):
    lane = lax.broadcasted_iota(jnp.int32, (1, LANES), 1)
    lo = (lane < HEAD_DIM).astype(dtype)
    return lo, (1 - lo).astype(dtype)


def _rope(z, c, s1, s2):
    outs = []
    for j in range(z.shape[1] // LANES):
        x = z[:, j * LANES:(j + 1) * LANES]
        outs.append(x * c + pltpu.roll(x, LANES - 8, 1) * s1 + pltpu.roll(x, 8, 1) * s2)
    return outs[0] if len(outs) == 1 else jnp.concatenate(outs, axis=1)


def _split3(x):
    hi = x.astype(BF16)
    r1 = x - hi.astype(F32)
    mid = r1.astype(BF16)
    lo = (r1 - mid.astype(F32)).astype(BF16)
    return hi, mid, lo


def _lane_prefix(x, tri):
    hi, mid, lo = _split3(x)
    r = _mm(jnp.concatenate([hi, mid, lo], axis=0), tri)
    return r[0:8] + r[8:16] + r[16:24]


def _tri(incl_upper=None, strict_lower=None):
    r = lax.broadcasted_iota(jnp.int32, (LANES, LANES), 0)
    c = lax.broadcasted_iota(jnp.int32, (LANES, LANES), 1)
    if incl_upper:
        return (r <= c).astype(BF16)
    return (r > c).astype(BF16)


def _count_ge(sm, x):
    return jnp.sum(jnp.where(sm >= x, 1.0, 0.0), axis=1, keepdims=True)


def _topk_bias(sm_ref, bias_ref, n_sel):
    kf = float(n_sel)
    sm = sm_ref[...]
    allowed = sm > NEG_INF
    mx = jnp.max(sm, axis=1, keepdims=True)
    mn = jnp.min(jnp.where(allowed, sm, jnp.inf), axis=1, keepdims=True)
    n_allowed = jnp.sum(jnp.where(allowed, 1.0, 0.0), axis=1, keepdims=True)
    few = n_allowed <= kf
    top_tie = _count_ge(sm, mx) >= kf

    def bisect(_, c):
        lo, hi = c
        mid = lo + (hi - lo) * 0.5
        ge = _count_ge(sm_ref[...], mid) >= kf
        return jnp.where(ge, mid, lo), jnp.where(ge, hi, mid)

    _, hi = lax.fori_loop(0, BISECT_STEPS, bisect, (mn, mx))

    done0 = jnp.where(few, 1.0, jnp.where(top_tie, 1.0, 0.0))
    thr0 = jnp.where(few, NEG_INF, mx)

    def not_done(c):
        return c[3] > 0.0

    def step(c):
        hi, thr, done, _ = c
        s = sm_ref[...]
        m = jnp.max(jnp.where(s < hi, s, NEG_INF), axis=1, keepdims=True)
        found = _count_ge(s, m) >= kf
        is_done = done > 0.5
        thr = jnp.where(is_done, thr, jnp.where(found, m, thr))
        hi = jnp.where(is_done, hi, jnp.where(found, hi, m))
        done = jnp.where(found, 1.0, done)
        return hi, thr, done, jnp.sum(1.0 - done)

    _, thr, _, _ = lax.while_loop(not_done, step, (hi, thr0, done0, jnp.sum(1.0 - done0)))

    is_gt = sm > thr
    is_eq = jnp.where(sm == thr, jnp.where(allowed, 1.0, 0.0), 0.0)
    need = kf - jnp.sum(jnp.where(is_gt, 1.0, 0.0), axis=1, keepdims=True)
    n_eq = jnp.sum(is_eq, axis=1, keepdims=True)
    bias_ref[...] = jnp.where(sm >= thr, jnp.where(allowed, 0.0, NEG_INF), NEG_INF)

    @pl.when(jnp.max(n_eq - need) > 0.0)
    def _():
        tri = _tri(incl_upper=True)

        def blk(j, run):
            cols = pl.ds(pl.multiple_of(j * LANES, LANES), LANES)
            s = sm_ref[:, cols]
            e = jnp.where(s == thr, jnp.where(s > NEG_INF, 1.0, 0.0), 0.0)
            rank = run + _mm(e.astype(BF16), tri)
            keep = jnp.where(e > 0.5, jnp.where(rank <= need, 1.0, 0.0), 0.0)
            bias_ref[:, cols] = jnp.where(s > thr, 0.0, jnp.where(keep > 0.5, 0.0, NEG_INF))
            return run + jnp.sum(e, axis=1, keepdims=True)

        lax.fori_loop(0, sm_ref.shape[1] // LANES, blk, jnp.zeros_like(need))


def _mod_kernel(c_ref, w_ref, b_ref, o_ref):
    a = _silu(c_ref[...]).astype(BF16)
    o_ref[...] = _mm(a, w_ref[...].astype(BF16)) + b_ref[...]


def _modulation(c_all, w_mod, b_mod):
    n_layers, _, n_out = w_mod.shape
    rows = c_all.shape[0]
    tn = MOD_COL_TILE
    return pl.pallas_call(
        _mod_kernel,
        grid=(n_layers, n_out // tn),
        in_specs=[
            pl.BlockSpec((rows, D_MODEL), lambda l, n: (0, 0)),
            pl.BlockSpec((None, D_MODEL, tn), lambda l, n: (l, 0, n)),
            pl.BlockSpec((None, 1, tn), lambda l, n: (l, 0, n)),
        ],
        out_specs=pl.BlockSpec((None, rows, tn), lambda l, n: (l, 0, n)),
        out_shape=jax.ShapeDtypeStruct((n_layers, rows, n_out), F32),
        compiler_params=_cparams("arbitrary", "arbitrary"),
        name="modulation",
    )(c_all, w_mod, b_mod.reshape(n_layers, 1, n_out))


def _rowvec_spec(vec, tm, tiles_per_seq):
    if vec.ndim == 3:
        return pl.BlockSpec((None, 1, vec.shape[-1]), lambda i: (i // tiles_per_seq, 0, 0))
    return pl.BlockSpec((tm, vec.shape[-1]), lambda i: (i, 0))


def _const_spec(arr, single_buffer=False):
    nd = arr.ndim
    if single_buffer:
        return pl.BlockSpec(arr.shape, lambda i: (0,) * nd, pipeline_mode=pl.Buffered(1))
    return pl.BlockSpec(arr.shape, lambda i: (0,) * nd)


def _position_minor_out(m, tm, tiles_per_seq):
    seq_len = tm * tiles_per_seq
    spec = lambda width: pl.BlockSpec((None, width, tm), lambda i: (i // tiles_per_seq, 0, i % tiles_per_seq))
    shape = lambda width: jax.ShapeDtypeStruct((m // seq_len, width, seq_len), F32)
    return spec, shape


def _inproj_ab_kernel(x_ref, g_ref, sh_ref, sc_ref, w_ref, rc_ref, rs1_ref, rs2_ref, gmg_ref, wsp_ref, bsp_ref,
                      q_ref, k32_ref, v32_ref, kiw_ref, kbf_ref, vbf_ref, qi_ref, kid_ref, ob_ref, *gv_ref):
    h = (_rms(x_ref[...], g_ref[...]) * (1.0 + sc_ref[...]) + sh_ref[...]).astype(BF16)
    c, s1, s2 = rc_ref[...], rs1_ref[...], rs2_ref[...]

    def seg(lo, hi):
        return _mm(h, w_ref[:, lo:hi])

    q_ref[...] = (_rope(seg(AB_Q, AB_K), c, s1, s2) * Q_SCALE).astype(BF16)
    k = _rope(seg(AB_K, AB_V), c, s1, s2)
    k32_ref[...] = k.T
    kbf_ref[...] = k.astype(BF16)
    v = seg(AB_V, AB_QI)
    v32_ref[...] = v.T
    vbf_ref[...] = v.astype(BF16)
    qi_ref[...] = _rope(seg(AB_QI, AB_KIW), c, s1, s2).astype(BF16)
    kz = seg(AB_KIW, AB_U)
    kr = _rope(kz, c, s1, s2)
    lane = lax.broadcasted_iota(jnp.int32, kz.shape, 1)
    kiw_ref[...] = jnp.where(lane < IDX_DIM, kr, kz).T
    kid_ref[...] = jnp.where(lane < IDX_DIM, kr, pltpu.roll(kr, IDX_DIM, 1)).astype(BF16)

    u = _gelu_tanh(seg(AB_U, AB_GV))
    gv = _gelu_tanh(seg(AB_GV, AB_COLS))
    gc = gv - jnp.mean(gv, axis=-1, keepdims=True)
    gvn = gc * lax.rsqrt(jnp.mean(gc * gc, axis=-1, keepdims=True) + EPS) * gmg_ref[...]
    if gv_ref:
        gv_ref[0][...] = gvn
    gvb = gvn.astype(BF16)
    lane = lax.broadcasted_iota(jnp.int32, (1, LANES), 1)
    mix = []
    for j in range(GM_GROUPS // 2):
        slab = gvb[:, j * LANES:(j + 1) * LANES]
        mix.append(jnp.where(lane < HEAD_DIM, _mm(wsp_ref[2 * j], slab), _mm(wsp_ref[2 * j + 1], slab)))
    ob_ref[...] = (u * (jnp.concatenate(mix, axis=1) + bsp_ref[...])).astype(BF16)


def _inproj_ab(x, g, shift, scale, w, rope_tabs, gm_g, wsp, bsp, tiles_per_seq, want_gv):
    m = x.shape[0]
    tm = min(ROW_TILE, m)
    n_tab_tiles = rope_tabs[0].shape[0] // tm
    row = lambda width: pl.BlockSpec((tm, width), lambda i: (i, 0))
    tab = pl.BlockSpec((tm, LANES), lambda i: (i % n_tab_tiles, 0))
    sds = lambda width, dt: jax.ShapeDtypeStruct((m, width), dt)
    col, col_sds = _position_minor_out(m, tm, tiles_per_seq)
    out_shape = [sds(512, BF16), col_sds(128), col_sds(128), col_sds(128), sds(128, BF16), sds(128, BF16),
                 sds(512, BF16), sds(128, BF16), sds(512, BF16)]
    out_specs = [row(512), col(128), col(128), col(128), row(128), row(128), row(512), row(128), row(512)]
    if want_gv:
        out_shape.append(sds(512, F32))
        out_specs.append(row(512))
    return pl.pallas_call(
        _inproj_ab_kernel,
        grid=(m // tm,),
        in_specs=[row(D_MODEL), _const_spec(g), _rowvec_spec(shift, tm, tiles_per_seq),
                  _rowvec_spec(scale, tm, tiles_per_seq), _const_spec(w), tab, tab, tab,
                  _const_spec(gm_g), _const_spec(wsp), _const_spec(bsp)],
        out_specs=out_specs,
        out_shape=out_shape,
        compiler_params=_cparams("parallel"),
        name="inproj_ab",
    )(x, g, shift, scale, w, *rope_tabs, gm_g, wsp, bsp)


def _inproj_cd_kernel(x_ref, g_ref, sh_ref, sc_ref, w_ref, rc_ref, rs1_ref, rs2_ref, fb_ref,
                      cq_ref, ck32_ref, cv32_ref, ckbf_ref, cvbf_ref,
                      dq_ref, dk32_ref, dv32_ref, dkbf_ref, dvbf_ref, lf_ref):
    h = (_rms(x_ref[...], g_ref[...]) * (1.0 + sc_ref[...]) + sh_ref[...]).astype(BF16)
    c, s1, s2 = rc_ref[...], rs1_ref[...], rs2_ref[...]

    def seg(lo, hi):
        return _mm(h, w_ref[:, lo:hi])

    cq_ref[...] = (_rope(seg(CD_CQ, CD_CK), c, s1, s2) * Q_SCALE).astype(BF16)
    ck = _rope(seg(CD_CK, CD_CV), c, s1, s2)
    ck32_ref[...] = ck.T
    ckbf_ref[...] = ck.astype(BF16)
    cv = seg(CD_CV, CD_DQ)
    for hd in range(C_HEADS):
        cv32_ref[pl.ds(hd, cv.shape[0], stride=C_HEADS), :] = cv[:, hd * LANES:(hd + 1) * LANES]
    cvbf_ref[...] = cv.astype(BF16)
    dq_ref[...] = (seg(CD_DQ, CD_DK) * Q_SCALE).astype(BF16)
    dk = seg(CD_DK, CD_DV)
    dk32_ref[...] = dk.T
    dkbf_ref[...] = dk.astype(BF16)
    dv = seg(CD_DV, CD_DF)
    dv32_ref[...] = dv.T
    dvbf_ref[...] = dv.astype(BF16)
    f = seg(CD_DF, CD_COLS) + fb_ref[...]
    lf_ref[...] = (jnp.minimum(f, 0.0) - jnp.log(1.0 + jnp.exp(-jnp.abs(f)))).T


def _inproj_cd(x, g, shift, scale, w, rope_tabs, fox_b, tiles_per_seq):
    m = x.shape[0]
    tm = min(ROW_TILE, m)
    n_tab_tiles = rope_tabs[0].shape[0] // tm
    row = lambda width: pl.BlockSpec((tm, width), lambda i: (i, 0))
    tab = pl.BlockSpec((tm, LANES), lambda i: (i % n_tab_tiles, 0))
    sds = lambda width, dt: jax.ShapeDtypeStruct((m, width), dt)
    col, col_sds = _position_minor_out(m, tm, tiles_per_seq)
    return pl.pallas_call(
        _inproj_cd_kernel,
        grid=(m // tm,),
        in_specs=[row(D_MODEL), _const_spec(g), _rowvec_spec(shift, tm, tiles_per_seq),
                  _rowvec_spec(scale, tm, tiles_per_seq), _const_spec(w), tab, tab, tab, _const_spec(fox_b)],
        out_specs=[row(512), col(512), pl.BlockSpec((tm * C_HEADS, LANES), lambda i: (i, 0)), row(512), row(512),
                   row(512), col(512), col(512), row(512), row(512), col(128)],
        out_shape=[sds(512, BF16), col_sds(512), jax.ShapeDtypeStruct((m * C_HEADS, LANES), F32),
                   sds(512, BF16), sds(512, BF16),
                   sds(512, BF16), col_sds(512), col_sds(512), sds(512, BF16), sds(512, BF16), col_sds(128)],
        compiler_params=_cparams("parallel"),
        name="inproj_cd",
    )(x, g, shift, scale, w, *rope_tabs, fox_b)


def _mixer_out_ffn_kernel(a_ref, b_ref, wa_ref, wb_ref, x_ref, gate1_ref, g1_ref,
                          g2_ref, sh_ref, sc_ref, wg_ref, wu_ref, wd_ref, gate2_ref, g3_ref, o_ref):
    op = _mm(a_ref[...], wa_ref[...]) + _mm(b_ref[...], wb_ref[...])
    x = x_ref[...] + gate1_ref[...] * _rms(op, g1_ref[...])
    h = (_rms(x, g2_ref[...]) * (1.0 + sc_ref[...]) + sh_ref[...]).astype(BF16)
    act = (_silu(_mm(h, wg_ref[...])) * _mm(h, wu_ref[...])).astype(BF16)
    o_ref[...] = x + gate2_ref[...] * _rms(_mm(act, wd_ref[...]), g3_ref[...])


def _mixer_out_ffn(a, b, wa, wb, x, gate1, g1, g2, shift, scale, wg, wu, wd, gate2, g3, tiles_per_seq_of):
    m = x.shape[0]
    tm = min(FFN_ROW_TILE, m)
    tps = tiles_per_seq_of(tm)
    row = lambda width: pl.BlockSpec((tm, width), lambda i: (i, 0))
    vec = lambda v: _rowvec_spec(v, tm, tps)
    once = lambda arr: _const_spec(arr, single_buffer=True)
    return pl.pallas_call(
        _mixer_out_ffn_kernel,
        grid=(m // tm,),
        in_specs=[row(a.shape[1]), row(b.shape[1]), once(wa), once(wb), row(D_MODEL), vec(gate1), _const_spec(g1),
                  _const_spec(g2), vec(shift), vec(scale), once(wg), once(wu), once(wd), vec(gate2), _const_spec(g3)],
        out_specs=row(D_MODEL),
        out_shape=jax.ShapeDtypeStruct((m, D_MODEL), F32),
        compiler_params=_cparams("parallel"),
        name="mixer_out_ffn",
    )(a, b, wa, wb, x, gate1, g1, g2, shift, scale, wg, wu, wd, gate2, g3)


def _dsa_prompt_kernel(qi_ref, wi_ref, kid_ref, q_ref, k_ref, v_ref, o_ref, sm_ref, bias_ref, *, n_sel, q_blk0):
    tq, s_len = sm_ref.shape
    i = q_blk0 + pl.program_id(1)
    m_lo, m_hi = _half_masks(BF16)
    kid = kid_ref[...]
    score = jnp.zeros((tq, s_len), F32)
    for j in range(IDX_HEADS // 2):
        slab = qi_ref[:, j * LANES:(j + 1) * LANES]
        s = _nt(jnp.concatenate([slab * m_lo, slab * m_hi], axis=0), kid)
        score = score + jnp.maximum(s[:tq], 0.0) * wi_ref[:, 2 * j:2 * j + 1]
        score = score + jnp.maximum(s[tq:], 0.0) * wi_ref[:, 2 * j + 1:2 * j + 2]
    score = score * (IDX_HEADS * IDX_DIM) ** -0.5
    row = i * tq + lax.broadcasted_iota(jnp.int32, (tq, s_len), 0)
    col = lax.broadcasted_iota(jnp.int32, (tq, s_len), 1)
    sm_ref[...] = jnp.where(col <= row, score, NEG_INF)
    _topk_bias(sm_ref, bias_ref, n_sel)

    k = k_ref[...]
    v = v_ref[...]
    n_slab = A_HEADS // 2
    outs = []
    for mask in (m_lo, m_hi):
        stk = jnp.concatenate([q_ref[:, j * LANES:(j + 1) * LANES] * mask for j in range(n_slab)], axis=0)
        lg = _nt(stk, k)
        per = []
        for j in range(n_slab):
            l = lg[j * tq:(j + 1) * tq] + bias_ref[...]
            p = jnp.exp2(l - jnp.max(l, axis=1, keepdims=True))
            per.append(_mm(p.astype(BF16), v) / jnp.sum(p, axis=1, keepdims=True))
        outs.append(per)
    lane = lax.broadcasted_iota(jnp.int32, (1, LANES), 1)
    for j in range(n_slab):
        o_ref[:, j * LANES:(j + 1) * LANES] = jnp.where(lane < HEAD_DIM, outs[0][j], outs[1][j]).astype(BF16)


def _causal_segments(s_len, tq):
    n_seg = min(CAUSAL_SEGMENTS, s_len // tq)
    per = s_len // tq // n_seg
    return [(seg * per, per, (seg + 1) * per * tq) for seg in range(n_seg)]


def _dsa_prompt(qi, wi, kid, q, k, v, n_sel):
    b, s_len, _ = q.shape
    tq = min(Q_TILE, s_len)
    outs = []
    for q_blk0, n_blk, kext in _causal_segments(s_len, tq):
        qblk = lambda width: pl.BlockSpec((None, tq, width), lambda bb, i: (bb, q_blk0 + i, 0))
        keys = lambda width: pl.BlockSpec((None, kext, width), lambda bb, i: (bb, 0, 0))
        outs.append(pl.pallas_call(
            functools.partial(_dsa_prompt_kernel, n_sel=n_sel, q_blk0=q_blk0),
            grid=(b, n_blk),
            in_specs=[qblk(512), qblk(IDX_HEADS), keys(128), qblk(512), keys(128), keys(128)],
            out_specs=pl.BlockSpec((None, tq, 512), lambda bb, i: (bb, i, 0)),
            out_shape=jax.ShapeDtypeStruct((b, n_blk * tq, 512), BF16),
            scratch_shapes=[pltpu.VMEM((tq, kext), F32), pltpu.VMEM((tq, kext), F32)],
            compiler_params=_cparams("parallel", "arbitrary"),
            name="dsa_prompt",
        )(qi, wi, kid, q, k, v))
    return jnp.concatenate(outs, axis=1)


def _diff_lambda(lamv_ref, lam_init):
    lv = lamv_ref[...]
    d1 = jnp.sum(lv[0:1] * lv[1:2], axis=1, keepdims=True)
    d2 = jnp.sum(lv[2:3] * lv[3:4], axis=1, keepdims=True)
    return jnp.exp(d1) - jnp.exp(d2) + lam_init


def _causal_softmax_pv(lg, v, row0, tq, seg_rows):
    r, kext = lg.shape
    split = kext - seg_rows
    row = row0 + lax.broadcasted_iota(jnp.int32, (r, seg_rows), 0) % tq
    col = split + lax.broadcasted_iota(jnp.int32, (r, seg_rows), 1)
    hi = jnp.where(col <= row, lg[:, split:], NEG_INF)
    m = jnp.max(hi, axis=1, keepdims=True)
    if split:
        lo = lg[:, :split]
        m = jnp.maximum(m, jnp.max(lo, axis=1, keepdims=True))
    p = jnp.exp2(hi - m)
    den = jnp.sum(p, axis=1, keepdims=True)
    pv = _mm(p.astype(BF16), v[split:])
    if split:
        p = jnp.exp2(lo - m)
        den = den + jnp.sum(p, axis=1, keepdims=True)
        pv = pv + _mm(p.astype(BF16), v[:split])
    return pv / den


def _diff_prompt_kernel(lamv_ref, gsub_ref, q_ref, k_ref, v_ref, o_ref, *, lam_init, q_blk0, seg_rows):
    tq = q_ref.shape[0]
    i = q_blk0 + pl.program_id(1)
    lam = _diff_lambda(lamv_ref, lam_init)
    m_lo, m_hi = _half_masks(BF16)
    for h in range(C_HEADS):
        cols = slice(h * LANES, (h + 1) * LANES)
        slab = q_ref[:, cols]
        lg = _nt(jnp.concatenate([slab * m_lo, slab * m_hi], axis=0), k_ref[:, cols])
        pv = _causal_softmax_pv(lg, v_ref[:, cols], i * tq, tq, seg_rows)
        o = pv[:tq] - lam * pv[tq:]
        o_ref[:, cols] = (_rms(o, gsub_ref[...]) * (1.0 - lam_init)).astype(BF16)


def _diff_prompt(lamv, gsub, q, k, v, lam_init):
    b, s_len, _ = q.shape
    tq = min(Q_TILE, s_len)
    cst = lambda arr: pl.BlockSpec(arr.shape, lambda bb, i: (0, 0))
    outs = []
    for q_blk0, n_blk, kext in _causal_segments(s_len, tq):
        qblk = pl.BlockSpec((None, tq, 512), lambda bb, i: (bb, q_blk0 + i, 0))
        keys = pl.BlockSpec((None, kext, 512), lambda bb, i: (bb, 0, 0))
        outs.append(pl.pallas_call(
            functools.partial(_diff_prompt_kernel, lam_init=lam_init, q_blk0=q_blk0, seg_rows=n_blk * tq),
            grid=(b, n_blk),
            in_specs=[cst(lamv), cst(gsub), qblk, keys, keys],
            out_specs=pl.BlockSpec((None, tq, 512), lambda bb, i: (bb, i, 0)),
            out_shape=jax.ShapeDtypeStruct((b, n_blk * tq, 512), BF16),
            compiler_params=_cparams("parallel", "arbitrary"),
            name="diff_prompt",
        )(lamv, gsub, q, k, v))
    return jnp.concatenate(outs, axis=1)


def _cumsum_kernel(x_ref, o_ref):
    tri = _tri(incl_upper=True)
    carry = jnp.zeros((8, 1), F32)
    for c in range(x_ref.shape[1] // LANES):
        cols = slice(c * LANES, (c + 1) * LANES)
        out = _lane_prefix(x_ref[:, cols], tri) + carry
        o_ref[:, cols] = out * LOG2E
        carry = out[:, LANES - 1:LANES]


def _cumsum_lanes(x):
    b, h, s_len = x.shape
    blk = pl.BlockSpec((None, h, s_len), lambda bb: (bb, 0, 0))
    return pl.pallas_call(
        _cumsum_kernel,
        grid=(b,),
        in_specs=[blk],
        out_specs=blk,
        out_shape=jax.ShapeDtypeStruct(x.shape, F32),
        compiler_params=_cparams("parallel"),
        name="fox_cumsum",
    )(x)


def _fox_prompt_kernel(q_ref, k_ref, v_ref, cq_ref, ck_ref, o_ref, *, q_blk0, seg_rows):
    tq = q_ref.shape[0]
    i = q_blk0 + pl.program_id(1)
    m_lo, m_hi = _half_masks(BF16)
    lane = lax.broadcasted_iota(jnp.int32, (1, LANES), 1)
    for j in range(D_HEADS // 2):
        cols = slice(j * LANES, (j + 1) * LANES)
        slab = q_ref[:, cols]
        lg = _nt(jnp.concatenate([slab * m_lo, slab * m_hi], axis=0), k_ref[:, cols])
        v = v_ref[:, cols]
        pv = []
        for hh in range(2):
            h = 2 * j + hh
            l = lg[hh * tq:(hh + 1) * tq] + cq_ref[:, h:h + 1] - ck_ref[h:h + 1, :]
            pv.append(_causal_softmax_pv(l, v, i * tq, tq, seg_rows))
        o_ref[:, cols] = jnp.where(lane < HEAD_DIM, pv[0], pv[1]).astype(BF16)


def _fox_prompt(q, k, v, cum, cum_t):
    b, s_len, _ = q.shape
    tq = min(Q_TILE, s_len)
    outs = []
    for q_blk0, n_blk, kext in _causal_segments(s_len, tq):
        qblk = lambda width: pl.BlockSpec((None, tq, width), lambda bb, i: (bb, q_blk0 + i, 0))
        keys = pl.BlockSpec((None, kext, 512), lambda bb, i: (bb, 0, 0))
        outs.append(pl.pallas_call(
            functools.partial(_fox_prompt_kernel, q_blk0=q_blk0, seg_rows=n_blk * tq),
            grid=(b, n_blk),
            in_specs=[qblk(512), keys, keys, qblk(D_HEADS),
                      pl.BlockSpec((None, D_HEADS, kext), lambda bb, i: (bb, 0, 0))],
            out_specs=pl.BlockSpec((None, tq, 512), lambda bb, i: (bb, i, 0)),
            out_shape=jax.ShapeDtypeStruct((b, n_blk * tq, 512), BF16),
            compiler_params=_cparams("parallel", "arbitrary"),
            name="fox_prompt",
        )(q, k, v, cum, cum_t))
    return jnp.concatenate(outs, axis=1)


def _pages_per_step(n_pages, cap):
    return max(d for d in range(1, min(cap, n_pages) + 1) if n_pages % d == 0)


def _page_specs(block, n_pg, reverse_from=None):
    specs = []
    for pg in range(n_pg):
        if reverse_from is None:
            idx = lambda b, st, pt, pg=pg: (pt[b, st * n_pg + pg],) + (0,) * (len(block) - 1)
        else:
            idx = lambda b, st, pt, pg=pg: (pt[b, reverse_from - (st * n_pg + pg)],) + (0,) * (len(block) - 1)
        specs.append(pl.BlockSpec((None,) + tuple(block[1:]), idx))
    return specs


def _per_seq(shape):
    nd = len(shape)
    return pl.BlockSpec((None,) + tuple(shape[1:]), lambda b, st, pt: (b,) + (0,) * (nd - 1))


def _softmax_init(m_ref, l_ref, acc_ref):
    m_ref[...] = jnp.full(m_ref.shape, NEG_INF, F32)
    l_ref[...] = jnp.zeros(l_ref.shape, F32)
    acc_ref[...] = jnp.zeros(acc_ref.shape, F32)


def _attend(q_ref, keys_t_of, values_of, n_v, values_t, bias, m_ref, l_ref, acc_ref):
    n_k, r, _ = q_ref.shape
    pv = _nt if values_t else _mm
    lg = jnp.concatenate([_mm(q_ref[i], keys_t_of(i)) for i in range(n_k)], axis=0) + bias
    m_old = m_ref[...]
    m_new = jnp.maximum(m_old, jnp.max(lg, axis=1, keepdims=True))
    m_safe = jnp.where(m_new == NEG_INF, 0.0, m_new)
    alpha = jnp.exp2(m_old - m_safe)
    p = jnp.exp2(lg - m_safe)
    l_ref[...] = alpha * l_ref[...] + jnp.sum(p, axis=1, keepdims=True)
    m_ref[...] = m_new
    p = p.astype(BF16)
    rv = r * (n_k // n_v)
    for j in range(n_v):
        rows = slice(j * rv, (j + 1) * rv)
        acc_ref[rows, :] = alpha[rows] * acc_ref[rows, :] + pv(p[rows], values_of(j))


def _page_lanes(pages, *idx):
    return jnp.concatenate([(p[idx] if idx else p[...]).astype(BF16) for p in pages], axis=1)


def _rows8(x):
    return jnp.concatenate([jnp.broadcast_to(x[h:h + 1], (8, x.shape[1])) for h in range(x.shape[0])], axis=0)


def _head_sum(w, n_groups):
    out = w[0:8]
    for h in range(1, n_groups):
        out = out + w[8 * h:8 * h + 8]
    return out


def _new_key_mask(rows):
    t = lax.broadcasted_iota(jnp.int32, (rows, LANES), 0) % 8
    c = lax.broadcasted_iota(jnp.int32, (rows, LANES), 1)
    return c <= t


def _dsa_select_kernel(pt_ref, q_ref, w_ref, knew_ref, *rest, n_pg, n_sel):
    pages = rest[:n_pg]
    bias_ref, sm_ref = rest[n_pg], rest[n_pg + 1]
    st = pl.program_id(1)
    q = q_ref[...]
    wcol = w_ref[...]
    scale = (IDX_HEADS * IDX_DIM) ** -0.5

    def scores(keys_t):
        return _head_sum(jnp.maximum(_mm(q, keys_t), 0.0) * wcol, IDX_HEADS) * scale

    step_keys = n_pg * PAGE_SIZE
    col0 = pl.multiple_of(st * step_keys, step_keys)
    sm_ref[:, pl.ds(col0, step_keys)] = scores(_page_lanes(pages))

    @pl.when(st == pl.num_programs(1) - 1)
    def _():
        past = sm_ref.shape[1] - LANES
        sm_ref[:, past:] = jnp.where(_new_key_mask(8), scores(knew_ref[...]), NEG_INF)
        _topk_bias(sm_ref, bias_ref, n_sel)


def _dsa_select(page_table, q, wcol, knew, cache_idx, n_sel):
    bd, n_pages = page_table.shape
    n_pg = _pages_per_step(n_pages, SELECT_PAGES_PER_STEP)
    width = n_pages * PAGE_SIZE + LANES
    return pl.pallas_call(
        functools.partial(_dsa_select_kernel, n_pg=n_pg, n_sel=n_sel),
        grid_spec=pltpu.PrefetchScalarGridSpec(
            num_scalar_prefetch=1,
            grid=(bd, n_pages // n_pg),
            in_specs=[_per_seq(q.shape), _per_seq(wcol.shape), _per_seq(knew.shape)]
                     + _page_specs(cache_idx.shape, n_pg),
            out_specs=pl.BlockSpec((None, 8, width), lambda b, st, pt: (b, 0, 0)),
            scratch_shapes=[pltpu.VMEM((8, width), F32)],
        ),
        out_shape=jax.ShapeDtypeStruct((bd, 8, width), F32),
        compiler_params=_cparams("parallel", "arbitrary"),
        name="dsa_select",
    )(page_table, q, wcol, knew, *([cache_idx] * n_pg))


def _dsa_sample_kernel(pt_ref, q_ref, bias_ref, knew_ref, vnew_ref, *rest, n_pg):
    kpages, vpages = rest[:n_pg], rest[n_pg:2 * n_pg]
    o_ref, m_ref, l_ref, acc_ref = rest[2 * n_pg:]
    st = pl.program_id(1)

    @pl.when(st == 0)
    def _():
        _softmax_init(m_ref, l_ref, acc_ref)

    n_g, rg, _ = q_ref.shape
    n_rep = n_g * rg // 8

    def bias_rows(cols):
        bt = bias_ref[:, cols]
        return jnp.concatenate([bt] * n_rep, axis=0)

    step_keys = n_pg * PAGE_SIZE
    col0 = pl.multiple_of(st * step_keys, step_keys)
    _attend(q_ref, lambda g: _page_lanes(kpages, g), lambda g: _page_lanes(vpages, g), n_g, True,
            bias_rows(pl.ds(col0, step_keys)), m_ref, l_ref, acc_ref)

    @pl.when(st == pl.num_programs(1) - 1)
    def _():
        past = bias_ref.shape[1] - LANES
        _attend(q_ref, lambda g: knew_ref[g], lambda g: vnew_ref[g], n_g, True,
                bias_rows(slice(past, past + LANES)), m_ref, l_ref, acc_ref)
        o = acc_ref[...] / l_ref[...]
        for g in range(n_g):
            for j in range(rg // 8):
                lo = j * LANES + g * HEAD_DIM
                o_ref[:, lo:lo + HEAD_DIM] = o[g * rg + 8 * j:g * rg + 8 * j + 8]


def _dsa_sample(page_table, q, bias, knew, vnew, cache_k, cache_v):
    bd, n_pages = page_table.shape
    n_pg = _pages_per_step(n_pages, DSA_PAGES_PER_STEP)
    rows = q.shape[1] * q.shape[2]
    return pl.pallas_call(
        functools.partial(_dsa_sample_kernel, n_pg=n_pg),
        grid_spec=pltpu.PrefetchScalarGridSpec(
            num_scalar_prefetch=1,
            grid=(bd, n_pages // n_pg),
            in_specs=[_per_seq(q.shape), _per_seq(bias.shape), _per_seq(knew.shape), _per_seq(vnew.shape)]
                     + _page_specs(cache_k.shape, n_pg) + _page_specs(cache_v.shape, n_pg),
            out_specs=pl.BlockSpec((None, 8, 512), lambda b, st, pt: (b, 0, 0)),
            scratch_shapes=[pltpu.VMEM((rows, 1), F32), pltpu.VMEM((rows, 1), F32),
                            pltpu.VMEM((rows, HEAD_DIM), F32)],
        ),
        out_shape=jax.ShapeDtypeStruct((bd, 8, 512), F32),
        compiler_params=_cparams("parallel", "arbitrary"),
        name="dsa_sample",
    )(page_table, q, bias, knew, vnew, *([cache_k] * n_pg), *([cache_v] * n_pg))


def _diff_sample_kernel(pt_ref, lamv_ref, gsub_ref, q_ref, knew_ref, vnew_ref, *rest, n_pg, lam_init):
    kpages, vpages = rest[:n_pg], rest[n_pg:2 * n_pg]
    o_ref, m_ref, l_ref, acc_ref = rest[2 * n_pg:]
    st = pl.program_id(1)

    @pl.when(st == 0)
    def _():
        _softmax_init(m_ref, l_ref, acc_ref)

    n_k = q_ref.shape[0]

    def values(h):
        rows = pl.ds(h, PAGE_SIZE, stride=C_HEADS)
        return jnp.concatenate([p[rows, :].astype(BF16) for p in vpages], axis=0)

    _attend(q_ref, lambda i: _page_lanes(kpages, i // 2, i % 2), values, C_HEADS, False,
            0.0, m_ref, l_ref, acc_ref)

    @pl.when(st == pl.num_programs(1) - 1)
    def _():
        bias = jnp.where(_new_key_mask(8 * n_k), 0.0, NEG_INF)
        _attend(q_ref, lambda i: knew_ref[i], lambda h: vnew_ref[h], C_HEADS, False, bias, m_ref, l_ref, acc_ref)
        lam = _diff_lambda(lamv_ref, lam_init)
        o = acc_ref[...] / l_ref[...]
        for h in range(C_HEADS):
            oh = o[16 * h:16 * h + 8] - lam * o[16 * h + 8:16 * h + 16]
            o_ref[:, h * LANES:(h + 1) * LANES] = _rms(oh, gsub_ref[...]) * (1.0 - lam_init)


def _diff_sample(page_table, lamv, gsub, q, knew, vnew, cache_k, cache_v, lam_init):
    bd, n_pages = page_table.shape
    n_pg = _pages_per_step(n_pages, CD_PAGES_PER_STEP)
    rows = q.shape[1] * q.shape[2]
    cst = lambda arr: pl.BlockSpec(arr.shape, lambda b, st, pt: (0, 0))
    return pl.pallas_call(
        functools.partial(_diff_sample_kernel, n_pg=n_pg, lam_init=lam_init),
        grid_spec=pltpu.PrefetchScalarGridSpec(
            num_scalar_prefetch=1,
            grid=(bd, n_pages // n_pg),
            in_specs=[cst(lamv), cst(gsub), _per_seq(q.shape), _per_seq(knew.shape), _per_seq(vnew.shape)]
                     + _page_specs(cache_k.shape, n_pg) + _page_specs(cache_v.shape, n_pg),
            out_specs=pl.BlockSpec((None, 8, 512), lambda b, st, pt: (b, 0, 0)),
            scratch_shapes=[pltpu.VMEM((rows, 1), F32), pltpu.VMEM((rows, 1), F32),
                            pltpu.VMEM((rows, 2 * HEAD_DIM), F32)],
        ),
        out_shape=jax.ShapeDtypeStruct((bd, 8, 512), F32),
        compiler_params=_cparams("parallel", "arbitrary"),
        name="diff_sample",
    )(page_table, lamv, gsub, q, knew, vnew, *([cache_k] * n_pg), *([cache_v] * n_pg))


def _fox_sample_kernel(pt_ref, q_ref, lft_ref, knew_ref, vnew_ref, *rest, n_pg):
    kpages, vpages, fpages = rest[:n_pg], rest[n_pg:2 * n_pg], rest[2 * n_pg:3 * n_pg]
    o_ref, m_ref, l_ref, acc_ref, carry_ref = rest[3 * n_pg:]
    st = pl.program_id(1)

    @pl.when(st == 0)
    def _():
        _softmax_init(m_ref, l_ref, acc_ref)
        carry_ref[...] = jnp.zeros(carry_ref.shape, F32)

    n_h = q_ref.shape[0]
    cn64 = _rows8(_lane_prefix(lft_ref[...], _tri(incl_upper=True)) * LOG2E)
    mine = lax.broadcasted_iota(jnp.int32, cn64.shape, 1) == lax.broadcasted_iota(jnp.int32, cn64.shape, 0) % 8
    cn_col = jnp.sum(jnp.where(mine, cn64, 0.0), axis=1, keepdims=True)
    tri_after = _tri(strict_lower=True)

    carry = carry_ref[...]
    sfx = []
    for pg in range(n_pg):
        x = fpages[pg][...]
        sfx.append(_lane_prefix(x, tri_after) + carry)
        carry = carry + jnp.sum(x, axis=1, keepdims=True)
    carry_ref[...] = carry
    sfx = jnp.concatenate(sfx, axis=1)
    _attend(q_ref, lambda h: _page_lanes(kpages, h), lambda h: _page_lanes(vpages, h), n_h, True,
            cn_col + _rows8(sfx * LOG2E), m_ref, l_ref, acc_ref)

    @pl.when(st == pl.num_programs(1) - 1)
    def _():
        bias = jnp.where(_new_key_mask(8 * n_h), cn_col - cn64, NEG_INF)
        _attend(q_ref, lambda h: knew_ref[h], lambda h: vnew_ref[h], n_h, True, bias, m_ref, l_ref, acc_ref)
        o = acc_ref[...] / l_ref[...]
        for h in range(n_h):
            o_ref[:, h * HEAD_DIM:(h + 1) * HEAD_DIM] = o[8 * h:8 * h + 8]


def _fox_sample(page_table, q, lf_t, knew, vnew, cache_k, cache_v, cache_lf_t):
    bd, n_pages = page_table.shape
    n_pg = _pages_per_step(n_pages, CD_PAGES_PER_STEP)
    rows = q.shape[1] * q.shape[2]
    last = n_pages - 1
    return pl.pallas_call(
        functools.partial(_fox_sample_kernel, n_pg=n_pg),
        grid_spec=pltpu.PrefetchScalarGridSpec(
            num_scalar_prefetch=1,
            grid=(bd, n_pages // n_pg),
            in_specs=[_per_seq(q.shape), _per_seq(lf_t.shape), _per_seq(knew.shape), _per_seq(vnew.shape)]
                     + _page_specs(cache_k.shape, n_pg, last) + _page_specs(cache_v.shape, n_pg, last)
                     + _page_specs(cache_lf_t.shape, n_pg, last),
            out_specs=pl.BlockSpec((None, 8, 512), lambda b, st, pt: (b, 0, 0)),
            scratch_shapes=[pltpu.VMEM((rows, 1), F32), pltpu.VMEM((rows, 1), F32),
                            pltpu.VMEM((rows, HEAD_DIM), F32), pltpu.VMEM((8, 1), F32)],
        ),
        out_shape=jax.ShapeDtypeStruct((bd, 8, 512), F32),
        compiler_params=_cparams("parallel", "arbitrary"),
        name="fox_sample",
    )(page_table, q, lf_t, knew, vnew, *([cache_k] * n_pg), *([cache_v] * n_pg), *([cache_lf_t] * n_pg))


def _rope_tables(pos):
    rot = HEAD_DIM // 4
    half = rot // 2
    inv = ROPE_THETA ** (-np.arange(half, dtype=np.float64) / half)
    ang = np.asarray(pos, np.float64)[:, None] * inv[None, :]
    lane = np.arange(LANES) % HEAD_DIM
    c = np.ones((len(pos), LANES), np.float64)
    s1 = np.zeros_like(c)
    s2 = np.zeros_like(c)
    first, second = lane < half, (lane >= half) & (lane < rot)
    c[:, first] = np.cos(ang)[:, lane[first]]
    c[:, second] = np.cos(ang)[:, lane[second] - half]
    s1[:, first] = -np.sin(ang)[:, lane[first]]
    s2[:, second] = np.sin(ang)[:, lane[second] - half]
    return tuple(jnp.asarray(a, F32) for a in (c, s1, s2))


def _pack_w_ab(w):
    q, k, v, qi, ki, wi, u, gv = jnp.split(w, np.cumsum(
        (512, 128, 128, 512, IDX_DIM, IDX_HEADS, GM_WIDTH, GM_WIDTH))[:-1].tolist(), axis=1)
    q = q.reshape(D_MODEL, A_HEADS, HEAD_DIM)[:, np.array(A_HEAD_PERM)].reshape(D_MODEL, 512)
    pad = jnp.zeros((D_MODEL, LANES - IDX_DIM - IDX_HEADS), w.dtype)
    return jnp.concatenate([q, k, v, qi, ki, wi, pad, u, gv], axis=1).astype(BF16)


def _pack_w_cd(w):
    pad = jnp.zeros((D_MODEL, CD_COLS - w.shape[1]), w.dtype)
    return jnp.concatenate([w, pad], axis=1).astype(BF16)


def _spatial_blockdiag(w_s, b_s, tm, rows_per_chunk):
    wt = (w_s * jnp.tril(jnp.ones((CHUNK, CHUNK), w_s.dtype)))[:, :rows_per_chunk, :rows_per_chunk]
    n_blk = tm // rows_per_chunk
    eye = jnp.eye(n_blk, dtype=w_s.dtype)
    wsp = jnp.einsum("ab,gts->gatbs", eye, wt).reshape(GM_GROUPS, tm, tm).astype(BF16)
    bias = jnp.repeat(b_s[:, :rows_per_chunk].T, HEAD_DIM, axis=1)
    return wsp, jnp.tile(bias, (n_blk, 1))


def _pad_rows(x, rows):
    return jnp.pad(x, ((0, 0), (0, rows - x.shape[1]), (0, 0)))


def _cache_out(xt, lead, tail):
    n_seq, _, length = xt.shape
    x = jnp.moveaxis(xt.reshape((n_seq,) + tail + (length,)), -1, 1)
    return x.reshape((1,) + lead + tail)


def _position_minor(cache):
    return jnp.moveaxis(cache, 1, -1)


def _group_minor(x, bd, t_new, n_groups, width):
    x = x.reshape(bd, t_new, n_groups, width).transpose(0, 2, 3, 1)
    return jnp.pad(x, ((0, 0), (0, 0), (0, 0), (0, PAGE_SIZE - t_new)))


def _group_major(x, bd, t_new, n_groups, width, pad_to=None):
    x = x.reshape(bd, t_new, n_groups, width).transpose(0, 2, 1, 3)
    if pad_to is not None:
        x = jnp.pad(x, ((0, 0), (0, 0), (0, pad_to - t_new), (0, 0)))
    return x


def kernel(x_prompt, x_sample, cache_a_k, cache_a_v, cache_a_idx, cache_c_k, cache_c_v, cache_d_k, cache_d_v,
           cache_d_logf, page_table, c_prompt, c_sample, w_mod, b_mod, g_norm, w_in_ab, w_out_ab, gm_norm_v,
           gm_spatial_w, gm_spatial_b, w_in_cd, w_out_cd, fox_bias, lam_q1, lam_k1, lam_q2, lam_k2, g_subln,
           w_gate, w_up, w_down):
    b, s_len, _ = x_prompt.shape
    bd, t_new, _ = x_sample.shape
    n_pages = page_table.shape[1]
    past_len = n_pages * PAGE_SIZE
    n_pool = cache_a_k.shape[1]
    assert t_new == 8 and s_len % CHUNK == 0 and (bd * t_new) % 8 == 0
    mp, ms = b * s_len, bd * t_new

    mod = _modulation(jnp.concatenate([c_prompt, c_sample], axis=0), w_mod, b_mod)
    mod = mod.reshape(mod.shape[0], b + bd, 6, D_MODEL)

    def mods(layer):
        p = [mod[layer, :b, i].reshape(b, 1, D_MODEL) for i in range(6)]
        s = [jnp.repeat(mod[layer, b:, i], t_new, axis=0) for i in range(6)]
        return p, s

    tps_p = lambda tm: s_len // tm
    tps_s = lambda tm: 1
    rope_p = _rope_tables(np.arange(s_len))
    rope_s = _rope_tables(np.tile(past_len + np.arange(t_new), bd))
    gvec = lambda layer, i: g_norm[layer, i].reshape(1, D_MODEL)

    xp = x_prompt.reshape(mp, D_MODEL)
    xs = x_sample.reshape(ms, D_MODEL)
    pt = page_table.astype(jnp.int32)

    li = 0
    mod_p, mod_s = mods(0)
    w_ab = _pack_w_ab(w_in_ab[li])
    gm_g = gm_norm_v[li].reshape(1, GM_WIDTH)
    tm_p, tm_s = min(ROW_TILE, mp), min(ROW_TILE, ms)
    wsp_p, bsp_p = _spatial_blockdiag(gm_spatial_w[li], gm_spatial_b[li], tm_p, CHUNK)
    wsp_s, bsp_s = _spatial_blockdiag(gm_spatial_w[li], gm_spatial_b[li], tm_s, t_new)
    wo = w_out_ab[li]
    wo_a = wo[:512].reshape(A_HEADS, HEAD_DIM, D_MODEL)[np.array(A_HEAD_PERM)].reshape(512, D_MODEL).astype(BF16)
    wo_b = wo[512:].astype(BF16)
    n_sel_p = min(TOPK_MAX, s_len // 4)
    n_sel_s = min(TOPK_MAX, (past_len + t_new) // 4)

    (q_p, k32_p, v32_p, kiw_p, kbf_p, vbf_p, qi_p, kid_p, ob_p) = _inproj_ab(
        xp, gvec(0, 0), mod_p[0], mod_p[1], w_ab, rope_p, gm_g, wsp_p, bsp_p, tps_p(tm_p), False)
    (q_s, k32_s, v32_s, kiw_s, kbf_s, vbf_s, qi_s, kid_s, ob_s, gv_s) = _inproj_ab(
        xs, gvec(0, 0), mod_s[0], mod_s[1], w_ab, rope_s, gm_g, wsp_s, bsp_s, 1, True)

    r3 = lambda a: a.reshape(b, s_len, a.shape[-1])
    wi_rows = lambda kiw: kiw[:, IDX_DIM:IDX_DIM + IDX_HEADS].transpose(0, 2, 1).reshape(-1, IDX_HEADS)
    oa_p = _dsa_prompt(r3(qi_p), r3(wi_rows(kiw_p)), r3(kid_p), r3(q_p), r3(kbf_p), r3(vbf_p), n_sel_p)

    qi_stk = qi_s.reshape(bd, t_new, IDX_HEADS, IDX_DIM).transpose(0, 2, 1, 3).reshape(bd, 64, IDX_DIM)
    wi_col = wi_rows(kiw_s).reshape(bd, t_new, IDX_HEADS).transpose(0, 2, 1).reshape(bd, 64, 1)
    ki_new = _group_minor(kid_s[:, :IDX_DIM], bd, t_new, 1, IDX_DIM)[:, 0]
    sel_bias = _dsa_select(pt, qi_stk, wi_col, ki_new, _position_minor(cache_a_idx[li]), n_sel_s)

    q_stk = q_s.reshape(bd, t_new, 4, 2, HEAD_DIM).transpose(0, 3, 2, 1, 4)
    q_stk = q_stk.reshape(bd, A_KV_HEADS, 4 * t_new, HEAD_DIM)
    o_s = _dsa_sample(pt, q_stk, sel_bias,
                      _group_minor(kbf_s, bd, t_new, A_KV_HEADS, HEAD_DIM),
                      _group_minor(vbf_s, bd, t_new, A_KV_HEADS, HEAD_DIM),
                      _position_minor(cache_a_k[li]), _position_minor(cache_a_v[li]))
    oa_s = o_s.reshape(ms, 512).astype(BF16)

    def finish_layer(layer, a, b_, wa, wb, x, md, tps):
        wg, wu, wd = w_gate[layer].astype(BF16), w_up[layer].astype(BF16), w_down[layer].astype(BF16)
        return _mixer_out_ffn(a, b_, wa, wb, x, md[2], gvec(layer, 1), gvec(layer, 2), md[3], md[4],
                              wg, wu, wd, md[5], gvec(layer, 3), tps)

    xp = finish_layer(0, oa_p.reshape(mp, 512), ob_p, wo_a, wo_b, xp, mod_p, tps_p)
    xs = finish_layer(0, oa_s, ob_s, wo_a, wo_b, xs, mod_s, tps_s)

    gp, gs = (b, s_len), (bd, t_new)
    a_k_p = _cache_out(k32_p, gp, (A_KV_HEADS, HEAD_DIM))
    a_v_p = _cache_out(v32_p, gp, (A_KV_HEADS, HEAD_DIM))
    a_idx_p = _cache_out(kiw_p[:, :IDX_DIM], gp, (IDX_DIM,))
    a_k_s = _cache_out(k32_s, gs, (A_KV_HEADS, HEAD_DIM))
    a_v_s = _cache_out(v32_s, gs, (A_KV_HEADS, HEAD_DIM))
    a_idx_s = _cache_out(kiw_s[:, :IDX_DIM], gs, (IDX_DIM,))
    gm_v_s = gv_s.reshape(1, bd, t_new, GM_WIDTH)

    layer = 1
    mod_p, mod_s = mods(layer)
    lam_init = 0.8 - 0.6 * math.exp(-0.3 * layer)
    w_cd = _pack_w_cd(w_in_cd[li])
    fox_b = jnp.pad(fox_bias[li], (0, LANES - D_HEADS)).reshape(1, LANES)
    lamv = jnp.stack([lam_q1[li], lam_k1[li], lam_q2[li], lam_k2[li]])
    gsub = g_subln[li].reshape(1, 2 * HEAD_DIM)
    wo = w_out_cd[li]
    wo_c, wo_d = wo[:512].astype(BF16), wo[512:].astype(BF16)

    (cq_p, ck32_p, cv32_p, ckbf_p, cvbf_p, dq_p, dk32_p, dv32_p, dkbf_p, dvbf_p, lf_p) = _inproj_cd(
        xp, gvec(1, 0), mod_p[0], mod_p[1], w_cd, rope_p, fox_b, tps_p(tm_p))
    (cq_s, ck32_s, cv32_s, ckbf_s, cvbf_s, dq_s, dk32_s, dv32_s, dkbf_s, dvbf_s, lf_s) = _inproj_cd(
        xs, gvec(1, 0), mod_s[0], mod_s[1], w_cd, rope_s, fox_b, 1)

    oc_p = _diff_prompt(lamv, gsub, r3(cq_p), r3(ckbf_p), r3(cvbf_p), lam_init)
    cum_t = _cumsum_lanes(lf_p[:, :D_HEADS])
    od_p = _fox_prompt(r3(dq_p), r3(dkbf_p), r3(dvbf_p), cum_t.transpose(0, 2, 1), cum_t)

    oc_s = _diff_sample(pt, lamv, gsub, _group_major(cq_s, bd, t_new, 2 * C_HEADS, HEAD_DIM),
                        _group_minor(ckbf_s, bd, t_new, 2 * C_HEADS, HEAD_DIM),
                        _group_major(cvbf_s, bd, t_new, C_HEADS, 2 * HEAD_DIM, PAGE_SIZE),
                        _position_minor(cache_c_k[li]),
                        cache_c_v[li].reshape(-1, PAGE_SIZE * C_HEADS, 2 * HEAD_DIM), lam_init)

    d_logf_s = _cache_out(lf_s[:, :D_HEADS], gs, (D_HEADS,))
    logf_s = d_logf_s[0]
    lf_t = jnp.pad(logf_s.transpose(0, 2, 1), ((0, 0), (0, 0), (0, LANES - t_new)))
    od_s = _fox_sample(pt, _group_major(dq_s, bd, t_new, D_HEADS, HEAD_DIM), lf_t,
                       _group_minor(dkbf_s, bd, t_new, D_HEADS, HEAD_DIM),
                       _group_minor(dvbf_s, bd, t_new, D_HEADS, HEAD_DIM),
                       _position_minor(cache_d_k[li]), _position_minor(cache_d_v[li]),
                       _position_minor(cache_d_logf[li]))

    xp = finish_layer(1, oc_p.reshape(mp, 512), od_p.reshape(mp, 512), wo_c, wo_d, xp, mod_p, tps_p)
    xs = finish_layer(1, oc_s.reshape(ms, 512).astype(BF16), od_s.reshape(ms, 512).astype(BF16), wo_c, wo_d, xs,
                      mod_s, tps_s)

    return (xp.reshape(b, s_len, D_MODEL), xs.reshape(bd, t_new, D_MODEL),
            a_k_p, a_v_p, a_idx_p,
            _cache_out(ck32_p, gp, (C_HEADS, 2, HEAD_DIM)), cv32_p.reshape((1,) + gp + (C_HEADS, 2 * HEAD_DIM)),
            _cache_out(dk32_p, gp, (D_HEADS, HEAD_DIM)), _cache_out(dv32_p, gp, (D_HEADS, HEAD_DIM)),
            _cache_out(lf_p[:, :D_HEADS], gp, (D_HEADS,)),
            a_k_s, a_v_s, a_idx_s,
            _cache_out(ck32_s, gs, (C_HEADS, 2, HEAD_DIM)), cv32_s.reshape((1,) + gs + (C_HEADS, 2 * HEAD_DIM)),
            _cache_out(dk32_s, gs, (D_HEADS, HEAD_DIM)), _cache_out(dv32_s, gs, (D_HEADS, HEAD_DIM)),
            d_logf_s, gm_v_s)
```
